```python
import math
import jax
import jax.numpy as jnp
from jax import lax
import numpy as np


D_MODEL = 1024
BATCH = 8
SEQ = 4096
DEPTH = 1

MEM_LEN = 256
CONV_CH = D_MODEL
CONV_K = 3
POOL_WINDOWS = (2, 4, 8, 16)
N_POOL_GROUPS = 4
POOL_GROUP_DIM = D_MODEL // N_POOL_GROUPS
POOL_WIDTH = N_POOL_GROUPS * POOL_GROUP_DIM
X_HEADS = 4
X_HEAD_DIM = D_MODEL // X_HEADS
X_WIDTH = X_HEADS * X_HEAD_DIM
N_BRANCH = 3
IN_SPLITS = (CONV_CH, 2 * CONV_CH, 3 * CONV_CH, 3 * CONV_CH + POOL_WIDTH, 3 * CONV_CH + POOL_WIDTH + X_WIDTH)
IN_COLS = 3 * CONV_CH + POOL_WIDTH + X_WIDTH + N_BRANCH * D_MODEL
N_GROUPS = 8
EXPERTS_PER_GROUP = 8
N_EXPERTS = N_GROUPS * EXPERTS_PER_GROUP
TOP_K = 2
D_EXPERT = D_MODEL // 2
MOE_BLOCK = 128
ALPHA = (2.0 * DEPTH) ** 0.25
BETA = (8.0 * DEPTH) ** -0.25
LN_EPS = 1e-5

kernel_name = 'hybrid_conv_pool_memory_hmoe'


def layer_norm(x, g, b):
    xf = x.astype(jnp.float32)
    mu = jnp.mean(xf, axis=-1, keepdims=True)
    var = jnp.mean(jnp.square(xf - mu), axis=-1, keepdims=True)
    y = (xf - mu) * lax.rsqrt(var + LN_EPS)
    return (y * g.astype(jnp.float32) + b.astype(jnp.float32)).astype(x.dtype)


def short_conv_branch(b_gate, c_gate, h, conv_w, w_conv_out):
    u = c_gate * h
    v = lax.conv_general_dilated(
        u, conv_w[:, None, :].astype(u.dtype), window_strides=(1,), padding=[(CONV_K - 1, 0)],
        dimension_numbers=('NWC', 'WIO', 'NWC'), feature_group_count=CONV_CH)
    return jnp.einsum('bsc,cd->bsd', b_gate * v, w_conv_out)


def multiscale_pool_branch(p, w_pool, pool_scale):
    bn, s, _ = p.shape
    pf = p.reshape(bn, s, N_POOL_GROUPS, POOL_GROUP_DIM).astype(jnp.float32)
    cs = jnp.cumsum(pf, axis=1)
    cs = jnp.concatenate([jnp.zeros_like(cs[:, :1]), cs], axis=1)
    pos = jnp.arange(1, s + 1, dtype=jnp.float32)
    means = []
    for gi, w in enumerate(POOL_WINDOWS):
        upper = cs[:, 1:, gi]
        lower = jnp.concatenate([jnp.zeros_like(cs[:, :w - 1, gi]), cs[:, :s - w + 1, gi]], axis=1)
        count = jnp.minimum(pos, float(w))[None, :, None]
        means.append((upper - lower) / count)
    d = (jnp.stack(means, axis=2) - pf).astype(p.dtype)
    y = jnp.einsum('bsgc,gcd->bsgd', d, w_pool).reshape(bn, s, POOL_WIDTH)
    return y * pool_scale


def memory_cross_attention(q, mem, w_kv, w_xo):
    bn, s, _ = q.shape
    m = mem.shape[1]
    kv = jnp.einsum('bmd,de->bme', mem, w_kv)
    k, v = jnp.split(kv, 2, axis=-1)
    qh = q.reshape(bn, s, X_HEADS, X_HEAD_DIM)
    kh = k.reshape(bn, m, X_HEADS, X_HEAD_DIM)
    vh = v.reshape(bn, m, X_HEADS, X_HEAD_DIM)
    sc = jnp.einsum('bshd,bmhd->bhsm', qh, kh).astype(jnp.float32) * (X_HEAD_DIM ** -0.5)
    a = jax.nn.softmax(sc, axis=-1).astype(q.dtype)
    o = jnp.einsum('bhsm,bmhd->bshd', a, vh).reshape(bn, s, X_WIDTH)
    return jnp.einsum('bse,ed->bsd', o, w_xo)


def hybrid_mixer(x, mem, w_in, b_gate, conv_w, w_conv_out, w_pool, pool_scale, w_kv, w_xo, w_o):
    bn, s, _ = x.shape
    z = jnp.einsum('bsd,de->bse', x, w_in)
    cb, cc, ch, pz, q, gz = jnp.split(z, IN_SPLITS, axis=-1)
    gates = jax.nn.sigmoid((gz + b_gate).astype(jnp.float32)).astype(x.dtype).reshape(bn, s, N_BRANCH, D_MODEL)
    y_conv = short_conv_branch(cb, cc, ch, conv_w, w_conv_out)
    y_pool = multiscale_pool_branch(pz, w_pool, pool_scale)
    y_mem = memory_cross_attention(q, mem, w_kv, w_xo)
    merged = gates[:, :, 0] * y_conv + gates[:, :, 1] * y_pool + gates[:, :, 2] * y_mem
    return jnp.einsum('bsd,de->bse', merged, w_o)


def hierarchical_moe(x, w_rg, b_rg, w_re, b_re, w_up, w_down):
    bn, s, d = x.shape
    n = bn * s
    xt = x.reshape(n, d)
    g_prob = jax.nn.softmax(jnp.einsum('nd,dg->ng', xt, w_rg).astype(jnp.float32) + b_rg.astype(jnp.float32), axis=-1)
    g_sel = jnp.argmax(g_prob, axis=-1)
    g_w = jnp.take_along_axis(g_prob, g_sel[:, None], axis=-1)
    e_all = (jnp.einsum('nd,de->ne', xt, w_re).astype(jnp.float32) + b_re.astype(jnp.float32)).reshape(n, N_GROUPS, EXPERTS_PER_GROUP)
    e_logits = jnp.take_along_axis(e_all, g_sel[:, None, None], axis=1)[:, 0]
    top_l, top_i = lax.top_k(e_logits, TOP_K)
    top_w = jax.nn.softmax(top_l, axis=-1) * g_w
    expert_id = (g_sel[:, None] * EXPERTS_PER_GROUP + top_i).reshape(-1)
    a_n = n * TOP_K
    n_blocks = (a_n + N_EXPERTS * (MOE_BLOCK - 1) + MOE_BLOCK - 1) // MOE_BLOCK
    p_rows = n_blocks * MOE_BLOCK
    order = jnp.argsort(expert_id)
    sorted_e = expert_id[order]
    counts = jnp.bincount(expert_id, length=N_EXPERTS)
    starts = jnp.cumsum(counts) - counts
    padded = (counts + MOE_BLOCK - 1) // MOE_BLOCK * MOE_BLOCK
    pends = jnp.cumsum(padded)
    pstarts = pends - padded
    dest_sorted = pstarts[sorted_e] + (jnp.arange(a_n) - starts[sorted_e])
    dest = jnp.zeros((a_n,), dtype=dest_sorted.dtype).at[order].set(dest_sorted)
    tok = jnp.arange(a_n) // TOP_K
    x_buf = jnp.zeros((p_rows, d), dtype=x.dtype).at[dest].set(xt[tok])
    block_e = jnp.minimum(jnp.searchsorted(pends, jnp.arange(n_blocks) * MOE_BLOCK, side='right'), N_EXPERTS - 1)

    def expert_block(args):
        xb, e = args
        hg, hv = jnp.split(xb @ w_up[e], 2, axis=-1)
        return (jax.nn.silu(hg) * hv) @ w_down[e]

    y_buf = lax.map(expert_block, (x_buf.reshape(n_blocks, MOE_BLOCK, d), block_e))
    y_assign = y_buf.reshape(p_rows, d)[dest].reshape(n, TOP_K, d)
    y = jnp.einsum('nkd,nk->nd', y_assign, top_w.astype(x.dtype))
    return y.reshape(bn, s, d)


def setup_inputs(seed: int = 0) -> dict:
    key = jax.random.key(seed)
    ks = jax.random.split(key, 24)
    L = DEPTH
    nrm = lambda k, shape: jax.random.normal(k, shape, dtype=jnp.float32)
    return {
        'x': nrm(ks[0], (BATCH, SEQ, D_MODEL)),
        'mem': nrm(ks[1], (BATCH, MEM_LEN, D_MODEL)),
        'w_in': nrm(ks[2], (L, D_MODEL, IN_COLS)) * D_MODEL ** -0.5,
        'b_gate': 0.1 * nrm(ks[3], (L, N_BRANCH * D_MODEL)),
        'conv_w': nrm(ks[4], (L, CONV_K, CONV_CH)) * CONV_K ** -0.5,
        'w_conv_out': nrm(ks[5], (L, CONV_CH, D_MODEL)) * CONV_CH ** -0.5,
        'w_pool': nrm(ks[6], (L, N_POOL_GROUPS, POOL_GROUP_DIM, POOL_GROUP_DIM)) * POOL_GROUP_DIM ** -0.5,
        'pool_scale': 1.0 + 0.1 * nrm(ks[7], (L, POOL_WIDTH)),
        'w_kv': nrm(ks[8], (L, D_MODEL, 2 * X_WIDTH)) * D_MODEL ** -0.5,
        'w_xo': nrm(ks[9], (L, X_WIDTH, D_MODEL)) * X_WIDTH ** -0.5,
        'w_o': nrm(ks[10], (L, D_MODEL, D_MODEL)) * (D_MODEL ** -0.5 * BETA),
        'ln1_g': 1.0 + 0.02 * nrm(ks[11], (L, D_MODEL)),
        'ln1_b': 0.02 * nrm(ks[12], (L, D_MODEL)),
        'w_router_group': nrm(ks[13], (L, D_MODEL, N_GROUPS)) * D_MODEL ** -0.5,
        'b_router_group': 0.01 * nrm(ks[14], (L, N_GROUPS)),
        'w_router_expert': nrm(ks[15], (L, D_MODEL, N_EXPERTS)) * D_MODEL ** -0.5,
        'b_router_expert': 0.01 * nrm(ks[16], (L, N_EXPERTS)),
        'w_up': nrm(ks[17], (L, N_EXPERTS, D_MODEL, 2 * D_EXPERT)) * D_MODEL ** -0.5,
        'w_down': nrm(ks[18], (L, N_EXPERTS, D_EXPERT, D_MODEL)) * (D_EXPERT ** -0.5 * BETA),
        'ln2_g': 1.0 + 0.02 * nrm(ks[19], (L, D_MODEL)),
        'ln2_b': 0.02 * nrm(ks[20], (L, D_MODEL)),
    }


def reference(x, mem, w_in, b_gate, conv_w, w_conv_out, w_pool, pool_scale, w_kv, w_xo, w_o, ln1_g, ln1_b,
              w_router_group, b_router_group, w_router_expert, b_router_expert, w_up, w_down, ln2_g, ln2_b):
    h = x
    for l in range(DEPTH):
        mix = hybrid_mixer(h, mem, w_in[l], b_gate[l], conv_w[l], w_conv_out[l], w_pool[l], pool_scale[l],
                           w_kv[l], w_xo[l], w_o[l])
        h = layer_norm(ALPHA * h + mix, ln1_g[l], ln1_b[l])
        ffn = hierarchical_moe(h, w_router_group[l], b_router_group[l], w_router_expert[l], b_router_expert[l],
                               w_up[l], w_down[l])
        h = layer_norm(ALPHA * h + ffn, ln2_g[l], ln2_b[l])
    return h
```

```python
import functools

import jax
import jax.numpy as jnp
from jax import lax
from jax.experimental import pallas as pl
from jax.experimental.pallas import tpu as pltpu

D_MODEL = 1024
CONV_K = 3
POOL_WINDOWS = (2, 4, 8, 16)
POOL_GROUP_DIM = 256
X_HEADS = 4
X_HEAD_DIM = 256
N_GROUPS = 8
EXPERTS_PER_GROUP = 8
N_EXPERTS = 64
TOP_K = 2
D_EXPERT = 512
DEPTH = 1
ALPHA = (2.0 * DEPTH) ** 0.25
LN_EPS = 1e-5

LANES = 128
POOL_HALO = 16
CONV_HALO = 8
TM_MIX = 512
TM_EXP = 256
TM_SCT = 512
TM_CMB = 256
VMEM_LIMIT = 58 * 1024 * 1024

F32 = jnp.float32
BF16 = jnp.bfloat16


def _dot(a, b):
    return jnp.dot(a, b, preferred_element_type=F32)


def _layer_norm(h, g, b):
    mu = jnp.mean(h, axis=-1, keepdims=True)
    c = h - mu
    var = jnp.mean(c * c, axis=-1, keepdims=True)
    return c * lax.rsqrt(var + LN_EPS) * g + b


def _kv_kernel(mem_ref, wkt_ref, wv_ref, kt_ref, v_ref):
    mb = mem_ref[0].astype(BF16)
    kt = lax.dot_general(wkt_ref[...], mb, (((1,), (1,)), ((), ())), preferred_element_type=F32)
    kt_ref[0] = kt.astype(BF16)
    v_ref[0] = _dot(mb, wv_ref[...]).astype(BF16)


def _kv_proj(mem, wkt, wv):
    bn, m, d = mem.shape
    return pl.pallas_call(
        _kv_kernel,
        grid=(bn,),
        in_specs=[
            pl.BlockSpec((1, m, d), lambda b: (b, 0, 0)),
            pl.BlockSpec((d, d), lambda b: (0, 0)),
            pl.BlockSpec((d, d), lambda b: (0, 0)),
        ],
        out_specs=[
            pl.BlockSpec((1, d, m), lambda b: (b, 0, 0)),
            pl.BlockSpec((1, m, d), lambda b: (b, 0, 0)),
        ],
        out_shape=[jax.ShapeDtypeStruct((bn, d, m), BF16), jax.ShapeDtypeStruct((bn, m, d), BF16)],
        compiler_params=pltpu.CompilerParams(dimension_semantics=("arbitrary",)),
        name="kv_proj",
    )(mem, wkt, wv)


def _mixer_kernel(x_ref, win_ref, bg_ref, cw_ref, wco_ref, wpool_ref, ps_ref, kt_ref, v_ref, wxo_ref, wo_ref,
                  g1_ref, b1_ref, wrh_ref, wrl_ref, br_ref,
                  h1_ref, route_ref, cnt_ref,
                  ubuf, pbuf, carry):
    tm = TM_MIX
    d = D_MODEL
    b = pl.program_id(0)
    s = pl.program_id(1)

    @pl.when(s == 0)
    def _():
        ubuf[0:CONV_HALO, :] = jnp.zeros((CONV_HALO, d), F32)
        pbuf[0:POOL_HALO, :] = jnp.zeros((POOL_HALO, d), F32)

    @pl.when((s == 0) & (b == 0))
    def _():
        carry[...] = jnp.zeros_like(carry)

    x = x_ref[...]
    xb = x.astype(BF16)

    def proj(sec):
        return _dot(xb, win_ref[:, sec * d:(sec + 1) * d])

    def gate(i):
        return jax.nn.sigmoid(proj(5 + i) + bg_ref[i:i + 1, :])

    u = proj(1) * proj(2)
    ubuf[CONV_HALO:CONV_HALO + tm, :] = u
    v = cw_ref[CONV_K - 1:CONV_K, :] * u
    for k in range(CONV_K - 1):
        off = CONV_HALO - (CONV_K - 1) + k
        v = v + cw_ref[k:k + 1, :] * ubuf[off:off + tm, :]
    ubuf[0:CONV_HALO, :] = ubuf[tm:tm + CONV_HALO, :]
    y_conv = _dot((proj(0) * v).astype(BF16), wco_ref[...])
    acc = gate(0) * y_conv

    p = proj(3)
    pbuf[POOL_HALO:POOL_HALO + tm, :] = p
    pos = (s * tm + 1 + lax.broadcasted_iota(jnp.int32, (tm, 1), 0)).astype(F32)
    ys = []
    for gi, w in enumerate(POOL_WINDOWS):
        c0 = gi * POOL_GROUP_DIM
        c1 = c0 + POOL_GROUP_DIM
        pg = p[:, c0:c1]
        ws = pg
        for j in range(1, w):
            ws = ws + pbuf[POOL_HALO - j:POOL_HALO - j + tm, c0:c1]
        inv_cnt = 1.0 / jnp.minimum(pos, float(w))
        dg = (ws * inv_cnt - pg).astype(BF16)
        ys.append(_dot(dg, wpool_ref[gi]))
    pbuf[0:POOL_HALO, :] = pbuf[tm:tm + POOL_HALO, :]
    y_pool = jnp.concatenate(ys, axis=1) * ps_ref[...]
    acc = acc + gate(1) * y_pool

    qb = proj(4).astype(BF16)
    scale = X_HEAD_DIM ** -0.5
    os_ = []
    for h in range(X_HEADS):
        c0 = h * X_HEAD_DIM
        c1 = c0 + X_HEAD_DIM
        sc = _dot(qb[:, c0:c1], kt_ref[0, c0:c1, :]) * scale
        e = jnp.exp(sc - jnp.max(sc, axis=-1, keepdims=True))
        a = e * (1.0 / jnp.sum(e, axis=-1, keepdims=True))
        os_.append(_dot(a.astype(BF16), v_ref[0, :, c0:c1]))
    y_mem = _dot(jnp.concatenate(os_, axis=1).astype(BF16), wxo_ref[...])
    acc = acc + gate(2) * y_mem

    mix = _dot(acc.astype(BF16), wo_ref[...])
    h1 = _layer_norm(ALPHA * x + mix, g1_ref[...], b1_ref[...])
    h1_ref[...] = h1

    hh = h1.astype(BF16)
    hl = (h1 - hh.astype(F32)).astype(BF16)
    logits = _dot(hh, wrh_ref[...]) + _dot(hl, wrh_ref[...]) + _dot(hh, wrl_ref[...]) + br_ref[...]
    lane = lax.broadcasted_iota(jnp.int32, (tm, LANES), 1)
    lane_f = lane.astype(F32)
    neg = jnp.float32(-jnp.inf)
    big = jnp.float32(1e9)

    def first_argmax(vals):
        m = jnp.max(vals, axis=-1, keepdims=True)
        idx = jnp.min(jnp.where(vals == m, lane_f, big), axis=-1, keepdims=True)
        return m, idx

    is_g = (lane >= N_EXPERTS) & (lane < N_EXPERTS + N_GROUPS)
    gmax, gidx = first_argmax(jnp.where(is_g, logits, neg))
    g_w = 1.0 / jnp.sum(jnp.where(is_g, jnp.exp(logits - gmax), 0.0), axis=-1, keepdims=True)
    gsel = gidx.astype(jnp.int32) - N_EXPERTS
    in_grp = (lane >> 3) == gsel
    le = jnp.where(in_grp, logits, neg)
    m1, i1 = first_argmax(le)
    m2, i2 = first_argmax(jnp.where(lane_f == i1, neg, le))
    t = jnp.exp(m2 - m1)
    den = 1.0 + t
    w1 = g_w / den
    w2 = g_w * t / den

    sel1 = lane_f == i1
    sel2 = lane_f == i2
    onehot = (sel1 | sel2).astype(BF16)
    row_i = lax.broadcasted_iota(jnp.int32, (tm, tm), 0)
    col_i = lax.broadcasted_iota(jnp.int32, (tm, tm), 1)
    before = (col_i < row_i).astype(BF16)
    prior = _dot(before, onehot) + carry[0:1, :]
    r1 = jnp.sum(jnp.where(sel1, prior, 0.0), axis=-1, keepdims=True)
    r2 = jnp.sum(jnp.where(sel2, prior, 0.0), axis=-1, keepdims=True)
    carry[...] = carry[...] + jnp.sum(onehot.astype(F32), axis=0, keepdims=True)
    cnt_ref[...] = carry[...].astype(jnp.int32)

    route = jnp.where(lane == 0, i1, 0.0)
    for k, col in enumerate((i2, w1, w2, r1, r2), start=1):
        route = jnp.where(lane == k, col, route)
    route_ref[...] = route


def _mixer(x2, w_in_b, b_gate3, conv_w, wco_b, wpool_b, pool_scale, kt, v, wxo_b, wo_b, ln_g, ln_b, wr_hi, wr_lo, br,
           bn, seq):
    n, d = x2.shape
    tm = TM_MIX
    spb = seq // tm
    const2 = lambda b, s: (0, 0)
    const3 = lambda b, s: (0, 0, 0)
    one = pl.Buffered(1)
    row_map = lambda b, s: (b * spb + s, 0)
    return pl.pallas_call(
        _mixer_kernel,
        grid=(bn, spb),
        in_specs=[
            pl.BlockSpec((tm, d), row_map),
            pl.BlockSpec(w_in_b.shape, const2, pipeline_mode=one),
            pl.BlockSpec(b_gate3.shape, const2, pipeline_mode=one),
            pl.BlockSpec(conv_w.shape, const2, pipeline_mode=one),
            pl.BlockSpec(wco_b.shape, const2, pipeline_mode=one),
            pl.BlockSpec(wpool_b.shape, const3, pipeline_mode=one),
            pl.BlockSpec(pool_scale.shape, const2, pipeline_mode=one),
            pl.BlockSpec((1,) + kt.shape[1:], lambda b, s: (b, 0, 0)),
            pl.BlockSpec((1,) + v.shape[1:], lambda b, s: (b, 0, 0)),
            pl.BlockSpec(wxo_b.shape, const2, pipeline_mode=one),
            pl.BlockSpec(wo_b.shape, const2, pipeline_mode=one),
            pl.BlockSpec(ln_g.shape, const2, pipeline_mode=one),
            pl.BlockSpec(ln_b.shape, const2, pipeline_mode=one),
            pl.BlockSpec(wr_hi.shape, const2, pipeline_mode=one),
            pl.BlockSpec(wr_lo.shape, const2, pipeline_mode=one),
            pl.BlockSpec(br.shape, const2, pipeline_mode=one),
        ],
        out_specs=[
            pl.BlockSpec((tm, d), row_map),
            pl.BlockSpec((tm, LANES), row_map),
            pl.BlockSpec((8, LANES), const2),
        ],
        out_shape=[
            jax.ShapeDtypeStruct((n, d), F32),
            jax.ShapeDtypeStruct((n, LANES), F32),
            jax.ShapeDtypeStruct((8, LANES), jnp.int32),
        ],
        scratch_shapes=[
            pltpu.VMEM((tm + CONV_HALO, d), F32),
            pltpu.VMEM((tm + POOL_HALO, d), F32),
            pltpu.VMEM((8, LANES), F32),
        ],
        compiler_params=pltpu.CompilerParams(
            dimension_semantics=("arbitrary", "arbitrary"), vmem_limit_bytes=VMEM_LIMIT),
        name="mixer",
    )(x2, w_in_b, b_gate3, conv_w, wco_b, wpool_b, pool_scale, kt, v, wxo_b, wo_b, ln_g, ln_b, wr_hi, wr_lo, br)


def _row_copy(src, src_row, dst, dst_row, sem):
    return pltpu.make_async_copy(src.at[pl.ds(src_row, 1), :], dst.at[pl.ds(dst_row, 1), :], sem)


def _scatter_kernel(zt_ref, d1_ref, d2_ref, h_ref, xbuf_ref, zbuf, sem):
    i = pl.program_id(0)
    tm = TM_SCT

    def zero_copy(e):
        start = pl.multiple_of(zt_ref[e], TM_EXP)
        return pltpu.make_async_copy(zbuf, xbuf_ref.at[pl.ds(start, TM_EXP), :], sem.at[2])

    @pl.when(i == 0)
    def _():
        zbuf[...] = jnp.zeros_like(zbuf)

        def start(e, c):
            @pl.when(zt_ref[e] >= 0)
            def _():
                zero_copy(e).start()
            return c

        def wait(e, c):
            @pl.when(zt_ref[e] >= 0)
            def _():
                zero_copy(e).wait()
            return c

        lax.fori_loop(0, 2 * N_EXPERTS, start, 0)
        lax.fori_loop(0, 2 * N_EXPERTS, wait, 0)

    def issue(r, c):
        _row_copy(h_ref, r, xbuf_ref, d1_ref[0, 0, r], sem.at[0]).start()
        _row_copy(h_ref, r, xbuf_ref, d2_ref[0, 0, r], sem.at[1]).start()
        return c

    lax.fori_loop(0, tm, issue, 0, unroll=8)
    pltpu.make_async_copy(h_ref, xbuf_ref.at[pl.ds(0, tm), :], sem.at[0]).wait()
    pltpu.make_async_copy(h_ref, xbuf_ref.at[pl.ds(0, tm), :], sem.at[1]).wait()


def _scatter(zero_tiles, dest1, dest2, h1, p_rows):
    n, d = h1.shape
    tm = TM_SCT
    nt = n // tm
    idx_spec = pl.BlockSpec((1, 1, tm), lambda i, zt: (i, 0, 0), memory_space=pltpu.SMEM)
    return pl.pallas_call(
        _scatter_kernel,
        grid_spec=pltpu.PrefetchScalarGridSpec(
            num_scalar_prefetch=1,
            grid=(nt,),
            in_specs=[idx_spec, idx_spec, pl.BlockSpec((tm, d), lambda i, zt: (i, 0))],
            out_specs=pl.BlockSpec(memory_space=pl.ANY),
            scratch_shapes=[pltpu.VMEM((TM_EXP, d), F32), pltpu.SemaphoreType.DMA((3,))],
        ),
        out_shape=jax.ShapeDtypeStruct((p_rows, d), F32),
        compiler_params=pltpu.CompilerParams(dimension_semantics=("arbitrary",)),
        name="scatter_rows",
    )(zero_tiles, dest1.reshape(nt, 1, tm), dest2.reshape(nt, 1, tm), h1)


def _expert_kernel(te_ref, nt_ref, x_ref, wup_ref, wdn_ref, y_ref):
    i = pl.program_id(0)

    @pl.when(i >= nt_ref[0])
    def _():
        y_ref[...] = jnp.zeros_like(y_ref)

    @pl.when(i < nt_ref[0])
    def _():
        xb = x_ref[...].astype(BF16)
        hgv = _dot(xb, wup_ref[0])
        hg = hgv[:, :D_EXPERT]
        hv = hgv[:, D_EXPERT:]
        act = (hg * jax.nn.sigmoid(hg)) * hv
        y_ref[...] = _dot(act.astype(BF16), wdn_ref[0])


def _experts(tile_expert, n_tiles, xbuf, w_up_b, w_down_b):
    p_rows, d = xbuf.shape
    tm = TM_EXP
    row_map = lambda i, te, nt: (i, 0)
    w_map = lambda i, te, nt: (te[i], 0, 0)
    return pl.pallas_call(
        _expert_kernel,
        grid_spec=pltpu.PrefetchScalarGridSpec(
            num_scalar_prefetch=2,
            grid=(p_rows // tm,),
            in_specs=[
                pl.BlockSpec((tm, d), row_map),
                pl.BlockSpec((1,) + w_up_b.shape[1:], w_map),
                pl.BlockSpec((1,) + w_down_b.shape[1:], w_map),
            ],
            out_specs=pl.BlockSpec((tm, d), row_map),
        ),
        out_shape=jax.ShapeDtypeStruct((p_rows, d), F32),
        compiler_params=pltpu.CompilerParams(dimension_semantics=("arbitrary",), vmem_limit_bytes=VMEM_LIMIT),
        name="experts",
    )(tile_expert, n_tiles, xbuf, w_up_b, w_down_b)


def _combine_kernel(d1_ref, d2_ref, h_ref, route_ref, ybuf_ref, g_ref, b_ref, out_ref, ya, yb, sem):
    tm = TM_CMB

    def issue(r, c):
        _row_copy(ybuf_ref, d1_ref[0, 0, r], ya, r, sem.at[0]).start()
        _row_copy(ybuf_ref, d2_ref[0, 0, r], yb, r, sem.at[1]).start()
        return c

    lax.fori_loop(0, tm, issue, 0, unroll=8)
    pltpu.make_async_copy(ybuf_ref.at[pl.ds(0, tm), :], ya, sem.at[0]).wait()
    pltpu.make_async_copy(ybuf_ref.at[pl.ds(0, tm), :], yb, sem.at[1]).wait()
    route = route_ref[...]
    ffn = route[:, 2:3] * ya[...] + route[:, 3:4] * yb[...]
    out_ref[...] = _layer_norm(ALPHA * h_ref[...] + ffn, g_ref[...], b_ref[...])


def _combine(dest1, dest2, h1, route, ybuf, ln_g, ln_b):
    n, d = h1.shape
    tm = TM_CMB
    nt = n // tm
    idx_spec = pl.BlockSpec((1, 1, tm), lambda i: (i, 0, 0), memory_space=pltpu.SMEM)
    return pl.pallas_call(
        _combine_kernel,
        grid=(nt,),
        in_specs=[
            idx_spec, idx_spec,
            pl.BlockSpec((tm, d), lambda i: (i, 0)),
            pl.BlockSpec((tm, LANES), lambda i: (i, 0)),
            pl.BlockSpec(memory_space=pl.ANY),
            pl.BlockSpec(ln_g.shape, lambda i: (0, 0)),
            pl.BlockSpec(ln_b.shape, lambda i: (0, 0)),
        ],
        out_specs=pl.BlockSpec((tm, d), lambda i: (i, 0)),
        out_shape=jax.ShapeDtypeStruct((n, d), F32),
        scratch_shapes=[pltpu.VMEM((tm, d), F32), pltpu.VMEM((tm, d), F32), pltpu.SemaphoreType.DMA((2,))],
        compiler_params=pltpu.CompilerParams(dimension_semantics=("arbitrary",)),
        name="combine_ln2",
    )(dest1.reshape(nt, 1, tm), dest2.reshape(nt, 1, tm), h1, route, ybuf, ln_g, ln_b)


def _layer(h, mem, w_in, b_gate, conv_w, w_conv_out, w_pool, pool_scale, w_kv, w_xo, w_o, ln1_g, ln1_b,
           w_rg, b_rg, w_re, b_re, w_up, w_down, ln2_g, ln2_b):
    bn, seq, d = h.shape
    n = bn * seq
    xw = X_HEADS * X_HEAD_DIM

    kt, v = _kv_proj(mem, w_kv[:, :xw].T.astype(BF16), w_kv[:, xw:].astype(BF16))

    pad = LANES - N_EXPERTS - N_GROUPS
    w_r = jnp.concatenate([w_re, w_rg, jnp.zeros((d, pad), F32)], axis=1)
    wr_hi = w_r.astype(BF16)
    wr_lo = (w_r - wr_hi.astype(F32)).astype(BF16)
    b_r = jnp.concatenate([b_re, b_rg, jnp.zeros((pad,), F32)])[None, :]

    h1, route, cnt = _mixer(
        h.reshape(n, d), w_in.astype(BF16), b_gate.reshape(3, d), conv_w, w_conv_out.astype(BF16),
        w_pool.astype(BF16), pool_scale[None, :], kt, v, w_xo.astype(BF16), w_o.astype(BF16),
        ln1_g[None, :], ln1_b[None, :], wr_hi, wr_lo, b_r, bn, seq)

    counts = cnt[0, :N_EXPERTS]
    padded = (counts + TM_EXP - 1) // TM_EXP * TM_EXP
    pends = jnp.cumsum(padded)
    pstarts = pends - padded
    max_tiles = (n * TOP_K + N_EXPERTS * (TM_EXP - 1)) // TM_EXP
    p_rows = max_tiles * TM_EXP
    n_tiles = (pends[-1] // TM_EXP).astype(jnp.int32)
    tile_ids = jnp.minimum(jnp.arange(max_tiles, dtype=jnp.int32), n_tiles - 1)
    tile_expert = jnp.minimum(
        jnp.searchsorted(pends, tile_ids * TM_EXP, side='right'), N_EXPERTS - 1).astype(jnp.int32)
    tail_ids = n_tiles + jnp.arange(N_EXPERTS, dtype=jnp.int32)
    zero_tiles = jnp.concatenate([
        jnp.where(counts % TM_EXP != 0, pends - TM_EXP, -1),
        jnp.where(tail_ids < max_tiles, tail_ids * TM_EXP, -1)]).astype(jnp.int32)
    ri = route[:, :8]
    e1 = ri[:, 0].astype(jnp.int32)
    e2 = ri[:, 1].astype(jnp.int32)
    dest1 = (pstarts[e1] + ri[:, 4].astype(jnp.int32)).astype(jnp.int32)
    dest2 = (pstarts[e2] + ri[:, 5].astype(jnp.int32)).astype(jnp.int32)

    xbuf = _scatter(zero_tiles, dest1, dest2, h1, p_rows)
    ybuf = _experts(tile_expert, n_tiles.reshape(1), xbuf, w_up.astype(BF16), w_down.astype(BF16))
    out = _combine(dest1, dest2, h1, route, ybuf, ln2_g[None, :], ln2_b[None, :])
    return out.reshape(bn, seq, d)


def kernel(x, mem, w_in, b_gate, conv_w, w_conv_out, w_pool, pool_scale, w_kv, w_xo, w_o, ln1_g, ln1_b,
           w_router_group, b_router_group, w_router_expert, b_router_expert, w_up, w_down, ln2_g, ln2_b):
    h = x
    for l in range(DEPTH):
        h = _layer(h, mem, w_in[l], b_gate[l], conv_w[l], w_conv_out[l], w_pool[l], pool_scale[l], w_kv[l],
                   w_xo[l], w_o[l], ln1_g[l], ln1_b[l], w_router_group[l], b_router_group[l],
                   w_router_expert[l], b_router_expert[l], w_up[l], w_down[l], ln2_g[l], ln2_b[l])
    return h
```

```python
import functools

import jax
import jax.numpy as jnp
from jax import lax
from jax.experimental import pallas as pl
from jax.experimental.pallas import tpu as pltpu

D_MODEL = 1024
CONV_K = 3
POOL_WINDOWS = (2, 4, 8, 16)
POOL_GROUP_DIM = 256
X_HEADS = 4
X_HEAD_DIM = 256
N_GROUPS = 8
EXPERTS_PER_GROUP = 8
N_EXPERTS = 64
TOP_K = 2
D_EXPERT = 512
DEPTH = 1
ALPHA = (2.0 * DEPTH) ** 0.25
LN_EPS = 1e-5

LANES = 128
POOL_HALO = 16
CONV_HALO = 8
TM_MIX = 512
TM_EXP = 256
TM_ROW = TM_MIX
VMEM_LIMIT = 58 * 1024 * 1024

F32 = jnp.float32
BF16 = jnp.bfloat16


def _dot(a, b):
    return jnp.dot(a, b, preferred_element_type=F32)


def _layer_norm(h, g, b):
    mu = jnp.mean(h, axis=-1, keepdims=True)
    c = h - mu
    var = jnp.mean(c * c, axis=-1, keepdims=True)
    return c * lax.rsqrt(var + LN_EPS) * g + b


def _kv_kernel(mem_ref, wkt_ref, wv_ref, kt_ref, v_ref):
    mb = mem_ref[0].astype(BF16)
    kt = lax.dot_general(wkt_ref[...], mb, (((1,), (1,)), ((), ())), preferred_element_type=F32)
    kt_ref[0] = kt.astype(BF16)
    v_ref[0] = _dot(mb, wv_ref[...]).astype(BF16)


def _kv_proj(mem, wkt, wv):
    bn, m, d = mem.shape
    return pl.pallas_call(
        _kv_kernel,
        grid=(bn,),
        in_specs=[
            pl.BlockSpec((1, m, d), lambda b: (b, 0, 0)),
            pl.BlockSpec((d, d), lambda b: (0, 0)),
            pl.BlockSpec((d, d), lambda b: (0, 0)),
        ],
        out_specs=[
            pl.BlockSpec((1, d, m), lambda b: (b, 0, 0)),
            pl.BlockSpec((1, m, d), lambda b: (b, 0, 0)),
        ],
        out_shape=[jax.ShapeDtypeStruct((bn, d, m), BF16), jax.ShapeDtypeStruct((bn, m, d), BF16)],
        compiler_params=pltpu.CompilerParams(dimension_semantics=("arbitrary",)),
        name="kv_proj",
    )(mem, wkt, wv)


def _mixer_kernel(x_ref, win_ref, bg_ref, cw_ref, wco_ref, wpool_ref, ps_ref, kt_ref, v_ref, wxo_ref, wo_ref,
                  g1_ref, b1_ref, wrh_ref, wrl_ref, br_ref,
                  h1_ref, route_ref, ri_ref, cnt_ref,
                  ubuf, pbuf, carry):
    tm = TM_MIX
    d = D_MODEL
    b = pl.program_id(0)
    s = pl.program_id(1)

    @pl.when(s == 0)
    def _():
        ubuf[0:CONV_HALO, :] = jnp.zeros((CONV_HALO, d), F32)
        pbuf[0:POOL_HALO, :] = jnp.zeros((POOL_HALO, d), F32)

    @pl.when((s == 0) & (b == 0))
    def _():
        carry[...] = jnp.zeros_like(carry)

    x = x_ref[...]
    xb = x.astype(BF16)

    def proj(sec):
        return _dot(xb, win_ref[:, sec * d:(sec + 1) * d])

    def gate(i):
        return jax.nn.sigmoid(proj(5 + i) + bg_ref[i:i + 1, :])

    u = proj(1) * proj(2)
    ubuf[CONV_HALO:CONV_HALO + tm, :] = u
    v = cw_ref[CONV_K - 1:CONV_K, :] * u
    for k in range(CONV_K - 1):
        off = CONV_HALO - (CONV_K - 1) + k
        v = v + cw_ref[k:k + 1, :] * ubuf[off:off + tm, :]
    ubuf[0:CONV_HALO, :] = ubuf[tm:tm + CONV_HALO, :]
    y_conv = _dot((proj(0) * v).astype(BF16), wco_ref[...])
    acc = gate(0) * y_conv

    p = proj(3)
    pbuf[POOL_HALO:POOL_HALO + tm, :] = p
    pos = (s * tm + 1 + lax.broadcasted_iota(jnp.int32, (tm, 1), 0)).astype(F32)
    ys = []
    for gi, w in enumerate(POOL_WINDOWS):
        c0 = gi * POOL_GROUP_DIM
        c1 = c0 + POOL_GROUP_DIM
        pg = p[:, c0:c1]
        ws = pg
        for j in range(1, w):
            ws = ws + pbuf[POOL_HALO - j:POOL_HALO - j + tm, c0:c1]
        inv_cnt = 1.0 / jnp.minimum(pos, float(w))
        dg = (ws * inv_cnt - pg).astype(BF16)
        ys.append(_dot(dg, wpool_ref[gi]))
    pbuf[0:POOL_HALO, :] = pbuf[tm:tm + POOL_HALO, :]
    y_pool = jnp.concatenate(ys, axis=1) * ps_ref[...]
    acc = acc + gate(1) * y_pool

    qb = proj(4).astype(BF16)
    scale = X_HEAD_DIM ** -0.5
    os_ = []
    for h in range(X_HEADS):
        c0 = h * X_HEAD_DIM
        c1 = c0 + X_HEAD_DIM
        sc = _dot(qb[:, c0:c1], kt_ref[0, c0:c1, :]) * scale
        e = jnp.exp(sc - jnp.max(sc, axis=-1, keepdims=True))
        a = e * (1.0 / jnp.sum(e, axis=-1, keepdims=True))
        os_.append(_dot(a.astype(BF16), v_ref[0, :, c0:c1]))
    y_mem = _dot(jnp.concatenate(os_, axis=1).astype(BF16), wxo_ref[...])
    acc = acc + gate(2) * y_mem

    mix = _dot(acc.astype(BF16), wo_ref[...])
    h1 = _layer_norm(ALPHA * x + mix, g1_ref[...], b1_ref[...])
    h1_ref[...] = h1

    hh = h1.astype(BF16)
    hl = (h1 - hh.astype(F32)).astype(BF16)
    logits = _dot(hh, wrh_ref[...]) + _dot(hl, wrh_ref[...]) + _dot(hh, wrl_ref[...]) + br_ref[...]
    lane = lax.broadcasted_iota(jnp.int32, (tm, LANES), 1)
    lane_f = lane.astype(F32)
    neg = jnp.float32(-jnp.inf)
    big = jnp.float32(1e9)

    def first_argmax(vals):
        m = jnp.max(vals, axis=-1, keepdims=True)
        idx = jnp.min(jnp.where(vals == m, lane_f, big), axis=-1, keepdims=True)
        return m, idx

    is_g = (lane >= N_EXPERTS) & (lane < N_EXPERTS + N_GROUPS)
    gmax, gidx = first_argmax(jnp.where(is_g, logits, neg))
    g_w = 1.0 / jnp.sum(jnp.where(is_g, jnp.exp(logits - gmax), 0.0), axis=-1, keepdims=True)
    gsel = gidx.astype(jnp.int32) - N_EXPERTS
    in_grp = (lane >> 3) == gsel
    le = jnp.where(in_grp, logits, neg)
    m1, i1 = first_argmax(le)
    m2, i2 = first_argmax(jnp.where(lane_f == i1, neg, le))
    t = jnp.exp(m2 - m1)
    den = 1.0 + t
    w1 = g_w / den
    w2 = g_w * t / den

    sel1 = lane_f == i1
    sel2 = lane_f == i2
    onehot = (sel1 | sel2).astype(BF16)
    row_i = lax.broadcasted_iota(jnp.int32, (tm, tm), 0)
    col_i = lax.broadcasted_iota(jnp.int32, (tm, tm), 1)
    before = (col_i < row_i).astype(BF16)
    prior = _dot(before, onehot) + carry[0:1, :]
    r1 = jnp.sum(jnp.where(sel1, prior, 0.0), axis=-1, keepdims=True)
    r2 = jnp.sum(jnp.where(sel2, prior, 0.0), axis=-1, keepdims=True)
    carry[...] = carry[...] + jnp.sum(onehot.astype(F32), axis=0, keepdims=True)
    cnt_ref[...] = carry[...].astype(jnp.int32)

    route = jnp.where(lane == 0, i1, 0.0)
    for k, col in enumerate((i2, r1, r2, w1, w2), start=1):
        route = jnp.where(lane == k, col, route)
    route_ref[...] = route
    ri_ref[0] = route.T[0:8, :].astype(jnp.int32)


def _mixer(x2, w_in_b, b_gate3, conv_w, wco_b, wpool_b, pool_scale, kt, v, wxo_b, wo_b, ln_g, ln_b, wr_hi, wr_lo, br,
           bn, seq):
    n, d = x2.shape
    tm = TM_MIX
    spb = seq // tm
    const2 = lambda b, s: (0, 0)
    const3 = lambda b, s: (0, 0, 0)
    one = pl.Buffered(1)
    row_map = lambda b, s: (b * spb + s, 0)
    return pl.pallas_call(
        _mixer_kernel,
        grid=(bn, spb),
        in_specs=[
            pl.BlockSpec((tm, d), row_map),
            pl.BlockSpec(w_in_b.shape, const2, pipeline_mode=one),
            pl.BlockSpec(b_gate3.shape, const2, pipeline_mode=one),
            pl.BlockSpec(conv_w.shape, const2, pipeline_mode=one),
            pl.BlockSpec(wco_b.shape, const2, pipeline_mode=one),
            pl.BlockSpec(wpool_b.shape, const3, pipeline_mode=one),
            pl.BlockSpec(pool_scale.shape, const2, pipeline_mode=one),
            pl.BlockSpec((1,) + kt.shape[1:], lambda b, s: (b, 0, 0)),
            pl.BlockSpec((1,) + v.shape[1:], lambda b, s: (b, 0, 0)),
            pl.BlockSpec(wxo_b.shape, const2, pipeline_mode=one),
            pl.BlockSpec(wo_b.shape, const2, pipeline_mode=one),
            pl.BlockSpec(ln_g.shape, const2, pipeline_mode=one),
            pl.BlockSpec(ln_b.shape, const2, pipeline_mode=one),
            pl.BlockSpec(wr_hi.shape, const2, pipeline_mode=one),
            pl.BlockSpec(wr_lo.shape, const2, pipeline_mode=one),
            pl.BlockSpec(br.shape, const2, pipeline_mode=one),
        ],
        out_specs=[
            pl.BlockSpec((tm, d), row_map),
            pl.BlockSpec((tm, LANES), row_map),
            pl.BlockSpec((1, 8, tm), lambda b, s: (b * spb + s, 0, 0)),
            pl.BlockSpec((8, LANES), const2),
        ],
        out_shape=[
            jax.ShapeDtypeStruct((n, d), F32),
            jax.ShapeDtypeStruct((n, LANES), F32),
            jax.ShapeDtypeStruct((n // tm, 8, tm), jnp.int32),
            jax.ShapeDtypeStruct((8, LANES), jnp.int32),
        ],
        scratch_shapes=[
            pltpu.VMEM((tm + CONV_HALO, d), F32),
            pltpu.VMEM((tm + POOL_HALO, d), F32),
            pltpu.VMEM((8, LANES), F32),
        ],
        compiler_params=pltpu.CompilerParams(
            dimension_semantics=("arbitrary", "arbitrary"), vmem_limit_bytes=VMEM_LIMIT),
        name="mixer",
    )(x2, w_in_b, b_gate3, conv_w, wco_b, wpool_b, pool_scale, kt, v, wxo_b, wo_b, ln_g, ln_b, wr_hi, wr_lo, br)


def _row_copy(src, src_row, dst, dst_row, sem):
    return pltpu.make_async_copy(src.at[pl.ds(src_row, 1), :], dst.at[pl.ds(dst_row, 1), :], sem)


def _dest_rows(ps_ref, ri_ref, r):
    return ps_ref[ri_ref[0, 0, r]] + ri_ref[0, 2, r], ps_ref[ri_ref[0, 1, r]] + ri_ref[0, 3, r]


def _scatter_kernel(zt_ref, ps_ref, ri_ref, h_ref, xbuf_ref, zbuf, sem):
    i = pl.program_id(0)
    tm = TM_ROW

    def zero_copy(e):
        start = pl.multiple_of(zt_ref[e], TM_EXP)
        return pltpu.make_async_copy(zbuf, xbuf_ref.at[pl.ds(start, TM_EXP), :], sem.at[2])

    @pl.when(i == 0)
    def _():
        zbuf[...] = jnp.zeros_like(zbuf)

        def start(e, c):
            @pl.when(zt_ref[e] >= 0)
            def _():
                zero_copy(e).start()
            return c

        def wait(e, c):
            @pl.when(zt_ref[e] >= 0)
            def _():
                zero_copy(e).wait()
            return c

        lax.fori_loop(0, 2 * N_EXPERTS, start, 0)
        lax.fori_loop(0, 2 * N_EXPERTS, wait, 0)

    def issue(r, c):
        d1, d2 = _dest_rows(ps_ref, ri_ref, r)
        _row_copy(h_ref, r, xbuf_ref, d1, sem.at[0]).start()
        _row_copy(h_ref, r, xbuf_ref, d2, sem.at[1]).start()
        return c

    lax.fori_loop(0, tm, issue, 0, unroll=8)
    pltpu.make_async_copy(h_ref, xbuf_ref.at[pl.ds(0, tm), :], sem.at[0]).wait()
    pltpu.make_async_copy(h_ref, xbuf_ref.at[pl.ds(0, tm), :], sem.at[1]).wait()


def _scatter(zero_tiles, pstarts, route_i, h1, p_rows):
    n, d = h1.shape
    tm = TM_ROW
    nt = n // tm
    return pl.pallas_call(
        _scatter_kernel,
        grid_spec=pltpu.PrefetchScalarGridSpec(
            num_scalar_prefetch=2,
            grid=(nt,),
            in_specs=[
                pl.BlockSpec((1, 8, tm), lambda i, zt, ps: (i, 0, 0), memory_space=pltpu.SMEM),
                pl.BlockSpec((tm, d), lambda i, zt, ps: (i, 0)),
            ],
            out_specs=pl.BlockSpec(memory_space=pl.ANY),
            scratch_shapes=[pltpu.VMEM((TM_EXP, d), F32), pltpu.SemaphoreType.DMA((3,))],
        ),
        out_shape=jax.ShapeDtypeStruct((p_rows, d), F32),
        compiler_params=pltpu.CompilerParams(dimension_semantics=("arbitrary",)),
        name="scatter_rows",
    )(zero_tiles, pstarts, route_i, h1)


def _expert_kernel(te_ref, nt_ref, x_ref, wup_ref, wdn_ref, y_ref, wup_b, wdn_b):
    i = pl.program_id(0)
    live = i < nt_ref[0]

    @pl.when(jnp.logical_not(live))
    def _():
        y_ref[...] = jnp.zeros_like(y_ref)

    @pl.when(live & ((i == 0) | (te_ref[i] != te_ref[jnp.maximum(i - 1, 0)])))
    def _():
        wup_b[...] = wup_ref[0].astype(BF16)
        wdn_b[...] = wdn_ref[0].astype(BF16)

    @pl.when(live)
    def _():
        xb = x_ref[...].astype(BF16)
        hgv = _dot(xb, wup_b[...])
        hg = hgv[:, :D_EXPERT]
        hv = hgv[:, D_EXPERT:]
        act = (hg * jax.nn.sigmoid(hg)) * hv
        y_ref[...] = _dot(act.astype(BF16), wdn_b[...])


def _experts(tile_expert, n_tiles, xbuf, w_up, w_down):
    p_rows, d = xbuf.shape
    tm = TM_EXP
    row_map = lambda i, te, nt: (i, 0)
    w_map = lambda i, te, nt: (te[i], 0, 0)
    return pl.pallas_call(
        _expert_kernel,
        grid_spec=pltpu.PrefetchScalarGridSpec(
            num_scalar_prefetch=2,
            grid=(p_rows // tm,),
            in_specs=[
                pl.BlockSpec((tm, d), row_map),
                pl.BlockSpec((1,) + w_up.shape[1:], w_map),
                pl.BlockSpec((1,) + w_down.shape[1:], w_map),
            ],
            out_specs=pl.BlockSpec((tm, d), row_map),
            scratch_shapes=[pltpu.VMEM(w_up.shape[1:], BF16), pltpu.VMEM(w_down.shape[1:], BF16)],
        ),
        out_shape=jax.ShapeDtypeStruct((p_rows, d), F32),
        compiler_params=pltpu.CompilerParams(dimension_semantics=("arbitrary",), vmem_limit_bytes=VMEM_LIMIT),
        name="experts",
    )(tile_expert, n_tiles, xbuf, w_up, w_down)


def _combine_kernel(ps_ref, ri_ref, h_ref, route_ref, ybuf_ref, g_ref, b_ref, out_ref, ya, yb, sem):
    tm = TM_ROW

    def issue(r, c):
        d1, d2 = _dest_rows(ps_ref, ri_ref, r)
        _row_copy(ybuf_ref, d1, ya, r, sem.at[0]).start()
        _row_copy(ybuf_ref, d2, yb, r, sem.at[1]).start()
        return c

    lax.fori_loop(0, tm, issue, 0, unroll=8)
    pltpu.make_async_copy(ybuf_ref.at[pl.ds(0, tm), :], ya, sem.at[0]).wait()
    pltpu.make_async_copy(ybuf_ref.at[pl.ds(0, tm), :], yb, sem.at[1]).wait()
    route = route_ref[...]
    ffn = route[:, 4:5] * ya[...] + route[:, 5:6] * yb[...]
    out_ref[...] = _layer_norm(ALPHA * h_ref[...] + ffn, g_ref[...], b_ref[...])


def _combine(pstarts, route_i, h1, route, ybuf, ln_g, ln_b):
    n, d = h1.shape
    tm = TM_ROW
    nt = n // tm
    return pl.pallas_call(
        _combine_kernel,
        grid_spec=pltpu.PrefetchScalarGridSpec(
            num_scalar_prefetch=1,
            grid=(nt,),
            in_specs=[
                pl.BlockSpec((1, 8, tm), lambda i, ps: (i, 0, 0), memory_space=pltpu.SMEM),
                pl.BlockSpec((tm, d), lambda i, ps: (i, 0)),
                pl.BlockSpec((tm, LANES), lambda i, ps: (i, 0)),
                pl.BlockSpec(memory_space=pl.ANY),
                pl.BlockSpec(ln_g.shape, lambda i, ps: (0, 0)),
                pl.BlockSpec(ln_b.shape, lambda i, ps: (0, 0)),
            ],
            out_specs=pl.BlockSpec((tm, d), lambda i, ps: (i, 0)),
            scratch_shapes=[pltpu.VMEM((tm, d), F32), pltpu.VMEM((tm, d), F32), pltpu.SemaphoreType.DMA((2,))],
        ),
        out_shape=jax.ShapeDtypeStruct((n, d), F32),
        compiler_params=pltpu.CompilerParams(dimension_semantics=("arbitrary",)),
        name="combine_ln2",
    )(pstarts, route_i, h1, route, ybuf, ln_g, ln_b)


def _layer(h, mem, w_in, b_gate, conv_w, w_conv_out, w_pool, pool_scale, w_kv, w_xo, w_o, ln1_g, ln1_b,
           w_rg, b_rg, w_re, b_re, w_up, w_down, ln2_g, ln2_b):
    bn, seq, d = h.shape
    n = bn * seq
    xw = X_HEADS * X_HEAD_DIM

    kt, v = _kv_proj(mem, w_kv[:, :xw].T.astype(BF16), w_kv[:, xw:].astype(BF16))

    pad = LANES - N_EXPERTS - N_GROUPS
    w_r = jnp.concatenate([w_re, w_rg, jnp.zeros((d, pad), F32)], axis=1)
    wr_hi = w_r.astype(BF16)
    wr_lo = (w_r - wr_hi.astype(F32)).astype(BF16)
    b_r = jnp.concatenate([b_re, b_rg, jnp.zeros((pad,), F32)])[None, :]

    h1, route, route_i, cnt = _mixer(
        h.reshape(n, d), w_in.astype(BF16), b_gate.reshape(3, d), conv_w, w_conv_out.astype(BF16),
        w_pool.astype(BF16), pool_scale[None, :], kt, v, w_xo.astype(BF16), w_o.astype(BF16),
        ln1_g[None, :], ln1_b[None, :], wr_hi, wr_lo, b_r, bn, seq)

    counts = cnt[0, :N_EXPERTS]
    padded = (counts + TM_EXP - 1) // TM_EXP * TM_EXP
    pends = jnp.cumsum(padded)
    pstarts = pends - padded
    max_tiles = (n * TOP_K + N_EXPERTS * (TM_EXP - 1)) // TM_EXP
    p_rows = max_tiles * TM_EXP
    n_tiles = (pends[-1] // TM_EXP).astype(jnp.int32)
    tile_ids = jnp.minimum(jnp.arange(max_tiles, dtype=jnp.int32), n_tiles - 1)
    tile_expert = jnp.sum(pends[None, :] <= (tile_ids * TM_EXP)[:, None], axis=1, dtype=jnp.int32)
    tile_expert = jnp.minimum(tile_expert, N_EXPERTS - 1)
    tail_ids = n_tiles + jnp.arange(N_EXPERTS, dtype=jnp.int32)
    zero_tiles = jnp.concatenate([
        jnp.where(counts % TM_EXP != 0, pends - TM_EXP, -1),
        jnp.where(tail_ids < max_tiles, tail_ids * TM_EXP, -1)]).astype(jnp.int32)
    pstarts = pstarts.astype(jnp.int32)

    xbuf = _scatter(zero_tiles, pstarts, route_i, h1, p_rows)
    ybuf = _experts(tile_expert, n_tiles.reshape(1), xbuf, w_up, w_down)
    out = _combine(pstarts, route_i, h1, route, ybuf, ln2_g[None, :], ln2_b[None, :])
    return out.reshape(bn, seq, d)


def kernel(x, mem, w_in, b_gate, conv_w, w_conv_out, w_pool, pool_scale, w_kv, w_xo, w_o, ln1_g, ln1_b,
           w_router_group, b_router_group, w_router_expert, b_router_expert, w_up, w_down, ln2_g, ln2_b):
    h = x
    for l in range(DEPTH):
        h = _layer(h, mem, w_in[l], b_gate[l], conv_w[l], w_conv_out[l], w_pool[l], pool_scale[l], w_kv[l],
                   w_xo[l], w_o[l], ln1_g[l], ln1_b[l], w_router_group[l], b_router_group[l],
                   w_router_expert[l], b_router_expert[l], w_up[l], w_down[l], ln2_g[l], ln2_b[l])
    return h
```

```python
import functools

import jax
import jax.numpy as jnp
from jax import lax
from jax.experimental import pallas as pl
from jax.experimental.pallas import tpu as pltpu

D_MODEL = 1024
CONV_K = 3
POOL_WINDOWS = (2, 4, 8, 16)
POOL_GROUP_DIM = 256
X_HEADS = 4
X_HEAD_DIM = 256
N_GROUPS = 8
EXPERTS_PER_GROUP = 8
N_EXPERTS = 64
TOP_K = 2
D_EXPERT = 512
DEPTH = 1
ALPHA = (2.0 * DEPTH) ** 0.25
LN_EPS = 1e-5

LANES = 128
SUB = 8
assert D_MODEL == SUB * LANES
POOL_HALO = 16
CONV_HALO = 8
TM_MIX = 512
TM_EXP = 256
TM_ROW = TM_MIX
VMEM_LIMIT = 58 * 1024 * 1024

F32 = jnp.float32
BF16 = jnp.bfloat16


def _dot(a, b):
    return jnp.dot(a, b, preferred_element_type=F32)


def _tok_load(ref, tm):
    return jnp.concatenate([ref[pl.ds(c, tm, stride=SUB), :] for c in range(SUB)], axis=1)


def _tok_store(ref, val, tm):
    for c in range(SUB):
        ref[pl.ds(c, tm, stride=SUB), :] = val[:, c * LANES:(c + 1) * LANES]


def _tile_copy(src, src_tok, dst, dst_tok, sem):
    s = pl.multiple_of(src_tok * SUB, SUB)
    t = pl.multiple_of(dst_tok * SUB, SUB)
    return pltpu.make_async_copy(src.at[pl.ds(s, SUB), :], dst.at[pl.ds(t, SUB), :], sem)


def _layer_norm(h, g, b):
    mu = jnp.mean(h, axis=-1, keepdims=True)
    c = h - mu
    var = jnp.mean(c * c, axis=-1, keepdims=True)
    return c * lax.rsqrt(var + LN_EPS) * g + b


def _kv_kernel(mem_ref, wkt_ref, wv_ref, kt_ref, v_ref):
    mb = mem_ref[0].astype(BF16)
    kt = lax.dot_general(wkt_ref[...], mb, (((1,), (1,)), ((), ())), preferred_element_type=F32)
    kt_ref[0] = kt.astype(BF16)
    v_ref[0] = _dot(mb, wv_ref[...]).astype(BF16)


def _kv_proj(mem, wkt, wv):
    bn, m, d = mem.shape
    return pl.pallas_call(
        _kv_kernel,
        grid=(bn,),
        in_specs=[
            pl.BlockSpec((1, m, d), lambda b: (b, 0, 0)),
            pl.BlockSpec((d, d), lambda b: (0, 0)),
            pl.BlockSpec((d, d), lambda b: (0, 0)),
        ],
        out_specs=[
            pl.BlockSpec((1, d, m), lambda b: (b, 0, 0)),
            pl.BlockSpec((1, m, d), lambda b: (b, 0, 0)),
        ],
        out_shape=[jax.ShapeDtypeStruct((bn, d, m), BF16), jax.ShapeDtypeStruct((bn, m, d), BF16)],
        compiler_params=pltpu.CompilerParams(dimension_semantics=("arbitrary",)),
        name="kv_proj",
    )(mem, wkt, wv)


def _mixer_kernel(x_ref, win_ref, bg_ref, cw_ref, wco_ref, wpool_ref, ps_ref, kt_ref, v_ref, wxo_ref, wo_ref,
                  g1_ref, b1_ref, wrh_ref, wrl_ref, br_ref,
                  h1_ref, route_ref, ri_ref, cnt_ref,
                  ubuf, pbuf, carry):
    tm = TM_MIX
    d = D_MODEL
    b = pl.program_id(0)
    s = pl.program_id(1)

    @pl.when(s == 0)
    def _():
        ubuf[0:CONV_HALO, :] = jnp.zeros((CONV_HALO, d), F32)
        pbuf[0:POOL_HALO, :] = jnp.zeros((POOL_HALO, d), F32)

    @pl.when((s == 0) & (b == 0))
    def _():
        carry[...] = jnp.zeros_like(carry)

    x = x_ref[...]
    xb = x.astype(BF16)

    def proj(sec):
        return _dot(xb, win_ref[:, sec * d:(sec + 1) * d])

    def gate(i):
        return jax.nn.sigmoid(proj(5 + i) + bg_ref[i:i + 1, :])

    u = proj(1) * proj(2)
    ubuf[CONV_HALO:CONV_HALO + tm, :] = u
    v = cw_ref[CONV_K - 1:CONV_K, :] * u
    for k in range(CONV_K - 1):
        off = CONV_HALO - (CONV_K - 1) + k
        v = v + cw_ref[k:k + 1, :] * ubuf[off:off + tm, :]
    ubuf[0:CONV_HALO, :] = ubuf[tm:tm + CONV_HALO, :]
    y_conv = _dot((proj(0) * v).astype(BF16), wco_ref[...])
    acc = gate(0) * y_conv

    p = proj(3)
    pbuf[POOL_HALO:POOL_HALO + tm, :] = p
    pos = (s * tm + 1 + lax.broadcasted_iota(jnp.int32, (tm, 1), 0)).astype(F32)
    ys = []
    for gi, w in enumerate(POOL_WINDOWS):
        c0 = gi * POOL_GROUP_DIM
        c1 = c0 + POOL_GROUP_DIM
        pg = p[:, c0:c1]
        ws = pg
        for j in range(1, w):
            ws = ws + pbuf[POOL_HALO - j:POOL_HALO - j + tm, c0:c1]
        inv_cnt = 1.0 / jnp.minimum(pos, float(w))
        dg = (ws * inv_cnt - pg).astype(BF16)
        ys.append(_dot(dg, wpool_ref[gi]))
    pbuf[0:POOL_HALO, :] = pbuf[tm:tm + POOL_HALO, :]
    y_pool = jnp.concatenate(ys, axis=1) * ps_ref[...]
    acc = acc + gate(1) * y_pool

    qb = proj(4).astype(BF16)
    scale = X_HEAD_DIM ** -0.5
    os_ = []
    for h in range(X_HEADS):
        c0 = h * X_HEAD_DIM
        c1 = c0 + X_HEAD_DIM
        sc = _dot(qb[:, c0:c1], kt_ref[0, c0:c1, :]) * scale
        e = jnp.exp(sc - jnp.max(sc, axis=-1, keepdims=True))
        a = e * (1.0 / jnp.sum(e, axis=-1, keepdims=True))
        os_.append(_dot(a.astype(BF16), v_ref[0, :, c0:c1]))
    y_mem = _dot(jnp.concatenate(os_, axis=1).astype(BF16), wxo_ref[...])
    acc = acc + gate(2) * y_mem

    mix = _dot(acc.astype(BF16), wo_ref[...])
    h1 = _layer_norm(ALPHA * x + mix, g1_ref[...], b1_ref[...])
    _tok_store(h1_ref, h1, tm)

    hh = h1.astype(BF16)
    hl = (h1 - hh.astype(F32)).astype(BF16)
    logits = _dot(hh, wrh_ref[...]) + _dot(hl, wrh_ref[...]) + _dot(hh, wrl_ref[...]) + br_ref[...]
    lane = lax.broadcasted_iota(jnp.int32, (tm, LANES), 1)
    lane_f = lane.astype(F32)
    neg = jnp.float32(-jnp.inf)
    big = jnp.float32(1e9)

    def first_argmax(vals):
        m = jnp.max(vals, axis=-1, keepdims=True)
        idx = jnp.min(jnp.where(vals == m, lane_f, big), axis=-1, keepdims=True)
        return m, idx

    is_g = (lane >= N_EXPERTS) & (lane < N_EXPERTS + N_GROUPS)
    gmax, gidx = first_argmax(jnp.where(is_g, logits, neg))
    g_w = 1.0 / jnp.sum(jnp.where(is_g, jnp.exp(logits - gmax), 0.0), axis=-1, keepdims=True)
    gsel = gidx.astype(jnp.int32) - N_EXPERTS
    in_grp = (lane >> 3) == gsel
    le = jnp.where(in_grp, logits, neg)
    m1, i1 = first_argmax(le)
    m2, i2 = first_argmax(jnp.where(lane_f == i1, neg, le))
    t = jnp.exp(m2 - m1)
    den = 1.0 + t
    w1 = g_w / den
    w2 = g_w * t / den

    sel1 = lane_f == i1
    sel2 = lane_f == i2
    onehot = (sel1 | sel2).astype(BF16)
    row_i = lax.broadcasted_iota(jnp.int32, (tm, tm), 0)
    col_i = lax.broadcasted_iota(jnp.int32, (tm, tm), 1)
    before = (col_i < row_i).astype(BF16)
    prior = _dot(before, onehot) + carry[0:1, :]
    r1 = jnp.sum(jnp.where(sel1, prior, 0.0), axis=-1, keepdims=True)
    r2 = jnp.sum(jnp.where(sel2, prior, 0.0), axis=-1, keepdims=True)
    carry[...] = carry[...] + jnp.sum(onehot.astype(F32), axis=0, keepdims=True)
    cnt_ref[...] = carry[...].astype(jnp.int32)

    route = jnp.where(lane == 0, i1, 0.0)
    for k, col in enumerate((i2, r1, r2, w1, w2), start=1):
        route = jnp.where(lane == k, col, route)
    route_ref[...] = route
    ri_ref[0] = route.T[0:8, :].astype(jnp.int32)


def _mixer(x2, w_in_b, b_gate3, conv_w, wco_b, wpool_b, pool_scale, kt, v, wxo_b, wo_b, ln_g, ln_b, wr_hi, wr_lo, br,
           bn, seq):
    n, d = x2.shape
    tm = TM_MIX
    spb = seq // tm
    const2 = lambda b, s: (0, 0)
    const3 = lambda b, s: (0, 0, 0)
    one = pl.Buffered(1)
    row_map = lambda b, s: (b * spb + s, 0)
    return pl.pallas_call(
        _mixer_kernel,
        grid=(bn, spb),
        in_specs=[
            pl.BlockSpec((tm, d), row_map),
            pl.BlockSpec(w_in_b.shape, const2, pipeline_mode=one),
            pl.BlockSpec(b_gate3.shape, const2, pipeline_mode=one),
            pl.BlockSpec(conv_w.shape, const2, pipeline_mode=one),
            pl.BlockSpec(wco_b.shape, const2, pipeline_mode=one),
            pl.BlockSpec(wpool_b.shape, const3, pipeline_mode=one),
            pl.BlockSpec(pool_scale.shape, const2, pipeline_mode=one),
            pl.BlockSpec((1,) + kt.shape[1:], lambda b, s: (b, 0, 0)),
            pl.BlockSpec((1,) + v.shape[1:], lambda b, s: (b, 0, 0)),
            pl.BlockSpec(wxo_b.shape, const2, pipeline_mode=one),
            pl.BlockSpec(wo_b.shape, const2, pipeline_mode=one),
            pl.BlockSpec(ln_g.shape, const2, pipeline_mode=one),
            pl.BlockSpec(ln_b.shape, const2, pipeline_mode=one),
            pl.BlockSpec(wr_hi.shape, const2, pipeline_mode=one),
            pl.BlockSpec(wr_lo.shape, const2, pipeline_mode=one),
            pl.BlockSpec(br.shape, const2, pipeline_mode=one),
        ],
        out_specs=[
            pl.BlockSpec((tm * SUB, LANES), row_map),
            pl.BlockSpec((tm, LANES), row_map),
            pl.BlockSpec((1, 8, tm), lambda b, s: (b * spb + s, 0, 0)),
            pl.BlockSpec((8, LANES), const2),
        ],
        out_shape=[
            jax.ShapeDtypeStruct((n * SUB, LANES), F32),
            jax.ShapeDtypeStruct((n, LANES), F32),
            jax.ShapeDtypeStruct((n // tm, 8, tm), jnp.int32),
            jax.ShapeDtypeStruct((8, LANES), jnp.int32),
        ],
        scratch_shapes=[
            pltpu.VMEM((tm + CONV_HALO, d), F32),
            pltpu.VMEM((tm + POOL_HALO, d), F32),
            pltpu.VMEM((8, LANES), F32),
        ],
        compiler_params=pltpu.CompilerParams(
            dimension_semantics=("arbitrary", "arbitrary"), vmem_limit_bytes=VMEM_LIMIT),
        name="mixer",
    )(x2, w_in_b, b_gate3, conv_w, wco_b, wpool_b, pool_scale, kt, v, wxo_b, wo_b, ln_g, ln_b, wr_hi, wr_lo, br)


def _dest_kernel(ps_ref, ri_ref, dest_ref):
    ri = ri_ref[0]
    e = ri[0:TOP_K, :]
    start = jnp.zeros_like(e)
    for j in range(N_EXPERTS):
        start = jnp.where(e == j, ps_ref[j], start)
    dest_ref[0] = jnp.zeros(dest_ref.shape[1:], jnp.int32)
    dest_ref[0, 0:TOP_K, :] = start + ri[TOP_K:2 * TOP_K, :]


def _dest(pstarts, route_i):
    nt, rows, tm = route_i.shape
    spec = pl.BlockSpec((1, rows, tm), lambda i, ps: (i, 0, 0))
    return pl.pallas_call(
        _dest_kernel,
        grid_spec=pltpu.PrefetchScalarGridSpec(num_scalar_prefetch=1, grid=(nt,), in_specs=[spec], out_specs=spec),
        out_shape=jax.ShapeDtypeStruct(route_i.shape, jnp.int32),
        compiler_params=pltpu.CompilerParams(dimension_semantics=("arbitrary",)),
        name="dest_rows",
    )(pstarts, route_i)


def _scatter_kernel(zt_ref, dest_ref, h_ref, xbuf_ref, zbuf, sem):
    i = pl.program_id(0)
    tm = TM_ROW

    def zero_copy(e):
        start = pl.multiple_of(zt_ref[e] * SUB, TM_EXP * SUB)
        return pltpu.make_async_copy(zbuf, xbuf_ref.at[pl.ds(start, TM_EXP * SUB), :], sem.at[2])

    @pl.when(i == 0)
    def _():
        zbuf[...] = jnp.zeros_like(zbuf)

        def start(e, c):
            @pl.when(zt_ref[e] >= 0)
            def _():
                zero_copy(e).start()
            return c

        def wait(e, c):
            @pl.when(zt_ref[e] >= 0)
            def _():
                zero_copy(e).wait()
            return c

        lax.fori_loop(0, 2 * N_EXPERTS, start, 0)
        lax.fori_loop(0, 2 * N_EXPERTS, wait, 0)

    def issue(r, c):
        for k in range(TOP_K):
            _tile_copy(h_ref, r, xbuf_ref, dest_ref[0, k, r], sem.at[k]).start(priority=k)
        return c

    lax.fori_loop(0, tm, issue, 0, unroll=8)
    for k in range(TOP_K):
        pltpu.make_async_copy(h_ref, xbuf_ref.at[pl.ds(0, tm * SUB), :], sem.at[k]).wait()


def _scatter(zero_tiles, dest, h1t, p_rows):
    tm = TM_ROW
    nt = dest.shape[0]
    return pl.pallas_call(
        _scatter_kernel,
        grid_spec=pltpu.PrefetchScalarGridSpec(
            num_scalar_prefetch=1,
            grid=(nt,),
            in_specs=[
                pl.BlockSpec((1, 8, tm), lambda i, zt: (i, 0, 0), memory_space=pltpu.SMEM),
                pl.BlockSpec((tm * SUB, LANES), lambda i, zt: (i, 0)),
            ],
            out_specs=pl.BlockSpec(memory_space=pl.ANY),
            scratch_shapes=[pltpu.VMEM((TM_EXP * SUB, LANES), F32), pltpu.SemaphoreType.DMA((3,))],
        ),
        out_shape=jax.ShapeDtypeStruct((p_rows * SUB, LANES), F32),
        compiler_params=pltpu.CompilerParams(dimension_semantics=("arbitrary",)),
        name="scatter_rows",
    )(zero_tiles, dest, h1t)


def _expert_kernel(te_ref, nt_ref, x_ref, wup_ref, wdn_ref, y_ref, wup_b, wdn_b):
    i = pl.program_id(0)
    live = i < nt_ref[0]

    @pl.when(jnp.logical_not(live))
    def _():
        y_ref[...] = jnp.zeros_like(y_ref)

    @pl.when(live & ((i == 0) | (te_ref[i] != te_ref[jnp.maximum(i - 1, 0)])))
    def _():
        wup_b[...] = wup_ref[0].astype(BF16)
        wdn_b[...] = wdn_ref[0].astype(BF16)

    @pl.when(live)
    def _():
        xb = _tok_load(x_ref, TM_EXP).astype(BF16)
        hgv = _dot(xb, wup_b[...])
        hg = hgv[:, :D_EXPERT]
        hv = hgv[:, D_EXPERT:]
        act = (hg * jax.nn.sigmoid(hg)) * hv
        _tok_store(y_ref, _dot(act.astype(BF16), wdn_b[...]), TM_EXP)


def _experts(tile_expert, n_tiles, xbuf, w_up, w_down):
    tm = TM_EXP
    row_map = lambda i, te, nt: (i, 0)
    w_map = lambda i, te, nt: (te[i], 0, 0)
    return pl.pallas_call(
        _expert_kernel,
        grid_spec=pltpu.PrefetchScalarGridSpec(
            num_scalar_prefetch=2,
            grid=(xbuf.shape[0] // (tm * SUB),),
            in_specs=[
                pl.BlockSpec((tm * SUB, LANES), row_map),
                pl.BlockSpec((1,) + w_up.shape[1:], w_map),
                pl.BlockSpec((1,) + w_down.shape[1:], w_map),
            ],
            out_specs=pl.BlockSpec((tm * SUB, LANES), row_map),
            scratch_shapes=[pltpu.VMEM(w_up.shape[1:], BF16), pltpu.VMEM(w_down.shape[1:], BF16)],
        ),
        out_shape=jax.ShapeDtypeStruct(xbuf.shape, F32),
        compiler_params=pltpu.CompilerParams(dimension_semantics=("arbitrary",), vmem_limit_bytes=VMEM_LIMIT),
        name="experts",
    )(tile_expert, n_tiles, xbuf, w_up, w_down)


def _combine_kernel(dest_ref, h_ref, route_ref, ybuf_ref, g_ref, b_ref, out_ref, ybufs, sem):
    tm = TM_ROW

    def issue(r, c):
        for k in range(TOP_K):
            _tile_copy(ybuf_ref, dest_ref[0, k, r], ybufs.at[k], r, sem.at[k]).start(priority=k)
        return c

    lax.fori_loop(0, tm, issue, 0, unroll=8)
    for k in range(TOP_K):
        pltpu.make_async_copy(ybuf_ref.at[pl.ds(0, tm * SUB), :], ybufs.at[k], sem.at[k]).wait()
    route = route_ref[...]
    ffn = route[:, 4:5] * _tok_load(ybufs.at[0], tm) + route[:, 5:6] * _tok_load(ybufs.at[1], tm)
    out_ref[...] = _layer_norm(ALPHA * _tok_load(h_ref, tm) + ffn, g_ref[...], b_ref[...])


def _combine(dest, h1t, route, ybuf, ln_g, ln_b):
    tm = TM_ROW
    nt = dest.shape[0]
    n = nt * tm
    return pl.pallas_call(
        _combine_kernel,
        grid=(nt,),
        in_specs=[
            pl.BlockSpec((1, 8, tm), lambda i: (i, 0, 0), memory_space=pltpu.SMEM),
            pl.BlockSpec((tm * SUB, LANES), lambda i: (i, 0)),
            pl.BlockSpec((tm, LANES), lambda i: (i, 0)),
            pl.BlockSpec(memory_space=pl.ANY),
            pl.BlockSpec(ln_g.shape, lambda i: (0, 0)),
            pl.BlockSpec(ln_b.shape, lambda i: (0, 0)),
        ],
        out_specs=pl.BlockSpec((tm, D_MODEL), lambda i: (i, 0)),
        out_shape=jax.ShapeDtypeStruct((n, D_MODEL), F32),
        scratch_shapes=[pltpu.VMEM((TOP_K, tm * SUB, LANES), F32), pltpu.SemaphoreType.DMA((TOP_K,))],
        compiler_params=pltpu.CompilerParams(dimension_semantics=("arbitrary",)),
        name="combine_ln2",
    )(dest, h1t, route, ybuf, ln_g, ln_b)


def _layer(h, mem, w_in, b_gate, conv_w, w_conv_out, w_pool, pool_scale, w_kv, w_xo, w_o, ln1_g, ln1_b,
           w_rg, b_rg, w_re, b_re, w_up, w_down, ln2_g, ln2_b):
    bn, seq, d = h.shape
    n = bn * seq
    xw = X_HEADS * X_HEAD_DIM

    kt, v = _kv_proj(mem, w_kv[:, :xw].T.astype(BF16), w_kv[:, xw:].astype(BF16))

    pad = LANES - N_EXPERTS - N_GROUPS
    w_r = jnp.concatenate([w_re, w_rg, jnp.zeros((d, pad), F32)], axis=1)
    wr_hi = w_r.astype(BF16)
    wr_lo = (w_r - wr_hi.astype(F32)).astype(BF16)
    b_r = jnp.concatenate([b_re, b_rg, jnp.zeros((pad,), F32)])[None, :]

    h1t, route, route_i, cnt = _mixer(
        h.reshape(n, d), w_in.astype(BF16), b_gate.reshape(3, d), conv_w, w_conv_out.astype(BF16),
        w_pool.astype(BF16), pool_scale[None, :], kt, v, w_xo.astype(BF16), w_o.astype(BF16),
        ln1_g[None, :], ln1_b[None, :], wr_hi, wr_lo, b_r, bn, seq)

    counts = cnt[0, :N_EXPERTS]
    padded = (counts + TM_EXP - 1) // TM_EXP * TM_EXP
    pends = jnp.cumsum(padded)
    pstarts = pends - padded
    max_tiles = (n * TOP_K + N_EXPERTS * (TM_EXP - 1)) // TM_EXP
    p_rows = max_tiles * TM_EXP
    n_tiles = (pends[-1] // TM_EXP).astype(jnp.int32)
    tile_ids = jnp.minimum(jnp.arange(max_tiles, dtype=jnp.int32), n_tiles - 1)
    tile_expert = jnp.sum(pends[None, :] <= (tile_ids * TM_EXP)[:, None], axis=1, dtype=jnp.int32)
    tile_expert = jnp.minimum(tile_expert, N_EXPERTS - 1)
    tail_ids = n_tiles + jnp.arange(N_EXPERTS, dtype=jnp.int32)
    zero_tiles = jnp.concatenate([
        jnp.where(counts % TM_EXP != 0, pends - TM_EXP, -1),
        jnp.where(tail_ids < max_tiles, tail_ids * TM_EXP, -1)]).astype(jnp.int32)

    dest = _dest(pstarts.astype(jnp.int32), route_i)
    xbuf = _scatter(zero_tiles, dest, h1t, p_rows)
    ybuf = _experts(tile_expert, n_tiles.reshape(1), xbuf, w_up, w_down)
    out = _combine(dest, h1t, route, ybuf, ln2_g[None, :], ln2_b[None, :])
    return out.reshape(bn, seq, d)


def kernel(x, mem, w_in, b_gate, conv_w, w_conv_out, w_pool, pool_scale, w_kv, w_xo, w_o, ln1_g, ln1_b,
           w_router_group, b_router_group, w_router_expert, b_router_expert, w_up, w_down, ln2_g, ln2_b):
    h = x
    for l in range(DEPTH):
        h = _layer(h, mem, w_in[l], b_gate[l], conv_w[l], w_conv_out[l], w_pool[l], pool_scale[l], w_kv[l],
                   w_xo[l], w_o[l], ln1_g[l], ln1_b[l], w_router_group[l], b_router_group[l],
                   w_router_expert[l], b_router_expert[l], w_up[l], w_down[l], ln2_g[l], ln2_b[l])
    return h
```

```python
import functools

import jax
import jax.numpy as jnp
from jax import lax
from jax.experimental import pallas as pl
from jax.experimental.pallas import tpu as pltpu

D_MODEL = 1024
CONV_K = 3
POOL_WINDOWS = (2, 4, 8, 16)
POOL_GROUP_DIM = 256
X_HEADS = 4
X_HEAD_DIM = 256
N_GROUPS = 8
EXPERTS_PER_GROUP = 8
N_EXPERTS = 64
TOP_K = 2
D_EXPERT = 512
DEPTH = 1
ALPHA = (2.0 * DEPTH) ** 0.25
LN_EPS = 1e-5

LANES = 128
SUB = 8
assert D_MODEL == SUB * LANES
POOL_HALO = 16
CONV_HALO = 8
TM_MIX = 512
TM_EXP = 512
TM_ROW = TM_MIX
VMEM_LIMIT = 58 * 1024 * 1024

F32 = jnp.float32
BF16 = jnp.bfloat16


def _dot(a, b):
    return jnp.dot(a, b, preferred_element_type=F32)


def _tok_load(ref, tm):
    return jnp.concatenate([ref[pl.ds(c, tm, stride=SUB), :] for c in range(SUB)], axis=1)


def _tok_store(ref, val, tm):
    for c in range(SUB):
        ref[pl.ds(c, tm, stride=SUB), :] = val[:, c * LANES:(c + 1) * LANES]


def _tile_copy(src, src_tok, dst, dst_tok, sem):
    s = pl.multiple_of(src_tok * SUB, SUB)
    t = pl.multiple_of(dst_tok * SUB, SUB)
    return pltpu.make_async_copy(src.at[pl.ds(s, SUB), :], dst.at[pl.ds(t, SUB), :], sem)


def _layer_norm(h, g, b):
    mu = jnp.mean(h, axis=-1, keepdims=True)
    c = h - mu
    var = jnp.mean(c * c, axis=-1, keepdims=True)
    return c * lax.rsqrt(var + LN_EPS) * g + b


def _kv_kernel(mem_ref, wkt_ref, wv_ref, kt_ref, v_ref):
    mb = mem_ref[0].astype(BF16)
    kt = lax.dot_general(wkt_ref[...], mb, (((1,), (1,)), ((), ())), preferred_element_type=F32)
    kt_ref[0] = kt.astype(BF16)
    v_ref[0] = _dot(mb, wv_ref[...]).astype(BF16)


def _kv_proj(mem, wkt, wv):
    bn, m, d = mem.shape
    return pl.pallas_call(
        _kv_kernel,
        grid=(bn,),
        in_specs=[
            pl.BlockSpec((1, m, d), lambda b: (b, 0, 0)),
            pl.BlockSpec((d, d), lambda b: (0, 0)),
            pl.BlockSpec((d, d), lambda b: (0, 0)),
        ],
        out_specs=[
            pl.BlockSpec((1, d, m), lambda b: (b, 0, 0)),
            pl.BlockSpec((1, m, d), lambda b: (b, 0, 0)),
        ],
        out_shape=[jax.ShapeDtypeStruct((bn, d, m), BF16), jax.ShapeDtypeStruct((bn, m, d), BF16)],
        compiler_params=pltpu.CompilerParams(dimension_semantics=("arbitrary",)),
        name="kv_proj",
    )(mem, wkt, wv)


def _mixer_kernel(x_ref, win_ref, bg_ref, cw_ref, wco_ref, wpool_ref, ps_ref, kt_ref, v_ref, wxo_ref, wo_ref,
                  g1_ref, b1_ref, wrh_ref, wrl_ref, br_ref,
                  h1_ref, route_ref, ri_ref, cnt_ref,
                  ubuf, pbuf, carry):
    tm = TM_MIX
    d = D_MODEL
    b = pl.program_id(0)
    s = pl.program_id(1)

    @pl.when(s == 0)
    def _():
        ubuf[0:CONV_HALO, :] = jnp.zeros((CONV_HALO, d), F32)
        pbuf[0:POOL_HALO, :] = jnp.zeros((POOL_HALO, d), F32)

    @pl.when((s == 0) & (b == 0))
    def _():
        carry[...] = jnp.zeros_like(carry)

    x = x_ref[...]
    xb = x.astype(BF16)

    def proj(sec):
        return _dot(xb, win_ref[:, sec * d:(sec + 1) * d])

    def gate(i):
        return jax.nn.sigmoid(proj(5 + i) + bg_ref[i:i + 1, :])

    u = proj(1) * proj(2)
    ubuf[CONV_HALO:CONV_HALO + tm, :] = u
    v = cw_ref[CONV_K - 1:CONV_K, :] * u
    for k in range(CONV_K - 1):
        off = CONV_HALO - (CONV_K - 1) + k
        v = v + cw_ref[k:k + 1, :] * ubuf[off:off + tm, :]
    ubuf[0:CONV_HALO, :] = ubuf[tm:tm + CONV_HALO, :]
    y_conv = _dot((proj(0) * v).astype(BF16), wco_ref[...])
    acc = gate(0) * y_conv

    p = proj(3)
    pbuf[POOL_HALO:POOL_HALO + tm, :] = p
    pos = (s * tm + 1 + lax.broadcasted_iota(jnp.int32, (tm, 1), 0)).astype(F32)
    ys = []
    for gi, w in enumerate(POOL_WINDOWS):
        c0 = gi * POOL_GROUP_DIM
        c1 = c0 + POOL_GROUP_DIM
        pg = p[:, c0:c1]
        ws = pg
        for j in range(1, w):
            ws = ws + pbuf[POOL_HALO - j:POOL_HALO - j + tm, c0:c1]
        inv_cnt = 1.0 / jnp.minimum(pos, float(w))
        dg = (ws * inv_cnt - pg).astype(BF16)
        ys.append(_dot(dg, wpool_ref[gi]))
    pbuf[0:POOL_HALO, :] = pbuf[tm:tm + POOL_HALO, :]
    y_pool = jnp.concatenate(ys, axis=1) * ps_ref[...]
    acc = acc + gate(1) * y_pool

    qb = proj(4).astype(BF16)
    scale = X_HEAD_DIM ** -0.5
    os_ = []
    for h in range(X_HEADS):
        c0 = h * X_HEAD_DIM
        c1 = c0 + X_HEAD_DIM
        sc = _dot(qb[:, c0:c1], kt_ref[0, c0:c1, :]) * scale
        e = jnp.exp(sc - jnp.max(sc, axis=-1, keepdims=True))
        a = e * (1.0 / jnp.sum(e, axis=-1, keepdims=True))
        os_.append(_dot(a.astype(BF16), v_ref[0, :, c0:c1]))
    y_mem = _dot(jnp.concatenate(os_, axis=1).astype(BF16), wxo_ref[...])
    acc = acc + gate(2) * y_mem

    mix = _dot(acc.astype(BF16), wo_ref[...])
    h1 = _layer_norm(ALPHA * x + mix, g1_ref[...], b1_ref[...])
    _tok_store(h1_ref, h1, tm)

    hh = h1.astype(BF16)
    hl = (h1 - hh.astype(F32)).astype(BF16)
    logits = _dot(hh, wrh_ref[...]) + _dot(hl, wrh_ref[...]) + _dot(hh, wrl_ref[...]) + br_ref[...]
    lane = lax.broadcasted_iota(jnp.int32, (tm, LANES), 1)
    lane_f = lane.astype(F32)
    neg = jnp.float32(-jnp.inf)
    big = jnp.float32(1e9)

    def first_argmax(vals):
        m = jnp.max(vals, axis=-1, keepdims=True)
        idx = jnp.min(jnp.where(vals == m, lane_f, big), axis=-1, keepdims=True)
        return m, idx

    is_g = (lane >= N_EXPERTS) & (lane < N_EXPERTS + N_GROUPS)
    gmax, gidx = first_argmax(jnp.where(is_g, logits, neg))
    g_w = 1.0 / jnp.sum(jnp.where(is_g, jnp.exp(logits - gmax), 0.0), axis=-1, keepdims=True)
    gsel = gidx.astype(jnp.int32) - N_EXPERTS
    in_grp = (lane >> 3) == gsel
    le = jnp.where(in_grp, logits, neg)
    m1, i1 = first_argmax(le)
    m2, i2 = first_argmax(jnp.where(lane_f == i1, neg, le))
    t = jnp.exp(m2 - m1)
    den = 1.0 + t
    w1 = g_w / den
    w2 = g_w * t / den

    sel1 = lane_f == i1
    sel2 = lane_f == i2
    onehot = (sel1 | sel2).astype(BF16)
    row_i = lax.broadcasted_iota(jnp.int32, (tm, tm), 0)
    col_i = lax.broadcasted_iota(jnp.int32, (tm, tm), 1)
    before = (col_i < row_i).astype(BF16)
    prior = _dot(before, onehot) + carry[0:1, :]
    r1 = jnp.sum(jnp.where(sel1, prior, 0.0), axis=-1, keepdims=True)
    r2 = jnp.sum(jnp.where(sel2, prior, 0.0), axis=-1, keepdims=True)
    carry[...] = carry[...] + jnp.sum(onehot.astype(F32), axis=0, keepdims=True)
    cnt_ref[...] = carry[...].astype(jnp.int32)

    route = jnp.where(lane == 0, i1, 0.0)
    for k, col in enumerate((i2, r1, r2, w1, w2), start=1):
        route = jnp.where(lane == k, col, route)
    route_ref[...] = route
    ri_ref[0] = route.T[0:8, :].astype(jnp.int32)


def _mixer(x2, w_in_b, b_gate3, conv_w, wco_b, wpool_b, pool_scale, kt, v, wxo_b, wo_b, ln_g, ln_b, wr_hi, wr_lo, br,
           bn, seq):
    n, d = x2.shape
    tm = TM_MIX
    spb = seq // tm
    const2 = lambda b, s: (0, 0)
    const3 = lambda b, s: (0, 0, 0)
    one = pl.Buffered(1)
    row_map = lambda b, s: (b * spb + s, 0)
    return pl.pallas_call(
        _mixer_kernel,
        grid=(bn, spb),
        in_specs=[
            pl.BlockSpec((tm, d), row_map),
            pl.BlockSpec(w_in_b.shape, const2, pipeline_mode=one),
            pl.BlockSpec(b_gate3.shape, const2, pipeline_mode=one),
            pl.BlockSpec(conv_w.shape, const2, pipeline_mode=one),
            pl.BlockSpec(wco_b.shape, const2, pipeline_mode=one),
            pl.BlockSpec(wpool_b.shape, const3, pipeline_mode=one),
            pl.BlockSpec(pool_scale.shape, const2, pipeline_mode=one),
            pl.BlockSpec((1,) + kt.shape[1:], lambda b, s: (b, 0, 0)),
            pl.BlockSpec((1,) + v.shape[1:], lambda b, s: (b, 0, 0)),
            pl.BlockSpec(wxo_b.shape, const2, pipeline_mode=one),
            pl.BlockSpec(wo_b.shape, const2, pipeline_mode=one),
            pl.BlockSpec(ln_g.shape, const2, pipeline_mode=one),
            pl.BlockSpec(ln_b.shape, const2, pipeline_mode=one),
            pl.BlockSpec(wr_hi.shape, const2, pipeline_mode=one),
            pl.BlockSpec(wr_lo.shape, const2, pipeline_mode=one),
            pl.BlockSpec(br.shape, const2, pipeline_mode=one),
        ],
        out_specs=[
            pl.BlockSpec((tm * SUB, LANES), row_map),
            pl.BlockSpec((tm, LANES), row_map),
            pl.BlockSpec((1, 8, tm), lambda b, s: (b * spb + s, 0, 0)),
            pl.BlockSpec((8, LANES), const2),
        ],
        out_shape=[
            jax.ShapeDtypeStruct((n * SUB, LANES), F32),
            jax.ShapeDtypeStruct((n, LANES), F32),
            jax.ShapeDtypeStruct((n // tm, 8, tm), jnp.int32),
            jax.ShapeDtypeStruct((8, LANES), jnp.int32),
        ],
        scratch_shapes=[
            pltpu.VMEM((tm + CONV_HALO, d), F32),
            pltpu.VMEM((tm + POOL_HALO, d), F32),
            pltpu.VMEM((8, LANES), F32),
        ],
        compiler_params=pltpu.CompilerParams(
            dimension_semantics=("arbitrary", "arbitrary"), vmem_limit_bytes=VMEM_LIMIT),
        name="mixer",
    )(x2, w_in_b, b_gate3, conv_w, wco_b, wpool_b, pool_scale, kt, v, wxo_b, wo_b, ln_g, ln_b, wr_hi, wr_lo, br)


def _dest_kernel(ps_ref, ri_ref, dest_ref):
    ri = ri_ref[0]
    e = ri[0:TOP_K, :]
    start = jnp.zeros_like(e)
    for j in range(N_EXPERTS):
        start = jnp.where(e == j, ps_ref[j], start)
    dest_ref[0] = jnp.zeros(dest_ref.shape[1:], jnp.int32)
    dest_ref[0, 0:TOP_K, :] = start + ri[TOP_K:2 * TOP_K, :]


def _dest(pstarts, route_i):
    nt, rows, tm = route_i.shape
    spec = pl.BlockSpec((1, rows, tm), lambda i, ps: (i, 0, 0))
    return pl.pallas_call(
        _dest_kernel,
        grid_spec=pltpu.PrefetchScalarGridSpec(num_scalar_prefetch=1, grid=(nt,), in_specs=[spec], out_specs=spec),
        out_shape=jax.ShapeDtypeStruct(route_i.shape, jnp.int32),
        compiler_params=pltpu.CompilerParams(dimension_semantics=("arbitrary",)),
        name="dest_rows",
    )(pstarts, route_i)


def _scatter_kernel(zt_ref, dest_ref, h_ref, xbuf_ref, zbuf, sem):
    i = pl.program_id(0)
    tm = TM_ROW

    def zero_copy(e):
        start = pl.multiple_of(zt_ref[e] * SUB, TM_EXP * SUB)
        return pltpu.make_async_copy(zbuf, xbuf_ref.at[pl.ds(start, TM_EXP * SUB), :], sem.at[2])

    @pl.when(i == 0)
    def _():
        zbuf[...] = jnp.zeros_like(zbuf)

        def start(e, c):
            @pl.when(zt_ref[e] >= 0)
            def _():
                zero_copy(e).start()
            return c

        def wait(e, c):
            @pl.when(zt_ref[e] >= 0)
            def _():
                zero_copy(e).wait()
            return c

        lax.fori_loop(0, 2 * N_EXPERTS, start, 0)
        lax.fori_loop(0, 2 * N_EXPERTS, wait, 0)

    def issue(r, c):
        for k in range(TOP_K):
            _tile_copy(h_ref, r, xbuf_ref, dest_ref[0, k, r], sem.at[k]).start(priority=k)
        return c

    lax.fori_loop(0, tm, issue, 0, unroll=8)
    for k in range(TOP_K):
        pltpu.make_async_copy(h_ref, xbuf_ref.at[pl.ds(0, tm * SUB), :], sem.at[k]).wait()


def _scatter(zero_tiles, dest, h1t, p_rows):
    tm = TM_ROW
    nt = dest.shape[0]
    return pl.pallas_call(
        _scatter_kernel,
        grid_spec=pltpu.PrefetchScalarGridSpec(
            num_scalar_prefetch=1,
            grid=(nt,),
            in_specs=[
                pl.BlockSpec((1, 8, tm), lambda i, zt: (i, 0, 0), memory_space=pltpu.SMEM),
                pl.BlockSpec((tm * SUB, LANES), lambda i, zt: (i, 0)),
            ],
            out_specs=pl.BlockSpec(memory_space=pl.ANY),
            scratch_shapes=[pltpu.VMEM((TM_EXP * SUB, LANES), F32), pltpu.SemaphoreType.DMA((3,))],
        ),
        out_shape=jax.ShapeDtypeStruct((p_rows * SUB, LANES), F32),
        compiler_params=pltpu.CompilerParams(dimension_semantics=("arbitrary",)),
        name="scatter_rows",
    )(zero_tiles, dest, h1t)


def _expert_kernel(te_ref, nt_ref, x_ref, wup_ref, wdn_ref, y_ref, wup_b, wdn_b):
    i = pl.program_id(0)
    live = i < nt_ref[0]

    @pl.when(jnp.logical_not(live))
    def _():
        y_ref[...] = jnp.zeros_like(y_ref)

    @pl.when(live & ((i == 0) | (te_ref[i] != te_ref[jnp.maximum(i - 1, 0)])))
    def _():
        wup_b[...] = wup_ref[0].astype(BF16)
        wdn_b[...] = wdn_ref[0].astype(BF16)

    @pl.when(live)
    def _():
        xb = _tok_load(x_ref, TM_EXP).astype(BF16)
        hgv = _dot(xb, wup_b[...])
        hg = hgv[:, :D_EXPERT]
        hv = hgv[:, D_EXPERT:]
        act = (hg * jax.nn.sigmoid(hg)) * hv
        _tok_store(y_ref, _dot(act.astype(BF16), wdn_b[...]), TM_EXP)


def _experts(tile_expert, n_tiles, xbuf, w_up, w_down):
    tm = TM_EXP
    row_map = lambda i, te, nt: (i, 0)
    w_map = lambda i, te, nt: (te[i], 0, 0)
    return pl.pallas_call(
        _expert_kernel,
        grid_spec=pltpu.PrefetchScalarGridSpec(
            num_scalar_prefetch=2,
            grid=(xbuf.shape[0] // (tm * SUB),),
            in_specs=[
                pl.BlockSpec((tm * SUB, LANES), row_map),
                pl.BlockSpec((1,) + w_up.shape[1:], w_map),
                pl.BlockSpec((1,) + w_down.shape[1:], w_map),
            ],
            out_specs=pl.BlockSpec((tm * SUB, LANES), row_map),
            scratch_shapes=[pltpu.VMEM(w_up.shape[1:], BF16), pltpu.VMEM(w_down.shape[1:], BF16)],
        ),
        out_shape=jax.ShapeDtypeStruct(xbuf.shape, F32),
        compiler_params=pltpu.CompilerParams(dimension_semantics=("arbitrary",), vmem_limit_bytes=VMEM_LIMIT),
        name="experts",
    )(tile_expert, n_tiles, xbuf, w_up, w_down)


def _combine_kernel(dest_ref, h_ref, route_ref, ybuf_ref, g_ref, b_ref, out_ref, ybufs, sem):
    tm = TM_ROW

    def issue(r, c):
        for k in range(TOP_K):
            _tile_copy(ybuf_ref, dest_ref[0, k, r], ybufs.at[k], r, sem.at[k]).start(priority=k)
        return c

    lax.fori_loop(0, tm, issue, 0, unroll=8)
    for k in range(TOP_K):
        pltpu.make_async_copy(ybuf_ref.at[pl.ds(0, tm * SUB), :], ybufs.at[k], sem.at[k]).wait()
    route = route_ref[...]
    ffn = route[:, 4:5] * _tok_load(ybufs.at[0], tm) + route[:, 5:6] * _tok_load(ybufs.at[1], tm)
    out_ref[...] = _layer_norm(ALPHA * _tok_load(h_ref, tm) + ffn, g_ref[...], b_ref[...])


def _combine(dest, h1t, route, ybuf, ln_g, ln_b):
    tm = TM_ROW
    nt = dest.shape[0]
    n = nt * tm
    return pl.pallas_call(
        _combine_kernel,
        grid=(nt,),
        in_specs=[
            pl.BlockSpec((1, 8, tm), lambda i: (i, 0, 0), memory_space=pltpu.SMEM),
            pl.BlockSpec((tm * SUB, LANES), lambda i: (i, 0)),
            pl.BlockSpec((tm, LANES), lambda i: (i, 0)),
            pl.BlockSpec(memory_space=pl.ANY),
            pl.BlockSpec(ln_g.shape, lambda i: (0, 0)),
            pl.BlockSpec(ln_b.shape, lambda i: (0, 0)),
        ],
        out_specs=pl.BlockSpec((tm, D_MODEL), lambda i: (i, 0)),
        out_shape=jax.ShapeDtypeStruct((n, D_MODEL), F32),
        scratch_shapes=[pltpu.VMEM((TOP_K, tm * SUB, LANES), F32), pltpu.SemaphoreType.DMA((TOP_K,))],
        compiler_params=pltpu.CompilerParams(dimension_semantics=("arbitrary",)),
        name="combine_ln2",
    )(dest, h1t, route, ybuf, ln_g, ln_b)


def _layer(h, mem, w_in, b_gate, conv_w, w_conv_out, w_pool, pool_scale, w_kv, w_xo, w_o, ln1_g, ln1_b,
           w_rg, b_rg, w_re, b_re, w_up, w_down, ln2_g, ln2_b):
    bn, seq, d = h.shape
    n = bn * seq
    xw = X_HEADS * X_HEAD_DIM

    kt, v = _kv_proj(mem, w_kv[:, :xw].T.astype(BF16), w_kv[:, xw:].astype(BF16))

    pad = LANES - N_EXPERTS - N_GROUPS
    w_r = jnp.concatenate([w_re, w_rg, jnp.zeros((d, pad), F32)], axis=1)
    wr_hi = w_r.astype(BF16)
    wr_lo = (w_r - wr_hi.astype(F32)).astype(BF16)
    b_r = jnp.concatenate([b_re, b_rg, jnp.zeros((pad,), F32)])[None, :]

    h1t, route, route_i, cnt = _mixer(
        h.reshape(n, d), w_in.astype(BF16), b_gate.reshape(3, d), conv_w, w_conv_out.astype(BF16),
        w_pool.astype(BF16), pool_scale[None, :], kt, v, w_xo.astype(BF16), w_o.astype(BF16),
        ln1_g[None, :], ln1_b[None, :], wr_hi, wr_lo, b_r, bn, seq)

    counts = cnt[0, :N_EXPERTS]
    padded = (counts + TM_EXP - 1) // TM_EXP * TM_EXP
    pends = jnp.cumsum(padded)
    pstarts = pends - padded
    max_tiles = (n * TOP_K + N_EXPERTS * (TM_EXP - 1)) // TM_EXP
    p_rows = max_tiles * TM_EXP
    n_tiles = (pends[-1] // TM_EXP).astype(jnp.int32)
    tile_ids = jnp.minimum(jnp.arange(max_tiles, dtype=jnp.int32), n_tiles - 1)
    tile_expert = jnp.sum(pends[None, :] <= (tile_ids * TM_EXP)[:, None], axis=1, dtype=jnp.int32)
    tile_expert = jnp.minimum(tile_expert, N_EXPERTS - 1)
    tail_ids = n_tiles + jnp.arange(N_EXPERTS, dtype=jnp.int32)
    zero_tiles = jnp.concatenate([
        jnp.where(counts % TM_EXP != 0, pends - TM_EXP, -1),
        jnp.where(tail_ids < max_tiles, tail_ids * TM_EXP, -1)]).astype(jnp.int32)

    dest = _dest(pstarts.astype(jnp.int32), route_i)
    xbuf = _scatter(zero_tiles, dest, h1t, p_rows)
    ybuf = _experts(tile_expert, n_tiles.reshape(1), xbuf, w_up, w_down)
    out = _combine(dest, h1t, route, ybuf, ln2_g[None, :], ln2_b[None, :])
    return out.reshape(bn, seq, d)


def kernel(x, mem, w_in, b_gate, conv_w, w_conv_out, w_pool, pool_scale, w_kv, w_xo, w_o, ln1_g, ln1_b,
           w_router_group, b_router_group, w_router_expert, b_router_expert, w_up, w_down, ln2_g, ln2_b):
    h = x
    for l in range(DEPTH):
        h = _layer(h, mem, w_in[l], b_gate[l], conv_w[l], w_conv_out[l], w_pool[l], pool_scale[l], w_kv[l],
                   w_xo[l], w_o[l], ln1_g[l], ln1_b[l], w_router_group[l], b_router_group[l],
                   w_router_expert[l], b_router_expert[l], w_up[l], w_down[l], ln2_g[l], ln2_b[l])
    return h
```

```python
import functools

import jax
import jax.numpy as jnp
from jax import lax
from jax.experimental import pallas as pl
from jax.experimental.pallas import tpu as pltpu

D_MODEL = 1024
CONV_K = 3
POOL_WINDOWS = (2, 4, 8, 16)
POOL_GROUP_DIM = 256
X_HEADS = 4
X_HEAD_DIM = 256
N_GROUPS = 8
EXPERTS_PER_GROUP = 8
N_EXPERTS = 64
TOP_K = 2
D_EXPERT = 512
DEPTH = 1
ALPHA = (2.0 * DEPTH) ** 0.25
LN_EPS = 1e-5

LANES = 128
SUB = 8
assert D_MODEL == SUB * LANES
POOL_HALO = 16
CONV_HALO = 8
TM_MIX = 512
TM_EXP = 256
TM_ROW = TM_MIX
VMEM_LIMIT = 58 * 1024 * 1024

F32 = jnp.float32
BF16 = jnp.bfloat16


def _dot(a, b):
    return jnp.dot(a, b, preferred_element_type=F32)


def _tok_load(ref, tm):
    return jnp.concatenate([ref[pl.ds(c, tm, stride=SUB), :] for c in range(SUB)], axis=1)


def _tok_store(ref, val, tm):
    for c in range(SUB):
        ref[pl.ds(c, tm, stride=SUB), :] = val[:, c * LANES:(c + 1) * LANES]


def _tile_copy(src, src_tok, dst, dst_tok, sem):
    s = pl.multiple_of(src_tok * SUB, SUB)
    t = pl.multiple_of(dst_tok * SUB, SUB)
    return pltpu.make_async_copy(src.at[pl.ds(s, SUB), :], dst.at[pl.ds(t, SUB), :], sem)


def _layer_norm(h, g, b):
    mu = jnp.mean(h, axis=-1, keepdims=True)
    c = h - mu
    var = jnp.mean(c * c, axis=-1, keepdims=True)
    return c * lax.rsqrt(var + LN_EPS) * g + b


def _kv_kernel(mem_ref, wkt_ref, wv_ref, kt_ref, v_ref):
    mb = mem_ref[0].astype(BF16)
    kt = lax.dot_general(wkt_ref[...], mb, (((1,), (1,)), ((), ())), preferred_element_type=F32)
    kt_ref[0] = kt.astype(BF16)
    v_ref[0] = _dot(mb, wv_ref[...]).astype(BF16)


def _kv_proj(mem, wkt, wv):
    bn, m, d = mem.shape
    return pl.pallas_call(
        _kv_kernel,
        grid=(bn,),
        in_specs=[
            pl.BlockSpec((1, m, d), lambda b: (b, 0, 0)),
            pl.BlockSpec((d, d), lambda b: (0, 0)),
            pl.BlockSpec((d, d), lambda b: (0, 0)),
        ],
        out_specs=[
            pl.BlockSpec((1, d, m), lambda b: (b, 0, 0)),
            pl.BlockSpec((1, m, d), lambda b: (b, 0, 0)),
        ],
        out_shape=[jax.ShapeDtypeStruct((bn, d, m), BF16), jax.ShapeDtypeStruct((bn, m, d), BF16)],
        compiler_params=pltpu.CompilerParams(dimension_semantics=("arbitrary",)),
        name="kv_proj",
    )(mem, wkt, wv)


def _mixer_kernel(x_ref, win_ref, bg_ref, cw_ref, wco_ref, wpool_ref, ps_ref, kt_ref, v_ref, wxo_ref, wo_ref,
                  g1_ref, b1_ref, wrh_ref, wrl_ref, br_ref,
                  h1_ref, route_ref, ri_ref, cnt_ref,
                  ubuf, pbuf, carry):
    tm = TM_MIX
    d = D_MODEL
    b = pl.program_id(0)
    s = pl.program_id(1)

    @pl.when(s == 0)
    def _():
        ubuf[0:CONV_HALO, :] = jnp.zeros((CONV_HALO, d), F32)
        pbuf[0:POOL_HALO, :] = jnp.zeros((POOL_HALO, d), F32)

    @pl.when((s == 0) & (b == 0))
    def _():
        carry[...] = jnp.zeros_like(carry)

    x = x_ref[...]
    xb = x.astype(BF16)

    def proj(sec):
        return _dot(xb, win_ref[:, sec * d:(sec + 1) * d])

    def gate(i):
        return jax.nn.sigmoid(proj(5 + i) + bg_ref[i:i + 1, :])

    u = proj(1) * proj(2)
    ubuf[CONV_HALO:CONV_HALO + tm, :] = u
    v = cw_ref[CONV_K - 1:CONV_K, :] * u
    for k in range(CONV_K - 1):
        off = CONV_HALO - (CONV_K - 1) + k
        v = v + cw_ref[k:k + 1, :] * ubuf[off:off + tm, :]
    ubuf[0:CONV_HALO, :] = ubuf[tm:tm + CONV_HALO, :]
    y_conv = _dot((proj(0) * v).astype(BF16), wco_ref[...])
    acc = gate(0) * y_conv

    p = proj(3)
    pbuf[POOL_HALO:POOL_HALO + tm, :] = p
    pos = (s * tm + 1 + lax.broadcasted_iota(jnp.int32, (tm, 1), 0)).astype(F32)
    ys = []
    for gi, w in enumerate(POOL_WINDOWS):
        c0 = gi * POOL_GROUP_DIM
        c1 = c0 + POOL_GROUP_DIM
        pg = p[:, c0:c1]
        ws = pg
        for j in range(1, w):
            ws = ws + pbuf[POOL_HALO - j:POOL_HALO - j + tm, c0:c1]
        inv_cnt = 1.0 / jnp.minimum(pos, float(w))
        dg = (ws * inv_cnt - pg).astype(BF16)
        ys.append(_dot(dg, wpool_ref[gi]))
    pbuf[0:POOL_HALO, :] = pbuf[tm:tm + POOL_HALO, :]
    y_pool = jnp.concatenate(ys, axis=1) * ps_ref[...]
    acc = acc + gate(1) * y_pool

    qb = proj(4).astype(BF16)
    scale = X_HEAD_DIM ** -0.5
    os_ = []
    for h in range(X_HEADS):
        c0 = h * X_HEAD_DIM
        c1 = c0 + X_HEAD_DIM
        sc = _dot(qb[:, c0:c1], kt_ref[0, c0:c1, :]) * scale
        e = jnp.exp(sc - jnp.max(sc, axis=-1, keepdims=True))
        a = e * (1.0 / jnp.sum(e, axis=-1, keepdims=True))
        os_.append(_dot(a.astype(BF16), v_ref[0, :, c0:c1]))
    y_mem = _dot(jnp.concatenate(os_, axis=1).astype(BF16), wxo_ref[...])
    acc = acc + gate(2) * y_mem

    mix = _dot(acc.astype(BF16), wo_ref[...])
    h1 = _layer_norm(ALPHA * x + mix, g1_ref[...], b1_ref[...])
    _tok_store(h1_ref, h1, tm)

    hh = h1.astype(BF16)
    hl = (h1 - hh.astype(F32)).astype(BF16)
    logits = _dot(hh, wrh_ref[...]) + _dot(hl, wrh_ref[...]) + _dot(hh, wrl_ref[...]) + br_ref[...]
    lane = lax.broadcasted_iota(jnp.int32, (tm, LANES), 1)
    lane_f = lane.astype(F32)
    neg = jnp.float32(-jnp.inf)
    big = jnp.float32(1e9)

    def first_argmax(vals):
        m = jnp.max(vals, axis=-1, keepdims=True)
        idx = jnp.min(jnp.where(vals == m, lane_f, big), axis=-1, keepdims=True)
        return m, idx

    is_g = (lane >= N_EXPERTS) & (lane < N_EXPERTS + N_GROUPS)
    gmax, gidx = first_argmax(jnp.where(is_g, logits, neg))
    g_w = 1.0 / jnp.sum(jnp.where(is_g, jnp.exp(logits - gmax), 0.0), axis=-1, keepdims=True)
    gsel = gidx.astype(jnp.int32) - N_EXPERTS
    in_grp = (lane >> 3) == gsel
    le = jnp.where(in_grp, logits, neg)
    m1, i1 = first_argmax(le)
    m2, i2 = first_argmax(jnp.where(lane_f == i1, neg, le))
    t = jnp.exp(m2 - m1)
    den = 1.0 + t
    w1 = g_w / den
    w2 = g_w * t / den

    sel1 = lane_f == i1
    sel2 = lane_f == i2
    onehot = (sel1 | sel2).astype(BF16)
    row_i = lax.broadcasted_iota(jnp.int32, (tm, tm), 0)
    col_i = lax.broadcasted_iota(jnp.int32, (tm, tm), 1)
    before = (col_i < row_i).astype(BF16)
    prior = _dot(before, onehot) + carry[0:1, :]
    r1 = jnp.sum(jnp.where(sel1, prior, 0.0), axis=-1, keepdims=True)
    r2 = jnp.sum(jnp.where(sel2, prior, 0.0), axis=-1, keepdims=True)
    carry[...] = carry[...] + jnp.sum(onehot.astype(F32), axis=0, keepdims=True)
    cnt_ref[...] = carry[...].astype(jnp.int32)

    route = jnp.where(lane == 0, i1, 0.0)
    for k, col in enumerate((i2, r1, r2, w1, w2), start=1):
        route = jnp.where(lane == k, col, route)
    route_ref[...] = route
    ri_ref[0] = route.T[0:8, :].astype(jnp.int32)


def _mixer(x2, w_in_b, b_gate3, conv_w, wco_b, wpool_b, pool_scale, kt, v, wxo_b, wo_b, ln_g, ln_b, wr_hi, wr_lo, br,
           bn, seq):
    n, d = x2.shape
    tm = TM_MIX
    spb = seq // tm
    const2 = lambda b, s: (0, 0)
    const3 = lambda b, s: (0, 0, 0)
    one = pl.Buffered(1)
    row_map = lambda b, s: (b * spb + s, 0)
    return pl.pallas_call(
        _mixer_kernel,
        grid=(bn, spb),
        in_specs=[
            pl.BlockSpec((tm, d), row_map),
            pl.BlockSpec(w_in_b.shape, const2, pipeline_mode=one),
            pl.BlockSpec(b_gate3.shape, const2, pipeline_mode=one),
            pl.BlockSpec(conv_w.shape, const2, pipeline_mode=one),
            pl.BlockSpec(wco_b.shape, const2, pipeline_mode=one),
            pl.BlockSpec(wpool_b.shape, const3, pipeline_mode=one),
            pl.BlockSpec(pool_scale.shape, const2, pipeline_mode=one),
            pl.BlockSpec((1,) + kt.shape[1:], lambda b, s: (b, 0, 0)),
            pl.BlockSpec((1,) + v.shape[1:], lambda b, s: (b, 0, 0)),
            pl.BlockSpec(wxo_b.shape, const2, pipeline_mode=one),
            pl.BlockSpec(wo_b.shape, const2, pipeline_mode=one),
            pl.BlockSpec(ln_g.shape, const2, pipeline_mode=one),
            pl.BlockSpec(ln_b.shape, const2, pipeline_mode=one),
            pl.BlockSpec(wr_hi.shape, const2, pipeline_mode=one),
            pl.BlockSpec(wr_lo.shape, const2, pipeline_mode=one),
            pl.BlockSpec(br.shape, const2, pipeline_mode=one),
        ],
        out_specs=[
            pl.BlockSpec((tm * SUB, LANES), row_map),
            pl.BlockSpec((tm, LANES), row_map),
            pl.BlockSpec((1, 8, tm), lambda b, s: (b * spb + s, 0, 0)),
            pl.BlockSpec((8, LANES), const2),
        ],
        out_shape=[
            jax.ShapeDtypeStruct((n * SUB, LANES), F32),
            jax.ShapeDtypeStruct((n, LANES), F32),
            jax.ShapeDtypeStruct((n // tm, 8, tm), jnp.int32),
            jax.ShapeDtypeStruct((8, LANES), jnp.int32),
        ],
        scratch_shapes=[
            pltpu.VMEM((tm + CONV_HALO, d), F32),
            pltpu.VMEM((tm + POOL_HALO, d), F32),
            pltpu.VMEM((8, LANES), F32),
        ],
        compiler_params=pltpu.CompilerParams(
            dimension_semantics=("arbitrary", "arbitrary"), vmem_limit_bytes=VMEM_LIMIT),
        name="mixer",
    )(x2, w_in_b, b_gate3, conv_w, wco_b, wpool_b, pool_scale, kt, v, wxo_b, wo_b, ln_g, ln_b, wr_hi, wr_lo, br)


def _dest_kernel(ps_ref, ri_ref, dest_ref):
    dest_ref[...] = jnp.zeros(dest_ref.shape, jnp.int32)
    for t in range(dest_ref.shape[0]):
        ri = ri_ref[t]
        e = ri[0:TOP_K, :]
        start = jnp.zeros_like(e)
        for j in range(N_EXPERTS):
            start = jnp.where(e == j, ps_ref[j], start)
        dest_ref[t, 0:TOP_K, :] = start + ri[TOP_K:2 * TOP_K, :]


def _dest(pstarts, route_i):
    nt, rows, tm = route_i.shape
    blk = 8
    spec = pl.BlockSpec((blk, rows, tm), lambda i, ps: (i, 0, 0))
    return pl.pallas_call(
        _dest_kernel,
        grid_spec=pltpu.PrefetchScalarGridSpec(
            num_scalar_prefetch=1, grid=(nt // blk,), in_specs=[spec], out_specs=spec),
        out_shape=jax.ShapeDtypeStruct(route_i.shape, jnp.int32),
        compiler_params=pltpu.CompilerParams(dimension_semantics=("arbitrary",)),
        name="dest_rows",
    )(pstarts, route_i)


def _scatter_kernel(zt_ref, dest_ref, h_ref, xbuf_ref, zbuf, sem):
    i = pl.program_id(0)
    tm = TM_ROW

    def zero_copy(e):
        start = pl.multiple_of(zt_ref[e] * SUB, TM_EXP * SUB)
        return pltpu.make_async_copy(zbuf, xbuf_ref.at[pl.ds(start, TM_EXP * SUB), :], sem.at[2])

    @pl.when(i == 0)
    def _():
        zbuf[...] = jnp.zeros_like(zbuf)

        def start(e, c):
            @pl.when(zt_ref[e] >= 0)
            def _():
                zero_copy(e).start()
            return c

        def wait(e, c):
            @pl.when(zt_ref[e] >= 0)
            def _():
                zero_copy(e).wait()
            return c

        lax.fori_loop(0, 2 * N_EXPERTS, start, 0)
        lax.fori_loop(0, 2 * N_EXPERTS, wait, 0)

    def issue(r, c):
        for k in range(TOP_K):
            _tile_copy(h_ref, r, xbuf_ref, dest_ref[0, k, r], sem.at[k]).start(priority=k)
        return c

    lax.fori_loop(0, tm, issue, 0, unroll=8)
    for k in range(TOP_K):
        pltpu.make_async_copy(h_ref, xbuf_ref.at[pl.ds(0, tm * SUB), :], sem.at[k]).wait()


def _scatter(zero_tiles, dest, h1t, p_rows):
    tm = TM_ROW
    nt = dest.shape[0]
    return pl.pallas_call(
        _scatter_kernel,
        grid_spec=pltpu.PrefetchScalarGridSpec(
            num_scalar_prefetch=1,
            grid=(nt,),
            in_specs=[
                pl.BlockSpec((1, 8, tm), lambda i, zt: (i, 0, 0), memory_space=pltpu.SMEM),
                pl.BlockSpec((tm * SUB, LANES), lambda i, zt: (i, 0)),
            ],
            out_specs=pl.BlockSpec(memory_space=pl.ANY),
            scratch_shapes=[pltpu.VMEM((TM_EXP * SUB, LANES), F32), pltpu.SemaphoreType.DMA((3,))],
        ),
        out_shape=jax.ShapeDtypeStruct((p_rows * SUB, LANES), F32),
        compiler_params=pltpu.CompilerParams(dimension_semantics=("arbitrary",)),
        name="scatter_rows",
    )(zero_tiles, dest, h1t)


def _expert_kernel(t0_ref, nt_ref, tot_ref, wup_ref, wdn_ref, xbuf_ref, ybuf_ref, wup_b, wdn_b, xs, ys, sem_in, sem_out):
    e = pl.program_id(0)
    t0 = t0_ref[e]
    nt = nt_ref[e]
    total = tot_ref[0]
    rows = TM_EXP * SUB
    max_tiles = xbuf_ref.shape[0] // rows

    def hbm_tile(ref, g):
        return ref.at[pl.ds(pl.multiple_of(g * rows, rows), rows), :]

    def x_copy(g, slot):
        return pltpu.make_async_copy(hbm_tile(xbuf_ref, g), xs.at[slot], sem_in.at[slot])

    def y_copy(g, slot):
        return pltpu.make_async_copy(ys.at[slot], hbm_tile(ybuf_ref, g), sem_out.at[slot])

    @pl.when(e == 0)
    def _():
        x_copy(0, 0).start()

    @pl.when(nt > 0)
    def _():
        wup_b[...] = wup_ref[0].astype(BF16)
        wdn_b[...] = wdn_ref[0].astype(BF16)

    def tile(j, c):
        g = t0 + j
        slot = lax.rem(g, 2)

        @pl.when(g + 1 < total)
        def _():
            x_copy(g + 1, 1 - slot).start()

        x_copy(g, slot).wait()

        @pl.when(g >= 2)
        def _():
            y_copy(g - 2, slot).wait()

        xb = _tok_load(xs.at[slot], TM_EXP).astype(BF16)
        hgv = _dot(xb, wup_b[...])
        hg = hgv[:, :D_EXPERT]
        hv = hgv[:, D_EXPERT:]
        act = (hg * jax.nn.sigmoid(hg)) * hv
        _tok_store(ys.at[slot], _dot(act.astype(BF16), wdn_b[...]), TM_EXP)
        y_copy(g, slot).start()
        return c

    lax.fori_loop(0, nt, tile, 0)

    @pl.when(e == pl.num_programs(0) - 1)
    def _():
        @pl.when(total >= 2)
        def _():
            y_copy(total - 2, lax.rem(total, 2)).wait()

        y_copy(total - 1, lax.rem(total - 1, 2)).wait()
        ys[0] = jnp.zeros(ys.shape[1:], F32)

        def zero_tail(g, c):
            y_copy(g, 0).start()
            y_copy(g, 0).wait()
            return c

        lax.fori_loop(total, max_tiles, zero_tail, 0)


def _experts(tile_start, tile_count, n_tiles, xbuf, w_up, w_down):
    w_map = lambda e, t0, nt, tot: (e, 0, 0)
    rows = TM_EXP * SUB
    return pl.pallas_call(
        _expert_kernel,
        grid_spec=pltpu.PrefetchScalarGridSpec(
            num_scalar_prefetch=3,
            grid=(N_EXPERTS,),
            in_specs=[
                pl.BlockSpec((1,) + w_up.shape[1:], w_map),
                pl.BlockSpec((1,) + w_down.shape[1:], w_map),
                pl.BlockSpec(memory_space=pl.ANY),
            ],
            out_specs=pl.BlockSpec(memory_space=pl.ANY),
            scratch_shapes=[
                pltpu.VMEM(w_up.shape[1:], BF16), pltpu.VMEM(w_down.shape[1:], BF16),
                pltpu.VMEM((2, rows, LANES), F32), pltpu.VMEM((2, rows, LANES), F32),
                pltpu.SemaphoreType.DMA((2,)), pltpu.SemaphoreType.DMA((2,)),
            ],
        ),
        out_shape=jax.ShapeDtypeStruct(xbuf.shape, F32),
        compiler_params=pltpu.CompilerParams(dimension_semantics=("arbitrary",), vmem_limit_bytes=VMEM_LIMIT),
        name="experts",
    )(tile_start, tile_count, n_tiles, w_up, w_down, xbuf)


def _combine_kernel(dcur_ref, dnext_ref, h_ref, route_ref, ybuf_ref, g_ref, b_ref, out_ref, ybufs, sem):
    tm = TM_ROW
    i = pl.program_id(0)
    slot = lax.rem(i, 2)

    def gather(d_ref, s):
        def issue(r, c):
            for k in range(TOP_K):
                _tile_copy(ybuf_ref, d_ref[0, k, r], ybufs.at[s, k], r, sem.at[s, k]).start(priority=k)
            return c

        lax.fori_loop(0, tm, issue, 0, unroll=8)

    @pl.when(i == 0)
    def _():
        gather(dcur_ref, 0)

    @pl.when(i + 1 < pl.num_programs(0))
    def _():
        gather(dnext_ref, 1 - slot)

    for k in range(TOP_K):
        pltpu.make_async_copy(ybuf_ref.at[pl.ds(0, tm * SUB), :], ybufs.at[slot, k], sem.at[slot, k]).wait()
    route = route_ref[...]
    ffn = route[:, 4:5] * _tok_load(ybufs.at[slot, 0], tm) + route[:, 5:6] * _tok_load(ybufs.at[slot, 1], tm)
    out_ref[...] = _layer_norm(ALPHA * _tok_load(h_ref, tm) + ffn, g_ref[...], b_ref[...])


def _combine(dest, h1t, route, ybuf, ln_g, ln_b):
    tm = TM_ROW
    nt = dest.shape[0]
    n = nt * tm
    return pl.pallas_call(
        _combine_kernel,
        grid=(nt,),
        in_specs=[
            pl.BlockSpec((1, 8, tm), lambda i: (i, 0, 0), memory_space=pltpu.SMEM),
            pl.BlockSpec((1, 8, tm), lambda i: (jnp.minimum(i + 1, nt - 1), 0, 0), memory_space=pltpu.SMEM),
            pl.BlockSpec((tm * SUB, LANES), lambda i: (i, 0)),
            pl.BlockSpec((tm, LANES), lambda i: (i, 0)),
            pl.BlockSpec(memory_space=pl.ANY),
            pl.BlockSpec(ln_g.shape, lambda i: (0, 0)),
            pl.BlockSpec(ln_b.shape, lambda i: (0, 0)),
        ],
        out_specs=pl.BlockSpec((tm, D_MODEL), lambda i: (i, 0)),
        out_shape=jax.ShapeDtypeStruct((n, D_MODEL), F32),
        scratch_shapes=[pltpu.VMEM((2, TOP_K, tm * SUB, LANES), F32), pltpu.SemaphoreType.DMA((2, TOP_K))],
        compiler_params=pltpu.CompilerParams(dimension_semantics=("arbitrary",)),
        name="combine_ln2",
    )(dest, dest, h1t, route, ybuf, ln_g, ln_b)


def _layer(h, mem, w_in, b_gate, conv_w, w_conv_out, w_pool, pool_scale, w_kv, w_xo, w_o, ln1_g, ln1_b,
           w_rg, b_rg, w_re, b_re, w_up, w_down, ln2_g, ln2_b):
    bn, seq, d = h.shape
    n = bn * seq
    xw = X_HEADS * X_HEAD_DIM

    kt, v = _kv_proj(mem, w_kv[:, :xw].T.astype(BF16), w_kv[:, xw:].astype(BF16))

    pad = LANES - N_EXPERTS - N_GROUPS
    w_r = jnp.concatenate([w_re, w_rg, jnp.zeros((d, pad), F32)], axis=1)
    wr_hi = w_r.astype(BF16)
    wr_lo = (w_r - wr_hi.astype(F32)).astype(BF16)
    b_r = jnp.concatenate([b_re, b_rg, jnp.zeros((pad,), F32)])[None, :]

    h1t, route, route_i, cnt = _mixer(
        h.reshape(n, d), w_in.astype(BF16), b_gate.reshape(3, d), conv_w, w_conv_out.astype(BF16),
        w_pool.astype(BF16), pool_scale[None, :], kt, v, w_xo.astype(BF16), w_o.astype(BF16),
        ln1_g[None, :], ln1_b[None, :], wr_hi, wr_lo, b_r, bn, seq)

    counts = cnt[0, :N_EXPERTS]
    padded = (counts + TM_EXP - 1) // TM_EXP * TM_EXP
    pends = jnp.cumsum(padded)
    pstarts = pends - padded
    max_tiles = (n * TOP_K + N_EXPERTS * (TM_EXP - 1)) // TM_EXP
    p_rows = max_tiles * TM_EXP
    n_tiles = (pends[-1] // TM_EXP).astype(jnp.int32)
    tail_ids = n_tiles + jnp.arange(N_EXPERTS, dtype=jnp.int32)
    zero_tiles = jnp.concatenate([
        jnp.where(counts % TM_EXP != 0, pends - TM_EXP, -1),
        jnp.where(tail_ids < max_tiles, tail_ids * TM_EXP, -1)]).astype(jnp.int32)

    dest = _dest(pstarts.astype(jnp.int32), route_i)
    xbuf = _scatter(zero_tiles, dest, h1t, p_rows)
    ybuf = _experts((pstarts // TM_EXP).astype(jnp.int32), (padded // TM_EXP).astype(jnp.int32), n_tiles.reshape(1),
                    xbuf, w_up, w_down)
    out = _combine(dest, h1t, route, ybuf, ln2_g[None, :], ln2_b[None, :])
    return out.reshape(bn, seq, d)


def kernel(x, mem, w_in, b_gate, conv_w, w_conv_out, w_pool, pool_scale, w_kv, w_xo, w_o, ln1_g, ln1_b,
           w_router_group, b_router_group, w_router_expert, b_router_expert, w_up, w_down, ln2_g, ln2_b):
    h = x
    for l in range(DEPTH):
        h = _layer(h, mem, w_in[l], b_gate[l], conv_w[l], w_conv_out[l], w_pool[l], pool_scale[l], w_kv[l],
                   w_xo[l], w_o[l], ln1_g[l], ln1_b[l], w_router_group[l], b_router_group[l],
                   w_router_expert[l], b_router_expert[l], w_up[l], w_down[l], ln2_g[l], ln2_b[l])
    return h
```

```python
import functools

import jax
import jax.numpy as jnp
from jax import lax
from jax.experimental import pallas as pl
from jax.experimental.pallas import tpu as pltpu

D_MODEL = 1024
CONV_K = 3
POOL_WINDOWS = (2, 4, 8, 16)
POOL_GROUP_DIM = 256
X_HEADS = 4
X_HEAD_DIM = 256
N_GROUPS = 8
EXPERTS_PER_GROUP = 8
N_EXPERTS = 64
TOP_K = 2
D_EXPERT = 512
DEPTH = 1
ALPHA = (2.0 * DEPTH) ** 0.25
LN_EPS = 1e-5

LANES = 128
SUB = 8
assert D_MODEL == SUB * LANES
POOL_HALO = 16
CONV_HALO = 8
TM_MIX = 512
TM_EXP = 256
TM_ROW = TM_MIX
VMEM_LIMIT = 58 * 1024 * 1024
ROW_DMA_PRIORITY = 1

F32 = jnp.float32
BF16 = jnp.bfloat16


def _dot(a, b):
    return jnp.dot(a, b, preferred_element_type=F32)


def _tok_load(ref, tm):
    return jnp.concatenate([ref[pl.ds(c, tm, stride=SUB), :] for c in range(SUB)], axis=1)


def _tok_store(ref, val, tm):
    for c in range(SUB):
        ref[pl.ds(c, tm, stride=SUB), :] = val[:, c * LANES:(c + 1) * LANES]


def _tile_copy(src, src_tok, dst, dst_tok, sem):
    s = pl.multiple_of(src_tok * SUB, SUB)
    t = pl.multiple_of(dst_tok * SUB, SUB)
    return pltpu.make_async_copy(src.at[pl.ds(s, SUB), :], dst.at[pl.ds(t, SUB), :], sem)


def _layer_norm(h, g, b):
    mu = jnp.mean(h, axis=-1, keepdims=True)
    c = h - mu
    var = jnp.mean(c * c, axis=-1, keepdims=True)
    return c * lax.rsqrt(var + LN_EPS) * g + b


def _kv_kernel(mem_ref, wkt_ref, wv_ref, kt_ref, v_ref):
    mb = mem_ref[0].astype(BF16)
    kt = lax.dot_general(wkt_ref[...], mb, (((1,), (1,)), ((), ())), preferred_element_type=F32)
    kt_ref[0] = kt.astype(BF16)
    v_ref[0] = _dot(mb, wv_ref[...]).astype(BF16)


def _kv_proj(mem, wkt, wv):
    bn, m, d = mem.shape
    return pl.pallas_call(
        _kv_kernel,
        grid=(bn,),
        in_specs=[
            pl.BlockSpec((1, m, d), lambda b: (b, 0, 0)),
            pl.BlockSpec((d, d), lambda b: (0, 0)),
            pl.BlockSpec((d, d), lambda b: (0, 0)),
        ],
        out_specs=[
            pl.BlockSpec((1, d, m), lambda b: (b, 0, 0)),
            pl.BlockSpec((1, m, d), lambda b: (b, 0, 0)),
        ],
        out_shape=[jax.ShapeDtypeStruct((bn, d, m), BF16), jax.ShapeDtypeStruct((bn, m, d), BF16)],
        compiler_params=pltpu.CompilerParams(dimension_semantics=("arbitrary",)),
        name="kv_proj",
    )(mem, wkt, wv)


def _mixer_kernel(x_ref, win_ref, bg_ref, cw_ref, wco_ref, wpool_ref, ps_ref, kt_ref, v_ref, wxo_ref, wo_ref,
                  g1_ref, b1_ref, wrh_ref, wrl_ref, br_ref,
                  h1_ref, route_ref, ri_ref, cnt_ref,
                  ubuf, pbuf, carry):
    tm = TM_MIX
    d = D_MODEL
    b = pl.program_id(0)
    s = pl.program_id(1)

    @pl.when(s == 0)
    def _():
        ubuf[0:CONV_HALO, :] = jnp.zeros((CONV_HALO, d), F32)
        pbuf[0:POOL_HALO, :] = jnp.zeros((POOL_HALO, d), F32)

    @pl.when((s == 0) & (b == 0))
    def _():
        carry[...] = jnp.zeros_like(carry)

    x = x_ref[...]
    xb = x.astype(BF16)

    def proj(sec):
        return _dot(xb, win_ref[:, sec * d:(sec + 1) * d])

    def gate(i):
        return jax.nn.sigmoid(proj(5 + i) + bg_ref[i:i + 1, :])

    u = proj(1) * proj(2)
    ubuf[CONV_HALO:CONV_HALO + tm, :] = u
    v = cw_ref[CONV_K - 1:CONV_K, :] * u
    for k in range(CONV_K - 1):
        off = CONV_HALO - (CONV_K - 1) + k
        v = v + cw_ref[k:k + 1, :] * ubuf[off:off + tm, :]
    ubuf[0:CONV_HALO, :] = ubuf[tm:tm + CONV_HALO, :]
    y_conv = _dot((proj(0) * v).astype(BF16), wco_ref[...])
    acc = gate(0) * y_conv

    p = proj(3)
    pbuf[POOL_HALO:POOL_HALO + tm, :] = p
    pos = (s * tm + 1 + lax.broadcasted_iota(jnp.int32, (tm, 1), 0)).astype(F32)
    ys = []
    for gi, w in enumerate(POOL_WINDOWS):
        c0 = gi * POOL_GROUP_DIM
        c1 = c0 + POOL_GROUP_DIM
        pg = p[:, c0:c1]
        ws = pg
        for j in range(1, w):
            ws = ws + pbuf[POOL_HALO - j:POOL_HALO - j + tm, c0:c1]
        inv_cnt = 1.0 / jnp.minimum(pos, float(w))
        dg = (ws * inv_cnt - pg).astype(BF16)
        ys.append(_dot(dg, wpool_ref[gi]))
    pbuf[0:POOL_HALO, :] = pbuf[tm:tm + POOL_HALO, :]
    y_pool = jnp.concatenate(ys, axis=1) * ps_ref[...]
    acc = acc + gate(1) * y_pool

    qb = proj(4).astype(BF16)
    scale = X_HEAD_DIM ** -0.5
    os_ = []
    for h in range(X_HEADS):
        c0 = h * X_HEAD_DIM
        c1 = c0 + X_HEAD_DIM
        sc = _dot(qb[:, c0:c1], kt_ref[0, c0:c1, :]) * scale
        e = jnp.exp(sc - jnp.max(sc, axis=-1, keepdims=True))
        a = e * (1.0 / jnp.sum(e, axis=-1, keepdims=True))
        os_.append(_dot(a.astype(BF16), v_ref[0, :, c0:c1]))
    y_mem = _dot(jnp.concatenate(os_, axis=1).astype(BF16), wxo_ref[...])
    acc = acc + gate(2) * y_mem

    mix = _dot(acc.astype(BF16), wo_ref[...])
    h1 = _layer_norm(ALPHA * x + mix, g1_ref[...], b1_ref[...])
    _tok_store(h1_ref, h1, tm)

    hh = h1.astype(BF16)
    hl = (h1 - hh.astype(F32)).astype(BF16)
    logits = _dot(hh, wrh_ref[...]) + _dot(hl, wrh_ref[...]) + _dot(hh, wrl_ref[...]) + br_ref[...]
    lane = lax.broadcasted_iota(jnp.int32, (tm, LANES), 1)
    lane_f = lane.astype(F32)
    neg = jnp.float32(-jnp.inf)
    big = jnp.float32(1e9)

    def first_argmax(vals):
        m = jnp.max(vals, axis=-1, keepdims=True)
        idx = jnp.min(jnp.where(vals == m, lane_f, big), axis=-1, keepdims=True)
        return m, idx

    is_g = (lane >= N_EXPERTS) & (lane < N_EXPERTS + N_GROUPS)
    gmax, gidx = first_argmax(jnp.where(is_g, logits, neg))
    g_w = 1.0 / jnp.sum(jnp.where(is_g, jnp.exp(logits - gmax), 0.0), axis=-1, keepdims=True)
    gsel = gidx.astype(jnp.int32) - N_EXPERTS
    in_grp = (lane >> 3) == gsel
    le = jnp.where(in_grp, logits, neg)
    m1, i1 = first_argmax(le)
    m2, i2 = first_argmax(jnp.where(lane_f == i1, neg, le))
    t = jnp.exp(m2 - m1)
    den = 1.0 + t
    w1 = g_w / den
    w2 = g_w * t / den

    sel1 = lane_f == i1
    sel2 = lane_f == i2
    onehot = (sel1 | sel2).astype(BF16)
    row_i = lax.broadcasted_iota(jnp.int32, (tm, tm), 0)
    col_i = lax.broadcasted_iota(jnp.int32, (tm, tm), 1)
    before = (col_i < row_i).astype(BF16)
    prior = _dot(before, onehot) + carry[0:1, :]
    r1 = jnp.sum(jnp.where(sel1, prior, 0.0), axis=-1, keepdims=True)
    r2 = jnp.sum(jnp.where(sel2, prior, 0.0), axis=-1, keepdims=True)
    carry[...] = carry[...] + jnp.sum(onehot.astype(F32), axis=0, keepdims=True)
    cnt_ref[...] = carry[...].astype(jnp.int32)

    route = jnp.where(lane == 0, i1, 0.0)
    for k, col in enumerate((i2, r1, r2, w1, w2), start=1):
        route = jnp.where(lane == k, col, route)
    route_ref[...] = route
    ri_ref[0] = route.T[0:8, :].astype(jnp.int32)


def _mixer(x2, w_in_b, b_gate3, conv_w, wco_b, wpool_b, pool_scale, kt, v, wxo_b, wo_b, ln_g, ln_b, wr_hi, wr_lo, br,
           bn, seq):
    n, d = x2.shape
    tm = TM_MIX
    spb = seq // tm
    const2 = lambda b, s: (0, 0)
    const3 = lambda b, s: (0, 0, 0)
    one = pl.Buffered(1)
    row_map = lambda b, s: (b * spb + s, 0)
    return pl.pallas_call(
        _mixer_kernel,
        grid=(bn, spb),
        in_specs=[
            pl.BlockSpec((tm, d), row_map),
            pl.BlockSpec(w_in_b.shape, const2, pipeline_mode=one),
            pl.BlockSpec(b_gate3.shape, const2, pipeline_mode=one),
            pl.BlockSpec(conv_w.shape, const2, pipeline_mode=one),
            pl.BlockSpec(wco_b.shape, const2, pipeline_mode=one),
            pl.BlockSpec(wpool_b.shape, const3, pipeline_mode=one),
            pl.BlockSpec(pool_scale.shape, const2, pipeline_mode=one),
            pl.BlockSpec((1,) + kt.shape[1:], lambda b, s: (b, 0, 0)),
            pl.BlockSpec((1,) + v.shape[1:], lambda b, s: (b, 0, 0)),
            pl.BlockSpec(wxo_b.shape, const2, pipeline_mode=one),
            pl.BlockSpec(wo_b.shape, const2, pipeline_mode=one),
            pl.BlockSpec(ln_g.shape, const2, pipeline_mode=one),
            pl.BlockSpec(ln_b.shape, const2, pipeline_mode=one),
            pl.BlockSpec(wr_hi.shape, const2, pipeline_mode=one),
            pl.BlockSpec(wr_lo.shape, const2, pipeline_mode=one),
            pl.BlockSpec(br.shape, const2, pipeline_mode=one),
        ],
        out_specs=[
            pl.BlockSpec((tm * SUB, LANES), row_map),
            pl.BlockSpec((tm, LANES), row_map),
            pl.BlockSpec((1, 8, tm), lambda b, s: (b * spb + s, 0, 0)),
            pl.BlockSpec((8, LANES), const2),
        ],
        out_shape=[
            jax.ShapeDtypeStruct((n * SUB, LANES), F32),
            jax.ShapeDtypeStruct((n, LANES), F32),
            jax.ShapeDtypeStruct((n // tm, 8, tm), jnp.int32),
            jax.ShapeDtypeStruct((8, LANES), jnp.int32),
        ],
        scratch_shapes=[
            pltpu.VMEM((tm + CONV_HALO, d), F32),
            pltpu.VMEM((tm + POOL_HALO, d), F32),
            pltpu.VMEM((8, LANES), F32),
        ],
        compiler_params=pltpu.CompilerParams(
            dimension_semantics=("arbitrary", "arbitrary"), vmem_limit_bytes=VMEM_LIMIT),
        name="mixer",
    )(x2, w_in_b, b_gate3, conv_w, wco_b, wpool_b, pool_scale, kt, v, wxo_b, wo_b, ln_g, ln_b, wr_hi, wr_lo, br)


def _dest_kernel(ps_ref, ri_ref, dest_ref):
    dest_ref[...] = jnp.zeros(dest_ref.shape, jnp.int32)
    for t in range(dest_ref.shape[0]):
        ri = ri_ref[t]
        e = ri[0:TOP_K, :]
        start = jnp.zeros_like(e)
        for j in range(N_EXPERTS):
            start = jnp.where(e == j, ps_ref[j], start)
        dest_ref[t, 0:TOP_K, :] = start + ri[TOP_K:2 * TOP_K, :]


def _dest(pstarts, route_i):
    nt, rows, tm = route_i.shape
    blk = 8
    spec = pl.BlockSpec((blk, rows, tm), lambda i, ps: (i, 0, 0))
    return pl.pallas_call(
        _dest_kernel,
        grid_spec=pltpu.PrefetchScalarGridSpec(
            num_scalar_prefetch=1, grid=(nt // blk,), in_specs=[spec], out_specs=spec),
        out_shape=jax.ShapeDtypeStruct(route_i.shape, jnp.int32),
        compiler_params=pltpu.CompilerParams(dimension_semantics=("arbitrary",)),
        name="dest_rows",
    )(pstarts, route_i)


def _scatter_kernel(zt_ref, dest_ref, h_ref, xbuf_ref, zbuf, sem):
    i = pl.program_id(0)
    tm = TM_ROW

    def zero_copy(e):
        start = pl.multiple_of(zt_ref[e] * SUB, TM_EXP * SUB)
        return pltpu.make_async_copy(zbuf, xbuf_ref.at[pl.ds(start, TM_EXP * SUB), :], sem.at[2])

    @pl.when(i == 0)
    def _():
        zbuf[...] = jnp.zeros_like(zbuf)

        def start(e, c):
            @pl.when(zt_ref[e] >= 0)
            def _():
                zero_copy(e).start()
            return c

        def wait(e, c):
            @pl.when(zt_ref[e] >= 0)
            def _():
                zero_copy(e).wait()
            return c

        lax.fori_loop(0, 2 * N_EXPERTS, start, 0)
        lax.fori_loop(0, 2 * N_EXPERTS, wait, 0)

    def issue(r, c):
        for k in range(TOP_K):
            _tile_copy(h_ref, r, xbuf_ref, dest_ref[0, k, r], sem.at[k]).start(priority=k)
        return c

    lax.fori_loop(0, tm, issue, 0, unroll=8)
    for k in range(TOP_K):
        pltpu.make_async_copy(h_ref, xbuf_ref.at[pl.ds(0, tm * SUB), :], sem.at[k]).wait()


def _scatter(zero_tiles, dest, h1t, p_rows):
    tm = TM_ROW
    nt = dest.shape[0]
    return pl.pallas_call(
        _scatter_kernel,
        grid_spec=pltpu.PrefetchScalarGridSpec(
            num_scalar_prefetch=1,
            grid=(nt,),
            in_specs=[
                pl.BlockSpec((1, 8, tm), lambda i, zt: (i, 0, 0), memory_space=pltpu.SMEM),
                pl.BlockSpec((tm * SUB, LANES), lambda i, zt: (i, 0)),
            ],
            out_specs=pl.BlockSpec(memory_space=pl.ANY),
            scratch_shapes=[pltpu.VMEM((TM_EXP * SUB, LANES), F32), pltpu.SemaphoreType.DMA((3,))],
        ),
        out_shape=jax.ShapeDtypeStruct((p_rows * SUB, LANES), F32),
        compiler_params=pltpu.CompilerParams(dimension_semantics=("arbitrary",)),
        name="scatter_rows",
    )(zero_tiles, dest, h1t)


def _expert_kernel(t0_ref, nt_ref, tot_ref, wup_ref, wdn_ref, xbuf_ref, ybuf_ref, wup_b, wdn_b, xs, ys, sem_in, sem_out):
    e = pl.program_id(0)
    t0 = t0_ref[e]
    nt = nt_ref[e]
    total = tot_ref[0]
    rows = TM_EXP * SUB
    max_tiles = xbuf_ref.shape[0] // rows

    def hbm_tile(ref, g):
        return ref.at[pl.ds(pl.multiple_of(g * rows, rows), rows), :]

    def x_copy(g, slot):
        return pltpu.make_async_copy(hbm_tile(xbuf_ref, g), xs.at[slot], sem_in.at[slot])

    def y_copy(g, slot):
        return pltpu.make_async_copy(ys.at[slot], hbm_tile(ybuf_ref, g), sem_out.at[slot])

    @pl.when(e == 0)
    def _():
        x_copy(0, 0).start(priority=ROW_DMA_PRIORITY)

    @pl.when(nt > 0)
    def _():
        wup_b[...] = wup_ref[0].astype(BF16)
        wdn_b[...] = wdn_ref[0].astype(BF16)

    def tile(j, c):
        g = t0 + j
        slot = lax.rem(g, 2)

        @pl.when(g + 1 < total)
        def _():
            x_copy(g + 1, 1 - slot).start(priority=ROW_DMA_PRIORITY)

        x_copy(g, slot).wait()

        @pl.when(g >= 2)
        def _():
            y_copy(g - 2, slot).wait()

        xb = _tok_load(xs.at[slot], TM_EXP).astype(BF16)
        hgv = _dot(xb, wup_b[...])
        hg = hgv[:, :D_EXPERT]
        hv = hgv[:, D_EXPERT:]
        act = (hg * jax.nn.sigmoid(hg)) * hv
        _tok_store(ys.at[slot], _dot(act.astype(BF16), wdn_b[...]), TM_EXP)
        y_copy(g, slot).start(priority=ROW_DMA_PRIORITY)
        return c

    lax.fori_loop(0, nt, tile, 0)

    @pl.when(e == pl.num_programs(0) - 1)
    def _():
        @pl.when(total >= 2)
        def _():
            y_copy(total - 2, lax.rem(total, 2)).wait()

        y_copy(total - 1, lax.rem(total - 1, 2)).wait()
        ys[0] = jnp.zeros(ys.shape[1:], F32)

        def zero_tail(g, c):
            y_copy(g, 0).start()
            y_copy(g, 0).wait()
            return c

        lax.fori_loop(total, max_tiles, zero_tail, 0)


def _experts(tile_start, tile_count, n_tiles, xbuf, w_up, w_down):
    w_map = lambda e, t0, nt, tot: (e, 0, 0)
    rows = TM_EXP * SUB
    return pl.pallas_call(
        _expert_kernel,
        grid_spec=pltpu.PrefetchScalarGridSpec(
            num_scalar_prefetch=3,
            grid=(N_EXPERTS,),
            in_specs=[
                pl.BlockSpec((1,) + w_up.shape[1:], w_map),
                pl.BlockSpec((1,) + w_down.shape[1:], w_map),
                pl.BlockSpec(memory_space=pl.ANY),
            ],
            out_specs=pl.BlockSpec(memory_space=pl.ANY),
            scratch_shapes=[
                pltpu.VMEM(w_up.shape[1:], BF16), pltpu.VMEM(w_down.shape[1:], BF16),
                pltpu.VMEM((2, rows, LANES), F32), pltpu.VMEM((2, rows, LANES), F32),
                pltpu.SemaphoreType.DMA((2,)), pltpu.SemaphoreType.DMA((2,)),
            ],
        ),
        out_shape=jax.ShapeDtypeStruct(xbuf.shape, F32),
        compiler_params=pltpu.CompilerParams(dimension_semantics=("arbitrary",), vmem_limit_bytes=VMEM_LIMIT),
        name="experts",
    )(tile_start, tile_count, n_tiles, w_up, w_down, xbuf)


def _combine_kernel(dcur_ref, dnext_ref, h_ref, route_ref, ybuf_ref, g_ref, b_ref, out_ref, ybufs, sem):
    tm = TM_ROW
    i = pl.program_id(0)
    slot = lax.rem(i, 2)

    def gather(d_ref, s):
        def issue(r, c):
            for k in range(TOP_K):
                _tile_copy(ybuf_ref, d_ref[0, k, r], ybufs.at[s, k], r, sem.at[s, k]).start(priority=k)
            return c

        lax.fori_loop(0, tm, issue, 0, unroll=8)

    @pl.when(i == 0)
    def _():
        gather(dcur_ref, 0)

    @pl.when(i + 1 < pl.num_programs(0))
    def _():
        gather(dnext_ref, 1 - slot)

    for k in range(TOP_K):
        pltpu.make_async_copy(ybuf_ref.at[pl.ds(0, tm * SUB), :], ybufs.at[slot, k], sem.at[slot, k]).wait()
    route = route_ref[...]
    ffn = route[:, 4:5] * _tok_load(ybufs.at[slot, 0], tm) + route[:, 5:6] * _tok_load(ybufs.at[slot, 1], tm)
    out_ref[...] = _layer_norm(ALPHA * _tok_load(h_ref, tm) + ffn, g_ref[...], b_ref[...])


def _combine(dest, h1t, route, ybuf, ln_g, ln_b):
    tm = TM_ROW
    nt = dest.shape[0]
    n = nt * tm
    return pl.pallas_call(
        _combine_kernel,
        grid=(nt,),
        in_specs=[
            pl.BlockSpec((1, 8, tm), lambda i: (i, 0, 0), memory_space=pltpu.SMEM),
            pl.BlockSpec((1, 8, tm), lambda i: (jnp.minimum(i + 1, nt - 1), 0, 0), memory_space=pltpu.SMEM),
            pl.BlockSpec((tm * SUB, LANES), lambda i: (i, 0)),
            pl.BlockSpec((tm, LANES), lambda i: (i, 0)),
            pl.BlockSpec(memory_space=pl.ANY),
            pl.BlockSpec(ln_g.shape, lambda i: (0, 0)),
            pl.BlockSpec(ln_b.shape, lambda i: (0, 0)),
        ],
        out_specs=pl.BlockSpec((tm, D_MODEL), lambda i: (i, 0)),
        out_shape=jax.ShapeDtypeStruct((n, D_MODEL), F32),
        scratch_shapes=[pltpu.VMEM((2, TOP_K, tm * SUB, LANES), F32), pltpu.SemaphoreType.DMA((2, TOP_K))],
        compiler_params=pltpu.CompilerParams(dimension_semantics=("arbitrary",)),
        name="combine_ln2",
    )(dest, dest, h1t, route, ybuf, ln_g, ln_b)


def _layer(h, mem, w_in, b_gate, conv_w, w_conv_out, w_pool, pool_scale, w_kv, w_xo, w_o, ln1_g, ln1_b,
           w_rg, b_rg, w_re, b_re, w_up, w_down, ln2_g, ln2_b):
    bn, seq, d = h.shape
    n = bn * seq
    xw = X_HEADS * X_HEAD_DIM

    kt, v = _kv_proj(mem, w_kv[:, :xw].T.astype(BF16), w_kv[:, xw:].astype(BF16))

    pad = LANES - N_EXPERTS - N_GROUPS
    w_r = jnp.concatenate([w_re, w_rg, jnp.zeros((d, pad), F32)], axis=1)
    wr_hi = w_r.astype(BF16)
    wr_lo = (w_r - wr_hi.astype(F32)).astype(BF16)
    b_r = jnp.concatenate([b_re, b_rg, jnp.zeros((pad,), F32)])[None, :]

    h1t, route, route_i, cnt = _mixer(
        h.reshape(n, d), w_in.astype(BF16), b_gate.reshape(3, d), conv_w, w_conv_out.astype(BF16),
        w_pool.astype(BF16), pool_scale[None, :], kt, v, w_xo.astype(BF16), w_o.astype(BF16),
        ln1_g[None, :], ln1_b[None, :], wr_hi, wr_lo, b_r, bn, seq)

    counts = cnt[0, :N_EXPERTS]
    padded = (counts + TM_EXP - 1) // TM_EXP * TM_EXP
    pends = jnp.cumsum(padded)
    pstarts = pends - padded
    max_tiles = (n * TOP_K + N_EXPERTS * (TM_EXP - 1)) // TM_EXP
    p_rows = max_tiles * TM_EXP
    n_tiles = (pends[-1] // TM_EXP).astype(jnp.int32)
    tail_ids = n_tiles + jnp.arange(N_EXPERTS, dtype=jnp.int32)
    zero_tiles = jnp.concatenate([
        jnp.where(counts % TM_EXP != 0, pends - TM_EXP, -1),
        jnp.where(tail_ids < max_tiles, tail_ids * TM_EXP, -1)]).astype(jnp.int32)

    dest = _dest(pstarts.astype(jnp.int32), route_i)
    xbuf = _scatter(zero_tiles, dest, h1t, p_rows)
    ybuf = _experts((pstarts // TM_EXP).astype(jnp.int32), (padded // TM_EXP).astype(jnp.int32), n_tiles.reshape(1),
                    xbuf, w_up, w_down)
    out = _combine(dest, h1t, route, ybuf, ln2_g[None, :], ln2_b[None, :])
    return out.reshape(bn, seq, d)


def kernel(x, mem, w_in, b_gate, conv_w, w_conv_out, w_pool, pool_scale, w_kv, w_xo, w_o, ln1_g, ln1_b,
           w_router_group, b_router_group, w_router_expert, b_router_expert, w_up, w_down, ln2_g, ln2_b):
    h = x
    for l in range(DEPTH):
        h = _layer(h, mem, w_in[l], b_gate[l], conv_w[l], w_conv_out[l], w_pool[l], pool_scale[l], w_kv[l],
                   w_xo[l], w_o[l], ln1_g[l], ln1_b[l], w_router_group[l], b_router_group[l],
                   w_router_expert[l], b_router_expert[l], w_up[l], w_down[l], ln2_g[l], ln2_b[l])
    return h
```

```python
import functools

import jax
import jax.numpy as jnp
from jax import lax
from jax.experimental import pallas as pl
from jax.experimental.pallas import tpu as pltpu

D_MODEL = 1024
CONV_K = 3
POOL_WINDOWS = (2, 4, 8, 16)
POOL_GROUP_DIM = 256
X_HEADS = 4
X_HEAD_DIM = 256
N_GROUPS = 8
EXPERTS_PER_GROUP = 8
N_EXPERTS = 64
TOP_K = 2
D_EXPERT = 512
DEPTH = 1
ALPHA = (2.0 * DEPTH) ** 0.25
LN_EPS = 1e-5

LANES = 128
SUB = 8
assert D_MODEL == SUB * LANES
POOL_HALO = 16
CONV_HALO = 8
TM_MIX = 512
TM_EXP = 256
TM_ROW = TM_MIX
VMEM_LIMIT = 58 * 1024 * 1024
ROW_DMA_PRIORITY = 1

F32 = jnp.float32
BF16 = jnp.bfloat16


def _dot(a, b):
    return jnp.dot(a, b, preferred_element_type=F32)


def _tok_load(ref, tm):
    return jnp.concatenate([ref[pl.ds(c, tm, stride=SUB), :] for c in range(SUB)], axis=1)


def _tok_store(ref, val, tm):
    for c in range(SUB):
        ref[pl.ds(c, tm, stride=SUB), :] = val[:, c * LANES:(c + 1) * LANES]


def _tile_copy(src, src_tok, dst, dst_tok, sem):
    s = pl.multiple_of(src_tok * SUB, SUB)
    t = pl.multiple_of(dst_tok * SUB, SUB)
    return pltpu.make_async_copy(src.at[pl.ds(s, SUB), :], dst.at[pl.ds(t, SUB), :], sem)


def _layer_norm(h, g, b):
    mu = jnp.mean(h, axis=-1, keepdims=True)
    c = h - mu
    var = jnp.mean(c * c, axis=-1, keepdims=True)
    return c * lax.rsqrt(var + LN_EPS) * g + b


def _kv_kernel(mem_ref, wkt_ref, wv_ref, kt_ref, v_ref):
    mb = mem_ref[0].astype(BF16)
    kt = lax.dot_general(wkt_ref[...], mb, (((1,), (1,)), ((), ())), preferred_element_type=F32)
    kt_ref[0] = kt.astype(BF16)
    v_ref[0] = _dot(mb, wv_ref[...]).astype(BF16)


def _kv_proj(mem, wkt, wv):
    bn, m, d = mem.shape
    return pl.pallas_call(
        _kv_kernel,
        grid=(bn,),
        in_specs=[
            pl.BlockSpec((1, m, d), lambda b: (b, 0, 0)),
            pl.BlockSpec((d, d), lambda b: (0, 0)),
            pl.BlockSpec((d, d), lambda b: (0, 0)),
        ],
        out_specs=[
            pl.BlockSpec((1, d, m), lambda b: (b, 0, 0)),
            pl.BlockSpec((1, m, d), lambda b: (b, 0, 0)),
        ],
        out_shape=[jax.ShapeDtypeStruct((bn, d, m), BF16), jax.ShapeDtypeStruct((bn, m, d), BF16)],
        compiler_params=pltpu.CompilerParams(dimension_semantics=("arbitrary",)),
        name="kv_proj",
    )(mem, wkt, wv)


def _mixer_kernel(x_ref, win_ref, bg_ref, cw_ref, wco_ref, wpool_ref, ps_ref, kt_ref, v_ref, wxo_ref, wo_ref,
                  g1_ref, b1_ref, wr_ref, br_ref,
                  h1_ref, route_ref, ri_ref, cnt_ref,
                  ubuf, pbuf, carry):
    tm = TM_MIX
    d = D_MODEL
    b = pl.program_id(0)
    s = pl.program_id(1)

    @pl.when(s == 0)
    def _():
        ubuf[0:CONV_HALO, :] = jnp.zeros((CONV_HALO, d), F32)
        pbuf[0:POOL_HALO, :] = jnp.zeros((POOL_HALO, d), F32)

    @pl.when((s == 0) & (b == 0))
    def _():
        carry[...] = jnp.zeros_like(carry)

    x = x_ref[...]
    xb = x.astype(BF16)

    def proj(sec):
        return _dot(xb, win_ref[:, sec * d:(sec + 1) * d])

    def gate(i):
        return jax.nn.sigmoid(proj(5 + i) + bg_ref[i:i + 1, :])

    u = proj(1) * proj(2)
    ubuf[CONV_HALO:CONV_HALO + tm, :] = u
    v = cw_ref[CONV_K - 1:CONV_K, :] * u
    for k in range(CONV_K - 1):
        off = CONV_HALO - (CONV_K - 1) + k
        v = v + cw_ref[k:k + 1, :] * ubuf[off:off + tm, :]
    ubuf[0:CONV_HALO, :] = ubuf[tm:tm + CONV_HALO, :]
    y_conv = _dot((proj(0) * v).astype(BF16), wco_ref[...])
    acc = gate(0) * y_conv

    p = proj(3)
    pbuf[POOL_HALO:POOL_HALO + tm, :] = p
    pos = (s * tm + 1 + lax.broadcasted_iota(jnp.int32, (tm, 1), 0)).astype(F32)
    ys = []
    for gi, w in enumerate(POOL_WINDOWS):
        c0 = gi * POOL_GROUP_DIM
        c1 = c0 + POOL_GROUP_DIM
        pg = p[:, c0:c1]
        ws = pg
        for j in range(1, w):
            ws = ws + pbuf[POOL_HALO - j:POOL_HALO - j + tm, c0:c1]
        inv_cnt = 1.0 / jnp.minimum(pos, float(w))
        dg = (ws * inv_cnt - pg).astype(BF16)
        ys.append(_dot(dg, wpool_ref[gi]))
    pbuf[0:POOL_HALO, :] = pbuf[tm:tm + POOL_HALO, :]
    y_pool = jnp.concatenate(ys, axis=1) * ps_ref[...]
    acc = acc + gate(1) * y_pool

    qb = proj(4).astype(BF16)
    scale = X_HEAD_DIM ** -0.5
    os_ = []
    for h in range(X_HEADS):
        c0 = h * X_HEAD_DIM
        c1 = c0 + X_HEAD_DIM
        sc = _dot(qb[:, c0:c1], kt_ref[0, c0:c1, :]) * scale
        e = jnp.exp(sc - jnp.max(sc, axis=-1, keepdims=True))
        a = e * (1.0 / jnp.sum(e, axis=-1, keepdims=True))
        os_.append(_dot(a.astype(BF16), v_ref[0, :, c0:c1]))
    y_mem = _dot(jnp.concatenate(os_, axis=1).astype(BF16), wxo_ref[...])
    acc = acc + gate(2) * y_mem

    mix = _dot(acc.astype(BF16), wo_ref[...])
    h1 = _layer_norm(ALPHA * x + mix, g1_ref[...], b1_ref[...])
    _tok_store(h1_ref, h1, tm)

    hh = h1.astype(BF16)
    hl = (h1 - hh.astype(F32)).astype(BF16)
    l2 = _dot(hh, wr_ref[...]) + _dot(hl, wr_ref[...])
    logits = l2[:, :LANES] + l2[:, LANES:] + br_ref[...]
    lane = lax.broadcasted_iota(jnp.int32, (tm, LANES), 1)
    lane_f = lane.astype(F32)
    neg = jnp.float32(-jnp.inf)
    big = jnp.float32(1e9)

    def first_argmax(vals):
        m = jnp.max(vals, axis=-1, keepdims=True)
        idx = jnp.min(jnp.where(vals == m, lane_f, big), axis=-1, keepdims=True)
        return m, idx

    is_g = (lane >= N_EXPERTS) & (lane < N_EXPERTS + N_GROUPS)
    gmax, gidx = first_argmax(jnp.where(is_g, logits, neg))
    g_w = 1.0 / jnp.sum(jnp.where(is_g, jnp.exp(logits - gmax), 0.0), axis=-1, keepdims=True)
    gsel = gidx.astype(jnp.int32) - N_EXPERTS
    in_grp = (lane >> 3) == gsel
    le = jnp.where(in_grp, logits, neg)
    m1, i1 = first_argmax(le)
    m2, i2 = first_argmax(jnp.where(lane_f == i1, neg, le))
    t = jnp.exp(m2 - m1)
    den = 1.0 + t
    w1 = g_w / den
    w2 = g_w * t / den

    sel1 = lane_f == i1
    sel2 = lane_f == i2
    onehot = (sel1 | sel2).astype(BF16)
    row_i = lax.broadcasted_iota(jnp.int32, (tm, tm), 0)
    col_i = lax.broadcasted_iota(jnp.int32, (tm, tm), 1)
    before = (col_i < row_i).astype(BF16)
    prior = _dot(before, onehot) + carry[0:1, :]
    r1 = jnp.sum(jnp.where(sel1, prior, 0.0), axis=-1, keepdims=True)
    r2 = jnp.sum(jnp.where(sel2, prior, 0.0), axis=-1, keepdims=True)
    carry[...] = carry[...] + jnp.sum(onehot.astype(F32), axis=0, keepdims=True)
    cnt_ref[...] = carry[...].astype(jnp.int32)

    route = jnp.where(lane == 0, i1, 0.0)
    for k, col in enumerate((i2, r1, r2, w1, w2), start=1):
        route = jnp.where(lane == k, col, route)
    route_ref[...] = route
    ri_ref[0] = route.T[0:8, :].astype(jnp.int32)


def _mixer(x2, w_in_b, b_gate3, conv_w, wco_b, wpool_b, pool_scale, kt, v, wxo_b, wo_b, ln_g, ln_b, wr, br,
           bn, seq):
    n, d = x2.shape
    tm = TM_MIX
    spb = seq // tm
    const2 = lambda b, s: (0, 0)
    const3 = lambda b, s: (0, 0, 0)
    one = pl.Buffered(1)
    row_map = lambda b, s: (b * spb + s, 0)
    return pl.pallas_call(
        _mixer_kernel,
        grid=(bn, spb),
        in_specs=[
            pl.BlockSpec((tm, d), row_map),
            pl.BlockSpec(w_in_b.shape, const2, pipeline_mode=one),
            pl.BlockSpec(b_gate3.shape, const2, pipeline_mode=one),
            pl.BlockSpec(conv_w.shape, const2, pipeline_mode=one),
            pl.BlockSpec(wco_b.shape, const2, pipeline_mode=one),
            pl.BlockSpec(wpool_b.shape, const3, pipeline_mode=one),
            pl.BlockSpec(pool_scale.shape, const2, pipeline_mode=one),
            pl.BlockSpec((1,) + kt.shape[1:], lambda b, s: (b, 0, 0)),
            pl.BlockSpec((1,) + v.shape[1:], lambda b, s: (b, 0, 0)),
            pl.BlockSpec(wxo_b.shape, const2, pipeline_mode=one),
            pl.BlockSpec(wo_b.shape, const2, pipeline_mode=one),
            pl.BlockSpec(ln_g.shape, const2, pipeline_mode=one),
            pl.BlockSpec(ln_b.shape, const2, pipeline_mode=one),
            pl.BlockSpec(wr.shape, const2, pipeline_mode=one),
            pl.BlockSpec(br.shape, const2, pipeline_mode=one),
        ],
        out_specs=[
            pl.BlockSpec((tm * SUB, LANES), row_map),
            pl.BlockSpec((tm, LANES), row_map),
            pl.BlockSpec((1, 8, tm), lambda b, s: (b * spb + s, 0, 0)),
            pl.BlockSpec((8, LANES), const2),
        ],
        out_shape=[
            jax.ShapeDtypeStruct((n * SUB, LANES), F32),
            jax.ShapeDtypeStruct((n, LANES), F32),
            jax.ShapeDtypeStruct((n // tm, 8, tm), jnp.int32),
            jax.ShapeDtypeStruct((8, LANES), jnp.int32),
        ],
        scratch_shapes=[
            pltpu.VMEM((tm + CONV_HALO, d), F32),
            pltpu.VMEM((tm + POOL_HALO, d), F32),
            pltpu.VMEM((8, LANES), F32),
        ],
        compiler_params=pltpu.CompilerParams(
            dimension_semantics=("arbitrary", "arbitrary"), vmem_limit_bytes=VMEM_LIMIT),
        name="mixer",
    )(x2, w_in_b, b_gate3, conv_w, wco_b, wpool_b, pool_scale, kt, v, wxo_b, wo_b, ln_g, ln_b, wr, br)


def _dest_kernel(ps_ref, ri_ref, dest_ref):
    dest_ref[...] = jnp.zeros(dest_ref.shape, jnp.int32)
    for t in range(dest_ref.shape[0]):
        ri = ri_ref[t]
        e = ri[0:TOP_K, :]
        start = jnp.zeros_like(e)
        for j in range(N_EXPERTS):
            start = jnp.where(e == j, ps_ref[j], start)
        dest_ref[t, 0:TOP_K, :] = start + ri[TOP_K:2 * TOP_K, :]


def _dest(pstarts, route_i):
    nt, rows, tm = route_i.shape
    blk = 8
    spec = pl.BlockSpec((blk, rows, tm), lambda i, ps: (i, 0, 0))
    return pl.pallas_call(
        _dest_kernel,
        grid_spec=pltpu.PrefetchScalarGridSpec(
            num_scalar_prefetch=1, grid=(nt // blk,), in_specs=[spec], out_specs=spec),
        out_shape=jax.ShapeDtypeStruct(route_i.shape, jnp.int32),
        compiler_params=pltpu.CompilerParams(dimension_semantics=("arbitrary",)),
        name="dest_rows",
    )(pstarts, route_i)


def _scatter_kernel(zt_ref, dest_ref, h_ref, xbuf_ref, zbuf, sem):
    i = pl.program_id(0)
    tm = TM_ROW

    def zero_copy(e):
        start = pl.multiple_of(zt_ref[e] * SUB, TM_EXP * SUB)
        return pltpu.make_async_copy(zbuf, xbuf_ref.at[pl.ds(start, TM_EXP * SUB), :], sem.at[2])

    @pl.when(i == 0)
    def _():
        zbuf[...] = jnp.zeros_like(zbuf)

        def start(e, c):
            @pl.when(zt_ref[e] >= 0)
            def _():
                zero_copy(e).start()
            return c

        def wait(e, c):
            @pl.when(zt_ref[e] >= 0)
            def _():
                zero_copy(e).wait()
            return c

        lax.fori_loop(0, 2 * N_EXPERTS, start, 0)
        lax.fori_loop(0, 2 * N_EXPERTS, wait, 0)

    def issue(r, c):
        for k in range(TOP_K):
            _tile_copy(h_ref, i * tm + r, xbuf_ref, dest_ref[k * tm + r], sem.at[k]).start(priority=k)
        return c

    lax.fori_loop(0, tm, issue, 0, unroll=8)

    def drain():
        for k in range(TOP_K):
            pltpu.make_async_copy(
                h_ref.at[pl.ds(0, tm * SUB), :], xbuf_ref.at[pl.ds(0, tm * SUB), :], sem.at[k]).wait()

    @pl.when(i > 0)
    def _():
        drain()

    @pl.when(i == pl.num_programs(0) - 1)
    def _():
        drain()


def _scatter(zero_tiles, dest_flat, h1t, p_rows):
    tm = TM_ROW
    nt = dest_flat.shape[0] // (TOP_K * tm)
    return pl.pallas_call(
        _scatter_kernel,
        grid_spec=pltpu.PrefetchScalarGridSpec(
            num_scalar_prefetch=1,
            grid=(nt,),
            in_specs=[
                pl.BlockSpec((TOP_K * tm,), lambda i, zt: (i,), memory_space=pltpu.SMEM),
                pl.BlockSpec(memory_space=pl.ANY),
            ],
            out_specs=pl.BlockSpec(memory_space=pl.ANY),
            scratch_shapes=[pltpu.VMEM((TM_EXP * SUB, LANES), F32), pltpu.SemaphoreType.DMA((3,))],
        ),
        out_shape=jax.ShapeDtypeStruct((p_rows * SUB, LANES), F32),
        compiler_params=pltpu.CompilerParams(dimension_semantics=("arbitrary",)),
        name="scatter_rows",
    )(zero_tiles, dest_flat, h1t)


def _expert_kernel(t0_ref, nt_ref, tot_ref, wup_ref, wdn_ref, xbuf_ref, ybuf_ref, wup_b, wdn_b, xs, ys, sem_in, sem_out):
    e = pl.program_id(0)
    t0 = t0_ref[e]
    nt = nt_ref[e]
    total = tot_ref[0]
    rows = TM_EXP * SUB
    max_tiles = xbuf_ref.shape[0] // rows

    def hbm_tile(ref, g):
        return ref.at[pl.ds(pl.multiple_of(g * rows, rows), rows), :]

    def x_copy(g, slot):
        return pltpu.make_async_copy(hbm_tile(xbuf_ref, g), xs.at[slot], sem_in.at[slot])

    def y_copy(g, slot):
        return pltpu.make_async_copy(ys.at[slot], hbm_tile(ybuf_ref, g), sem_out.at[slot])

    @pl.when(e == 0)
    def _():
        x_copy(0, 0).start(priority=ROW_DMA_PRIORITY)

    @pl.when(nt > 0)
    def _():
        wup_b[...] = wup_ref[0].astype(BF16)
        wdn_b[...] = wdn_ref[0].astype(BF16)

    def tile(j, c):
        g = t0 + j
        slot = lax.rem(g, 2)

        @pl.when(g + 1 < total)
        def _():
            x_copy(g + 1, 1 - slot).start(priority=ROW_DMA_PRIORITY)

        x_copy(g, slot).wait()

        @pl.when(g >= 2)
        def _():
            y_copy(g - 2, slot).wait()

        xb = _tok_load(xs.at[slot], TM_EXP).astype(BF16)
        hgv = _dot(xb, wup_b[...])
        hg = hgv[:, :D_EXPERT]
        hv = hgv[:, D_EXPERT:]
        act = (hg * jax.nn.sigmoid(hg)) * hv
        _tok_store(ys.at[slot], _dot(act.astype(BF16), wdn_b[...]), TM_EXP)
        y_copy(g, slot).start(priority=ROW_DMA_PRIORITY)
        return c

    lax.fori_loop(0, nt, tile, 0)

    @pl.when(e == pl.num_programs(0) - 1)
    def _():
        @pl.when(total >= 2)
        def _():
            y_copy(total - 2, lax.rem(total, 2)).wait()

        y_copy(total - 1, lax.rem(total - 1, 2)).wait()
        ys[0] = jnp.zeros(ys.shape[1:], F32)

        def zero_tail(g, c):
            y_copy(g, 0).start()
            y_copy(g, 0).wait()
            return c

        lax.fori_loop(total, max_tiles, zero_tail, 0)


def _experts(tile_start, tile_count, n_tiles, xbuf, w_up, w_down):
    w_map = lambda e, t0, nt, tot: (e, 0, 0)
    rows = TM_EXP * SUB
    return pl.pallas_call(
        _expert_kernel,
        grid_spec=pltpu.PrefetchScalarGridSpec(
            num_scalar_prefetch=3,
            grid=(N_EXPERTS,),
            in_specs=[
                pl.BlockSpec((1,) + w_up.shape[1:], w_map),
                pl.BlockSpec((1,) + w_down.shape[1:], w_map),
                pl.BlockSpec(memory_space=pl.ANY),
            ],
            out_specs=pl.BlockSpec(memory_space=pl.ANY),
            scratch_shapes=[
                pltpu.VMEM(w_up.shape[1:], BF16), pltpu.VMEM(w_down.shape[1:], BF16),
                pltpu.VMEM((2, rows, LANES), F32), pltpu.VMEM((2, rows, LANES), F32),
                pltpu.SemaphoreType.DMA((2,)), pltpu.SemaphoreType.DMA((2,)),
            ],
        ),
        out_shape=jax.ShapeDtypeStruct(xbuf.shape, F32),
        compiler_params=pltpu.CompilerParams(dimension_semantics=("arbitrary",), vmem_limit_bytes=VMEM_LIMIT),
        name="experts",
    )(tile_start, tile_count, n_tiles, w_up, w_down, xbuf)


def _combine_kernel(dcur_ref, dnext_ref, h_ref, route_ref, ybuf_ref, g_ref, b_ref, out_ref, ybufs, sem):
    tm = TM_ROW
    i = pl.program_id(0)
    slot = lax.rem(i, 2)

    def gather(d_ref, s):
        def issue(r, c):
            for k in range(TOP_K):
                _tile_copy(ybuf_ref, d_ref[k * tm + r], ybufs.at[s, k], r, sem.at[s, k]).start(priority=k)
            return c

        lax.fori_loop(0, tm, issue, 0, unroll=8)

    @pl.when(i == 0)
    def _():
        gather(dcur_ref, 0)

    @pl.when(i + 1 < pl.num_programs(0))
    def _():
        gather(dnext_ref, 1 - slot)

    for k in range(TOP_K):
        pltpu.make_async_copy(ybuf_ref.at[pl.ds(0, tm * SUB), :], ybufs.at[slot, k], sem.at[slot, k]).wait()
    route = route_ref[...]
    ffn = route[:, 4:5] * _tok_load(ybufs.at[slot, 0], tm) + route[:, 5:6] * _tok_load(ybufs.at[slot, 1], tm)
    out_ref[...] = _layer_norm(ALPHA * _tok_load(h_ref, tm) + ffn, g_ref[...], b_ref[...])


def _combine(dest_flat, h1t, route, ybuf, ln_g, ln_b):
    tm = TM_ROW
    nt = dest_flat.shape[0] // (TOP_K * tm)
    n = nt * tm
    return pl.pallas_call(
        _combine_kernel,
        grid=(nt,),
        in_specs=[
            pl.BlockSpec((TOP_K * tm,), lambda i: (i,), memory_space=pltpu.SMEM),
            pl.BlockSpec((TOP_K * tm,), lambda i: (jnp.minimum(i + 1, nt - 1),), memory_space=pltpu.SMEM),
            pl.BlockSpec((tm * SUB, LANES), lambda i: (i, 0)),
            pl.BlockSpec((tm, LANES), lambda i: (i, 0)),
            pl.BlockSpec(memory_space=pl.ANY),
            pl.BlockSpec(ln_g.shape, lambda i: (0, 0)),
            pl.BlockSpec(ln_b.shape, lambda i: (0, 0)),
        ],
        out_specs=pl.BlockSpec((tm, D_MODEL), lambda i: (i, 0)),
        out_shape=jax.ShapeDtypeStruct((n, D_MODEL), F32),
        scratch_shapes=[pltpu.VMEM((2, TOP_K, tm * SUB, LANES), F32), pltpu.SemaphoreType.DMA((2, TOP_K))],
        compiler_params=pltpu.CompilerParams(dimension_semantics=("arbitrary",)),
        name="combine_ln2",
    )(dest_flat, dest_flat, h1t, route, ybuf, ln_g, ln_b)


def _layer(h, mem, w_in, b_gate, conv_w, w_conv_out, w_pool, pool_scale, w_kv, w_xo, w_o, ln1_g, ln1_b,
           w_rg, b_rg, w_re, b_re, w_up, w_down, ln2_g, ln2_b):
    bn, seq, d = h.shape
    n = bn * seq
    xw = X_HEADS * X_HEAD_DIM

    kt, v = _kv_proj(mem, w_kv[:, :xw].T.astype(BF16), w_kv[:, xw:].astype(BF16))

    pad = LANES - N_EXPERTS - N_GROUPS
    w_r = jnp.concatenate([w_re, w_rg, jnp.zeros((d, pad), F32)], axis=1)
    wr_hi = w_r.astype(BF16)
    wr_lo = (w_r - wr_hi.astype(F32)).astype(BF16)
    b_r = jnp.concatenate([b_re, b_rg, jnp.zeros((pad,), F32)])[None, :]

    h1t, route, route_i, cnt = _mixer(
        h.reshape(n, d), w_in.astype(BF16), b_gate.reshape(3, d), conv_w, w_conv_out.astype(BF16),
        w_pool.astype(BF16), pool_scale[None, :], kt, v, w_xo.astype(BF16), w_o.astype(BF16),
        ln1_g[None, :], ln1_b[None, :], jnp.concatenate([wr_hi, wr_lo], axis=1), b_r, bn, seq)

    counts = cnt[0, :N_EXPERTS]
    padded = (counts + TM_EXP - 1) // TM_EXP * TM_EXP
    pends = jnp.cumsum(padded)
    pstarts = pends - padded
    max_tiles = (n * TOP_K + N_EXPERTS * (TM_EXP - 1)) // TM_EXP
    p_rows = max_tiles * TM_EXP
    n_tiles = (pends[-1] // TM_EXP).astype(jnp.int32)
    tail_ids = n_tiles + jnp.arange(N_EXPERTS, dtype=jnp.int32)
    zero_tiles = jnp.concatenate([
        jnp.where(counts % TM_EXP != 0, pends - TM_EXP, -1),
        jnp.where(tail_ids < max_tiles, tail_ids * TM_EXP, -1)]).astype(jnp.int32)

    dest = _dest(pstarts.astype(jnp.int32), route_i)[:, :TOP_K, :].reshape(-1)
    xbuf = _scatter(zero_tiles, dest, h1t, p_rows)
    ybuf = _experts((pstarts // TM_EXP).astype(jnp.int32), (padded // TM_EXP).astype(jnp.int32), n_tiles.reshape(1),
                    xbuf, w_up, w_down)
    out = _combine(dest, h1t, route, ybuf, ln2_g[None, :], ln2_b[None, :])
    return out.reshape(bn, seq, d)


def kernel(x, mem, w_in, b_gate, conv_w, w_conv_out, w_pool, pool_scale, w_kv, w_xo, w_o, ln1_g, ln1_b,
           w_router_group, b_router_group, w_router_expert, b_router_expert, w_up, w_down, ln2_g, ln2_b):
    h = x
    for l in range(DEPTH):
        h = _layer(h, mem, w_in[l], b_gate[l], conv_w[l], w_conv_out[l], w_pool[l], pool_scale[l], w_kv[l],
                   w_xo[l], w_o[l], ln1_g[l], ln1_b[l], w_router_group[l], b_router_group[l],
                   w_router_expert[l], b_router_expert[l], w_up[l], w_down[l], ln2_g[l], ln2_b[l])
    return h
```

```python
import functools

import jax
import jax.numpy as jnp
from jax import lax
from jax.experimental import pallas as pl
from jax.experimental.pallas import tpu as pltpu

D_MODEL = 1024
CONV_K = 3
POOL_WINDOWS = (2, 4, 8, 16)
POOL_GROUP_DIM = 256
X_HEADS = 4
X_HEAD_DIM = 256
N_GROUPS = 8
EXPERTS_PER_GROUP = 8
N_EXPERTS = 64
TOP_K = 2
D_EXPERT = 512
DEPTH = 1
ALPHA = (2.0 * DEPTH) ** 0.25
LN_EPS = 1e-5

LANES = 128
SUB = 8
assert D_MODEL == SUB * LANES
POOL_HALO = 16
CONV_HALO = 8
TM_MIX = 512
TM_EXP = 256
TM_ROW = TM_MIX
VMEM_LIMIT = 58 * 1024 * 1024
ROW_DMA_PRIORITY = 1

F32 = jnp.float32
BF16 = jnp.bfloat16


def _dot(a, b):
    return jnp.dot(a, b, preferred_element_type=F32)


def _tok_load(ref, tm):
    return jnp.concatenate([ref[pl.ds(c, tm, stride=SUB), :] for c in range(SUB)], axis=1)


def _tok_store(ref, val, tm):
    for c in range(SUB):
        ref[pl.ds(c, tm, stride=SUB), :] = val[:, c * LANES:(c + 1) * LANES]


def _tile_copy(src, src_tok, dst, dst_tok, sem):
    s = pl.multiple_of(src_tok * SUB, SUB)
    t = pl.multiple_of(dst_tok * SUB, SUB)
    return pltpu.make_async_copy(src.at[pl.ds(s, SUB), :], dst.at[pl.ds(t, SUB), :], sem)


def _layer_norm(h, g, b):
    mu = jnp.mean(h, axis=-1, keepdims=True)
    c = h - mu
    var = jnp.mean(c * c, axis=-1, keepdims=True)
    return c * lax.rsqrt(var + LN_EPS) * g + b


def _kv_kernel(mem_ref, wkt_ref, wv_ref, kt_ref, v_ref):
    mb = mem_ref[0].astype(BF16)
    kt = lax.dot_general(wkt_ref[...], mb, (((1,), (1,)), ((), ())), preferred_element_type=F32)
    kt_ref[0] = kt.astype(BF16)
    v_ref[0] = _dot(mb, wv_ref[...]).astype(BF16)


def _kv_proj(mem, wkt, wv):
    bn, m, d = mem.shape
    return pl.pallas_call(
        _kv_kernel,
        grid=(bn,),
        in_specs=[
            pl.BlockSpec((1, m, d), lambda b: (b, 0, 0)),
            pl.BlockSpec((d, d), lambda b: (0, 0)),
            pl.BlockSpec((d, d), lambda b: (0, 0)),
        ],
        out_specs=[
            pl.BlockSpec((1, d, m), lambda b: (b, 0, 0)),
            pl.BlockSpec((1, m, d), lambda b: (b, 0, 0)),
        ],
        out_shape=[jax.ShapeDtypeStruct((bn, d, m), BF16), jax.ShapeDtypeStruct((bn, m, d), BF16)],
        compiler_params=pltpu.CompilerParams(dimension_semantics=("arbitrary",)),
        name="kv_proj",
    )(mem, wkt, wv)


def _mixer_kernel(x_ref, win_ref, bg_ref, cw_ref, wco_ref, wpool_ref, ps_ref, kt_ref, v_ref, wxo_ref, wo_ref,
                  g1_ref, b1_ref, wr_ref, br_ref,
                  h1_ref, route_ref, ri_ref, cnt_ref,
                  ubuf, pbuf, carry):
    tm = TM_MIX
    d = D_MODEL
    b = pl.program_id(0)
    s = pl.program_id(1)

    @pl.when(s == 0)
    def _():
        ubuf[0:CONV_HALO, :] = jnp.zeros((CONV_HALO, d), F32)
        pbuf[0:POOL_HALO, :] = jnp.zeros((POOL_HALO, d), F32)

    @pl.when((s == 0) & (b == 0))
    def _():
        carry[...] = jnp.zeros_like(carry)

    x = x_ref[...]
    xb = x.astype(BF16)

    def proj(sec):
        return _dot(xb, win_ref[:, sec * d:(sec + 1) * d])

    def gate(i):
        return jax.nn.sigmoid(proj(5 + i) + bg_ref[i:i + 1, :])

    u = proj(1) * proj(2)
    ubuf[CONV_HALO:CONV_HALO + tm, :] = u
    v = cw_ref[CONV_K - 1:CONV_K, :] * u
    for k in range(CONV_K - 1):
        off = CONV_HALO - (CONV_K - 1) + k
        v = v + cw_ref[k:k + 1, :] * ubuf[off:off + tm, :]
    ubuf[0:CONV_HALO, :] = ubuf[tm:tm + CONV_HALO, :]
    y_conv = _dot((proj(0) * v).astype(BF16), wco_ref[...])
    acc = gate(0) * y_conv

    p = proj(3)
    pbuf[POOL_HALO:POOL_HALO + tm, :] = p
    pos = (s * tm + 1 + lax.broadcasted_iota(jnp.int32, (tm, 1), 0)).astype(F32)
    ys = []
    for gi, w in enumerate(POOL_WINDOWS):
        c0 = gi * POOL_GROUP_DIM
        c1 = c0 + POOL_GROUP_DIM
        pg = p[:, c0:c1]
        ws = pg
        for j in range(1, w):
            ws = ws + pbuf[POOL_HALO - j:POOL_HALO - j + tm, c0:c1]
        inv_cnt = 1.0 / jnp.minimum(pos, float(w))
        dg = (ws * inv_cnt - pg).astype(BF16)
        ys.append(_dot(dg, wpool_ref[gi]))
    pbuf[0:POOL_HALO, :] = pbuf[tm:tm + POOL_HALO, :]
    y_pool = jnp.concatenate(ys, axis=1) * ps_ref[...]
    acc = acc + gate(1) * y_pool

    qb = proj(4).astype(BF16)
    scale = X_HEAD_DIM ** -0.5
    os_ = []
    for h in range(X_HEADS):
        c0 = h * X_HEAD_DIM
        c1 = c0 + X_HEAD_DIM
        sc = _dot(qb[:, c0:c1], kt_ref[0, c0:c1, :]) * scale
        e = jnp.exp(sc - jnp.max(sc, axis=-1, keepdims=True))
        a = e * (1.0 / jnp.sum(e, axis=-1, keepdims=True))
        os_.append(_dot(a.astype(BF16), v_ref[0, :, c0:c1]))
    y_mem = _dot(jnp.concatenate(os_, axis=1).astype(BF16), wxo_ref[...])
    acc = acc + gate(2) * y_mem

    mix = _dot(acc.astype(BF16), wo_ref[...])
    h1 = _layer_norm(ALPHA * x + mix, g1_ref[...], b1_ref[...])
    _tok_store(h1_ref, h1, tm)

    hh = h1.astype(BF16)
    hl = (h1 - hh.astype(F32)).astype(BF16)
    wrh = wr_ref[:, :LANES]
    logits = _dot(hh, wrh) + _dot(hl, wrh) + _dot(hh, wr_ref[:, LANES:]) + br_ref[...]
    lane = lax.broadcasted_iota(jnp.int32, (tm, LANES), 1)
    lane_f = lane.astype(F32)
    neg = jnp.float32(-jnp.inf)
    big = jnp.float32(1e9)

    def first_argmax(vals):
        m = jnp.max(vals, axis=-1, keepdims=True)
        idx = jnp.min(jnp.where(vals == m, lane_f, big), axis=-1, keepdims=True)
        return m, idx

    is_g = (lane >= N_EXPERTS) & (lane < N_EXPERTS + N_GROUPS)
    gmax, gidx = first_argmax(jnp.where(is_g, logits, neg))
    g_w = 1.0 / jnp.sum(jnp.where(is_g, jnp.exp(logits - gmax), 0.0), axis=-1, keepdims=True)
    gsel = gidx.astype(jnp.int32) - N_EXPERTS
    in_grp = (lane >> 3) == gsel
    le = jnp.where(in_grp, logits, neg)
    m1, i1 = first_argmax(le)
    m2, i2 = first_argmax(jnp.where(lane_f == i1, neg, le))
    t = jnp.exp(m2 - m1)
    den = 1.0 + t
    w1 = g_w / den
    w2 = g_w * t / den

    sel1 = lane_f == i1
    sel2 = lane_f == i2
    onehot = (sel1 | sel2).astype(BF16)
    row_i = lax.broadcasted_iota(jnp.int32, (tm, tm), 0)
    col_i = lax.broadcasted_iota(jnp.int32, (tm, tm), 1)
    before = (col_i < row_i).astype(BF16)
    prior = _dot(before, onehot) + carry[0:1, :]
    r1 = jnp.sum(jnp.where(sel1, prior, 0.0), axis=-1, keepdims=True)
    r2 = jnp.sum(jnp.where(sel2, prior, 0.0), axis=-1, keepdims=True)
    carry[...] = carry[...] + jnp.sum(onehot.astype(F32), axis=0, keepdims=True)
    cnt_ref[...] = carry[...].astype(jnp.int32)

    route = jnp.where(lane == 0, i1, 0.0)
    for k, col in enumerate((i2, r1, r2, w1, w2), start=1):
        route = jnp.where(lane == k, col, route)
    route_ref[...] = route
    ri_ref[0] = route.T[0:8, :].astype(jnp.int32)


def _mixer(x2, w_in_b, b_gate3, conv_w, wco_b, wpool_b, pool_scale, kt, v, wxo_b, wo_b, ln_g, ln_b, wr, br,
           bn, seq):
    n, d = x2.shape
    tm = TM_MIX
    spb = seq // tm
    const2 = lambda b, s: (0, 0)
    const3 = lambda b, s: (0, 0, 0)
    one = pl.Buffered(1)
    row_map = lambda b, s: (b * spb + s, 0)
    return pl.pallas_call(
        _mixer_kernel,
        grid=(bn, spb),
        in_specs=[
            pl.BlockSpec((tm, d), row_map),
            pl.BlockSpec(w_in_b.shape, const2, pipeline_mode=one),
            pl.BlockSpec(b_gate3.shape, const2, pipeline_mode=one),
            pl.BlockSpec(conv_w.shape, const2, pipeline_mode=one),
            pl.BlockSpec(wco_b.shape, const2, pipeline_mode=one),
            pl.BlockSpec(wpool_b.shape, const3, pipeline_mode=one),
            pl.BlockSpec(pool_scale.shape, const2, pipeline_mode=one),
            pl.BlockSpec((1,) + kt.shape[1:], lambda b, s: (b, 0, 0)),
            pl.BlockSpec((1,) + v.shape[1:], lambda b, s: (b, 0, 0)),
            pl.BlockSpec(wxo_b.shape, const2, pipeline_mode=one),
            pl.BlockSpec(wo_b.shape, const2, pipeline_mode=one),
            pl.BlockSpec(ln_g.shape, const2, pipeline_mode=one),
            pl.BlockSpec(ln_b.shape, const2, pipeline_mode=one),
            pl.BlockSpec(wr.shape, const2, pipeline_mode=one),
            pl.BlockSpec(br.shape, const2, pipeline_mode=one),
        ],
        out_specs=[
            pl.BlockSpec((tm * SUB, LANES), row_map),
            pl.BlockSpec((tm, LANES), row_map),
            pl.BlockSpec((1, 8, tm), lambda b, s: (b * spb + s, 0, 0)),
            pl.BlockSpec((8, LANES), const2),
        ],
        out_shape=[
            jax.ShapeDtypeStruct((n * SUB, LANES), F32),
            jax.ShapeDtypeStruct((n, LANES), F32),
            jax.ShapeDtypeStruct((n // tm, 8, tm), jnp.int32),
            jax.ShapeDtypeStruct((8, LANES), jnp.int32),
        ],
        scratch_shapes=[
            pltpu.VMEM((tm + CONV_HALO, d), F32),
            pltpu.VMEM((tm + POOL_HALO, d), F32),
            pltpu.VMEM((8, LANES), F32),
        ],
        compiler_params=pltpu.CompilerParams(
            dimension_semantics=("arbitrary", "arbitrary"), vmem_limit_bytes=VMEM_LIMIT),
        name="mixer",
    )(x2, w_in_b, b_gate3, conv_w, wco_b, wpool_b, pool_scale, kt, v, wxo_b, wo_b, ln_g, ln_b, wr, br)


def _dest_kernel(ps_ref, ri_ref, dest_ref):
    dest_ref[...] = jnp.zeros(dest_ref.shape, jnp.int32)
    for t in range(dest_ref.shape[0]):
        ri = ri_ref[t]
        e = ri[0:TOP_K, :]
        start = jnp.zeros_like(e)
        for j in range(N_EXPERTS):
            start = jnp.where(e == j, ps_ref[j], start)
        dest_ref[t, 0:TOP_K, :] = start + ri[TOP_K:2 * TOP_K, :]


def _dest(pstarts, route_i):
    nt, rows, tm = route_i.shape
    blk = 8
    spec = pl.BlockSpec((blk, rows, tm), lambda i, ps: (i, 0, 0))
    return pl.pallas_call(
        _dest_kernel,
        grid_spec=pltpu.PrefetchScalarGridSpec(
            num_scalar_prefetch=1, grid=(nt // blk,), in_specs=[spec], out_specs=spec),
        out_shape=jax.ShapeDtypeStruct(route_i.shape, jnp.int32),
        compiler_params=pltpu.CompilerParams(dimension_semantics=("arbitrary",)),
        name="dest_rows",
    )(pstarts, route_i)


def _scatter_kernel(zt_ref, dest_ref, h_ref, wup_ref, wdn_ref, xbuf_ref, wupb_ref, wdnb_ref, zbuf, sem):
    i = pl.program_id(0)
    tm = TM_ROW

    def zero_copy(e):
        start = pl.multiple_of(zt_ref[e] * SUB, TM_EXP * SUB)
        return pltpu.make_async_copy(zbuf, xbuf_ref.at[pl.ds(start, TM_EXP * SUB), :], sem.at[2])

    @pl.when(i == 0)
    def _():
        zbuf[...] = jnp.zeros_like(zbuf)

        def start(e, c):
            @pl.when(zt_ref[e] >= 0)
            def _():
                zero_copy(e).start()
            return c

        def wait(e, c):
            @pl.when(zt_ref[e] >= 0)
            def _():
                zero_copy(e).wait()
            return c

        lax.fori_loop(0, 2 * N_EXPERTS, start, 0)
        lax.fori_loop(0, 2 * N_EXPERTS, wait, 0)

    def issue(r, c):
        for k in range(TOP_K):
            _tile_copy(h_ref, r, xbuf_ref, dest_ref[k * tm + r], sem.at[k]).start(priority=k)
        return c

    lax.fori_loop(0, tm, issue, 0, unroll=8)
    wupb_ref[...] = wup_ref[...].astype(BF16)
    wdnb_ref[...] = wdn_ref[...].astype(BF16)
    for k in range(TOP_K):
        pltpu.make_async_copy(h_ref, xbuf_ref.at[pl.ds(0, tm * SUB), :], sem.at[k]).wait()


def _scatter(zero_tiles, dest_flat, h1t, w_up, w_down, p_rows):
    tm = TM_ROW
    nt = dest_flat.shape[0] // (TOP_K * tm)
    assert nt >= N_EXPERTS, "one expert's weights are converted per grid step"
    w_map = lambda i, zt: (jnp.minimum(i, N_EXPERTS - 1), 0, 0)
    wup_blk = (1,) + w_up.shape[1:]
    wdn_blk = (1,) + w_down.shape[1:]
    return pl.pallas_call(
        _scatter_kernel,
        grid_spec=pltpu.PrefetchScalarGridSpec(
            num_scalar_prefetch=1,
            grid=(nt,),
            in_specs=[
                pl.BlockSpec((TOP_K * tm,), lambda i, zt: (i,), memory_space=pltpu.SMEM),
                pl.BlockSpec((tm * SUB, LANES), lambda i, zt: (i, 0)),
                pl.BlockSpec(wup_blk, w_map),
                pl.BlockSpec(wdn_blk, w_map),
            ],
            out_specs=[
                pl.BlockSpec(memory_space=pl.ANY),
                pl.BlockSpec(wup_blk, w_map),
                pl.BlockSpec(wdn_blk, w_map),
            ],
            scratch_shapes=[pltpu.VMEM((TM_EXP * SUB, LANES), F32), pltpu.SemaphoreType.DMA((3,))],
        ),
        out_shape=[
            jax.ShapeDtypeStruct((p_rows * SUB, LANES), F32),
            jax.ShapeDtypeStruct(w_up.shape, BF16),
            jax.ShapeDtypeStruct(w_down.shape, BF16),
        ],
        compiler_params=pltpu.CompilerParams(dimension_semantics=("arbitrary",), vmem_limit_bytes=VMEM_LIMIT),
        name="scatter_rows",
    )(zero_tiles, dest_flat, h1t, w_up, w_down)


def _expert_kernel(t0_ref, nt_ref, tot_ref, wup_ref, wdn_ref, xbuf_ref, ybuf_ref, xs, ys, sem_in, sem_out):
    e = pl.program_id(0)
    t0 = t0_ref[e]
    nt = nt_ref[e]
    total = tot_ref[0]
    rows = TM_EXP * SUB
    max_tiles = xbuf_ref.shape[0] // rows

    def hbm_tile(ref, g):
        return ref.at[pl.ds(pl.multiple_of(g * rows, rows), rows), :]

    def x_copy(g, slot):
        return pltpu.make_async_copy(hbm_tile(xbuf_ref, g), xs.at[slot], sem_in.at[slot])

    def y_copy(g, slot):
        return pltpu.make_async_copy(ys.at[slot], hbm_tile(ybuf_ref, g), sem_out.at[slot])

    @pl.when(e == 0)
    def _():
        x_copy(0, 0).start(priority=ROW_DMA_PRIORITY)

    def tile(j, c):
        g = t0 + j
        slot = lax.rem(g, 2)

        @pl.when(g + 1 < total)
        def _():
            x_copy(g + 1, 1 - slot).start(priority=ROW_DMA_PRIORITY)

        x_copy(g, slot).wait()

        @pl.when(g >= 2)
        def _():
            y_copy(g - 2, slot).wait()

        xb = _tok_load(xs.at[slot], TM_EXP).astype(BF16)
        hgv = _dot(xb, wup_ref[0])
        hg = hgv[:, :D_EXPERT]
        hv = hgv[:, D_EXPERT:]
        act = (hg * jax.nn.sigmoid(hg)) * hv
        _tok_store(ys.at[slot], _dot(act.astype(BF16), wdn_ref[0]), TM_EXP)
        y_copy(g, slot).start(priority=ROW_DMA_PRIORITY)
        return c

    lax.fori_loop(0, nt, tile, 0)

    @pl.when(e == pl.num_programs(0) - 1)
    def _():
        @pl.when(total >= 2)
        def _():
            y_copy(total - 2, lax.rem(total, 2)).wait()

        y_copy(total - 1, lax.rem(total - 1, 2)).wait()
        ys[0] = jnp.zeros(ys.shape[1:], F32)

        def zero_tail(g, c):
            y_copy(g, 0).start()
            y_copy(g, 0).wait()
            return c

        lax.fori_loop(total, max_tiles, zero_tail, 0)


def _experts(tile_start, tile_count, n_tiles, xbuf, w_up, w_down):
    w_map = lambda e, t0, nt, tot: (e, 0, 0)
    rows = TM_EXP * SUB
    return pl.pallas_call(
        _expert_kernel,
        grid_spec=pltpu.PrefetchScalarGridSpec(
            num_scalar_prefetch=3,
            grid=(N_EXPERTS,),
            in_specs=[
                pl.BlockSpec((1,) + w_up.shape[1:], w_map),
                pl.BlockSpec((1,) + w_down.shape[1:], w_map),
                pl.BlockSpec(memory_space=pl.ANY),
            ],
            out_specs=pl.BlockSpec(memory_space=pl.ANY),
            scratch_shapes=[
                pltpu.VMEM((2, rows, LANES), F32), pltpu.VMEM((2, rows, LANES), F32),
                pltpu.SemaphoreType.DMA((2,)), pltpu.SemaphoreType.DMA((2,)),
            ],
        ),
        out_shape=jax.ShapeDtypeStruct(xbuf.shape, F32),
        compiler_params=pltpu.CompilerParams(dimension_semantics=("arbitrary",), vmem_limit_bytes=VMEM_LIMIT),
        name="experts",
    )(tile_start, tile_count, n_tiles, w_up, w_down, xbuf)


def _combine_kernel(dcur_ref, dnext_ref, h_ref, route_ref, ybuf_ref, g_ref, b_ref, out_ref, ybufs, sem):
    tm = TM_ROW
    i = pl.program_id(0)
    slot = lax.rem(i, 2)

    def gather(d_ref, s):
        def issue(r, c):
            for k in range(TOP_K):
                _tile_copy(ybuf_ref, d_ref[k * tm + r], ybufs.at[s, k], r, sem.at[s, k]).start(priority=k)
            return c

        lax.fori_loop(0, tm, issue, 0, unroll=8)

    @pl.when(i == 0)
    def _():
        gather(dcur_ref, 0)

    @pl.when(i + 1 < pl.num_programs(0))
    def _():
        gather(dnext_ref, 1 - slot)

    for k in range(TOP_K):
        pltpu.make_async_copy(ybuf_ref.at[pl.ds(0, tm * SUB), :], ybufs.at[slot, k], sem.at[slot, k]).wait()
    route = route_ref[...]
    ffn = route[:, 4:5] * _tok_load(ybufs.at[slot, 0], tm) + route[:, 5:6] * _tok_load(ybufs.at[slot, 1], tm)
    out_ref[...] = _layer_norm(ALPHA * _tok_load(h_ref, tm) + ffn, g_ref[...], b_ref[...])


def _combine(dest_flat, h1t, route, ybuf, ln_g, ln_b):
    tm = TM_ROW
    nt = dest_flat.shape[0] // (TOP_K * tm)
    n = nt * tm
    return pl.pallas_call(
        _combine_kernel,
        grid=(nt,),
        in_specs=[
            pl.BlockSpec((TOP_K * tm,), lambda i: (i,), memory_space=pltpu.SMEM),
            pl.BlockSpec((TOP_K * tm,), lambda i: (jnp.minimum(i + 1, nt - 1),), memory_space=pltpu.SMEM),
            pl.BlockSpec((tm * SUB, LANES), lambda i: (i, 0)),
            pl.BlockSpec((tm, LANES), lambda i: (i, 0)),
            pl.BlockSpec(memory_space=pl.ANY),
            pl.BlockSpec(ln_g.shape, lambda i: (0, 0)),
            pl.BlockSpec(ln_b.shape, lambda i: (0, 0)),
        ],
        out_specs=pl.BlockSpec((tm, D_MODEL), lambda i: (i, 0)),
        out_shape=jax.ShapeDtypeStruct((n, D_MODEL), F32),
        scratch_shapes=[pltpu.VMEM((2, TOP_K, tm * SUB, LANES), F32), pltpu.SemaphoreType.DMA((2, TOP_K))],
        compiler_params=pltpu.CompilerParams(dimension_semantics=("arbitrary",)),
        name="combine_ln2",
    )(dest_flat, dest_flat, h1t, route, ybuf, ln_g, ln_b)


def _layer(h, mem, w_in, b_gate, conv_w, w_conv_out, w_pool, pool_scale, w_kv, w_xo, w_o, ln1_g, ln1_b,
           w_rg, b_rg, w_re, b_re, w_up, w_down, ln2_g, ln2_b):
    bn, seq, d = h.shape
    n = bn * seq
    xw = X_HEADS * X_HEAD_DIM

    kt, v = _kv_proj(mem, w_kv[:, :xw].T.astype(BF16), w_kv[:, xw:].astype(BF16))

    pad = LANES - N_EXPERTS - N_GROUPS
    w_r = jnp.concatenate([w_re, w_rg, jnp.zeros((d, pad), F32)], axis=1)
    wr_hi = w_r.astype(BF16)
    wr_lo = (w_r - wr_hi.astype(F32)).astype(BF16)
    b_r = jnp.concatenate([b_re, b_rg, jnp.zeros((pad,), F32)])[None, :]

    h1t, route, route_i, cnt = _mixer(
        h.reshape(n, d), w_in.astype(BF16), b_gate.reshape(3, d), conv_w, w_conv_out.astype(BF16),
        w_pool.astype(BF16), pool_scale[None, :], kt, v, w_xo.astype(BF16), w_o.astype(BF16),
        ln1_g[None, :], ln1_b[None, :], jnp.concatenate([wr_hi, wr_lo], axis=1), b_r, bn, seq)

    counts = cnt[0, :N_EXPERTS]
    padded = (counts + TM_EXP - 1) // TM_EXP * TM_EXP
    pends = jnp.cumsum(padded)
    pstarts = pends - padded
    max_tiles = (n * TOP_K + N_EXPERTS * (TM_EXP - 1)) // TM_EXP
    p_rows = max_tiles * TM_EXP
    n_tiles = (pends[-1] // TM_EXP).astype(jnp.int32)
    tail_ids = n_tiles + jnp.arange(N_EXPERTS, dtype=jnp.int32)
    zero_tiles = jnp.concatenate([
        jnp.where(counts % TM_EXP != 0, pends - TM_EXP, -1),
        jnp.where(tail_ids < max_tiles, tail_ids * TM_EXP, -1)]).astype(jnp.int32)

    dest = _dest(pstarts.astype(jnp.int32), route_i)[:, :TOP_K, :].reshape(-1)
    xbuf, w_up_b, w_down_b = _scatter(zero_tiles, dest, h1t, w_up, w_down, p_rows)
    ybuf = _experts((pstarts // TM_EXP).astype(jnp.int32), (padded // TM_EXP).astype(jnp.int32), n_tiles.reshape(1),
                    xbuf, w_up_b, w_down_b)
    out = _combine(dest, h1t, route, ybuf, ln2_g[None, :], ln2_b[None, :])
    return out.reshape(bn, seq, d)


def kernel(x, mem, w_in, b_gate, conv_w, w_conv_out, w_pool, pool_scale, w_kv, w_xo, w_o, ln1_g, ln1_b,
           w_router_group, b_router_group, w_router_expert, b_router_expert, w_up, w_down, ln2_g, ln2_b):
    h = x
    for l in range(DEPTH):
        h = _layer(h, mem, w_in[l], b_gate[l], conv_w[l], w_conv_out[l], w_pool[l], pool_scale[l], w_kv[l],
                   w_xo[l], w_o[l], ln1_g[l], ln1_b[l], w_router_group[l], b_router_group[l],
                   w_router_expert[l], b_router_expert[l], w_up[l], w_down[l], ln2_g[l], ln2_b[l])
    return h
```

```python
import functools

import jax
import jax.numpy as jnp
from jax import lax
from jax.experimental import pallas as pl
from jax.experimental.pallas import tpu as pltpu

D_MODEL = 1024
CONV_K = 3
POOL_WINDOWS = (2, 4, 8, 16)
POOL_GROUP_DIM = 256
X_HEADS = 4
X_HEAD_DIM = 256
N_GROUPS = 8
EXPERTS_PER_GROUP = 8
N_EXPERTS = 64
TOP_K = 2
D_EXPERT = 512
DEPTH = 1
ALPHA = (2.0 * DEPTH) ** 0.25
LN_EPS = 1e-5

LANES = 128
SUB = 8
assert D_MODEL == SUB * LANES
POOL_HALO = 16
POOL_PAD = 8
assert POOL_WINDOWS[0] == 2 and all(b == 2 * a for a, b in zip(POOL_WINDOWS, POOL_WINDOWS[1:]))
assert POOL_PAD >= POOL_WINDOWS[-2] and POOL_HALO >= POOL_WINDOWS[-1] and len(POOL_WINDOWS) == 4
CONV_HALO = 8
TM_MIX = 512
TM_EXP = 256
TM_ROW = TM_MIX
VMEM_LIMIT = 58 * 1024 * 1024
ROW_DMA_PRIORITY = 1

F32 = jnp.float32
BF16 = jnp.bfloat16


def _dot(a, b):
    return jnp.dot(a, b, preferred_element_type=F32)


def _tok_load(ref, tm):
    return jnp.concatenate([ref[pl.ds(c, tm, stride=SUB), :] for c in range(SUB)], axis=1)


def _tok_store(ref, val, tm):
    for c in range(SUB):
        ref[pl.ds(c, tm, stride=SUB), :] = val[:, c * LANES:(c + 1) * LANES]


def _tile_copy(src, src_tok, dst, dst_tok, sem):
    s = pl.multiple_of(src_tok * SUB, SUB)
    t = pl.multiple_of(dst_tok * SUB, SUB)
    return pltpu.make_async_copy(src.at[pl.ds(s, SUB), :], dst.at[pl.ds(t, SUB), :], sem)


def _layer_norm(h, g, b):
    mu = jnp.mean(h, axis=-1, keepdims=True)
    c = h - mu
    var = jnp.mean(c * c, axis=-1, keepdims=True)
    return c * lax.rsqrt(var + LN_EPS) * g + b


def _kv_kernel(mem_ref, wkt_ref, wv_ref, kt_ref, v_ref):
    mb = mem_ref[0].astype(BF16)
    kt = lax.dot_general(wkt_ref[...], mb, (((1,), (1,)), ((), ())), preferred_element_type=F32)
    kt_ref[0] = kt.astype(BF16)
    v_ref[0] = _dot(mb, wv_ref[...]).astype(BF16)


def _kv_proj(mem, wkt, wv):
    bn, m, d = mem.shape
    return pl.pallas_call(
        _kv_kernel,
        grid=(bn,),
        in_specs=[
            pl.BlockSpec((1, m, d), lambda b: (b, 0, 0)),
            pl.BlockSpec((d, d), lambda b: (0, 0)),
            pl.BlockSpec((d, d), lambda b: (0, 0)),
        ],
        out_specs=[
            pl.BlockSpec((1, d, m), lambda b: (b, 0, 0)),
            pl.BlockSpec((1, m, d), lambda b: (b, 0, 0)),
        ],
        out_shape=[jax.ShapeDtypeStruct((bn, d, m), BF16), jax.ShapeDtypeStruct((bn, m, d), BF16)],
        compiler_params=pltpu.CompilerParams(dimension_semantics=("arbitrary",)),
        name="kv_proj",
    )(mem, wkt, wv)


def _mixer_kernel(x_ref, win_ref, bg_ref, cw_ref, wco_ref, wpool_ref, ps_ref, kt_ref, v_ref, wxo_ref, wo_ref,
                  g1_ref, b1_ref, wr_ref, br_ref,
                  h1_ref, route_ref, ri_ref, cnt_ref,
                  ubuf, pbuf, s1buf, s2buf, s3buf, carry):
    tm = TM_MIX
    d = D_MODEL
    b = pl.program_id(0)
    s = pl.program_id(1)

    @pl.when(s == 0)
    def _():
        ubuf[0:CONV_HALO, :] = jnp.zeros((CONV_HALO, d), F32)
        pbuf[0:POOL_PAD + POOL_HALO, :] = jnp.zeros((POOL_PAD + POOL_HALO, d), F32)
        for sbuf in (s1buf, s2buf, s3buf):
            sbuf[0:POOL_PAD, :] = jnp.zeros((POOL_PAD, sbuf.shape[1]), F32)

    @pl.when((s == 0) & (b == 0))
    def _():
        carry[...] = jnp.zeros_like(carry)

    x = x_ref[...]
    xb = x.astype(BF16)

    def proj(sec):
        return _dot(xb, win_ref[:, sec * d:(sec + 1) * d])

    def gate(i):
        return jax.nn.sigmoid(proj(5 + i) + bg_ref[i:i + 1, :])

    u = proj(1) * proj(2)
    ubuf[CONV_HALO:CONV_HALO + tm, :] = u
    v = cw_ref[CONV_K - 1:CONV_K, :] * u
    for k in range(CONV_K - 1):
        off = CONV_HALO - (CONV_K - 1) + k
        v = v + cw_ref[k:k + 1, :] * ubuf[off:off + tm, :]
    ubuf[0:CONV_HALO, :] = ubuf[tm:tm + CONV_HALO, :]
    y_conv = _dot((proj(0) * v).astype(BF16), wco_ref[...])
    acc = gate(0) * y_conv

    p = proj(3)
    base = POOL_PAD + POOL_HALO
    n = POOL_HALO + tm
    pbuf[base:base + tm, :] = p
    lvl = pbuf[POOL_PAD:POOL_PAD + n, :] + pbuf[POOL_PAD - 1:POOL_PAD - 1 + n, :]
    sums = [lvl[POOL_HALO:, :POOL_GROUP_DIM]]
    for gi, sbuf in enumerate((s1buf, s2buf, s3buf), start=1):
        shift = POOL_WINDOWS[gi - 1]
        rest = lvl[:, POOL_GROUP_DIM:]
        sbuf[POOL_PAD:POOL_PAD + n, :] = rest
        lvl = rest + sbuf[POOL_PAD - shift:POOL_PAD - shift + n, :]
        sums.append(lvl[POOL_HALO:, :POOL_GROUP_DIM])
    pos = (s * tm + 1 + lax.broadcasted_iota(jnp.int32, (tm, 1), 0)).astype(F32)
    ys = []
    for gi, w in enumerate(POOL_WINDOWS):
        c0 = gi * POOL_GROUP_DIM
        inv_cnt = 1.0 / jnp.minimum(pos, float(w))
        dg = (sums[gi] * inv_cnt - p[:, c0:c0 + POOL_GROUP_DIM]).astype(BF16)
        ys.append(_dot(dg, wpool_ref[gi]))
    pbuf[POOL_PAD:base, :] = pbuf[POOL_PAD + tm:base + tm, :]
    y_pool = jnp.concatenate(ys, axis=1) * ps_ref[...]
    acc = acc + gate(1) * y_pool

    qb = proj(4).astype(BF16)
    scale = X_HEAD_DIM ** -0.5
    os_ = []
    for h in range(X_HEADS):
        c0 = h * X_HEAD_DIM
        c1 = c0 + X_HEAD_DIM
        sc = _dot(qb[:, c0:c1], kt_ref[0, c0:c1, :]) * scale
        e = jnp.exp(sc - jnp.max(sc, axis=-1, keepdims=True))
        a = e * (1.0 / jnp.sum(e, axis=-1, keepdims=True))
        os_.append(_dot(a.astype(BF16), v_ref[0, :, c0:c1]))
    y_mem = _dot(jnp.concatenate(os_, axis=1).astype(BF16), wxo_ref[...])
    acc = acc + gate(2) * y_mem

    mix = _dot(acc.astype(BF16), wo_ref[...])
    h1 = _layer_norm(ALPHA * x + mix, g1_ref[...], b1_ref[...])
    _tok_store(h1_ref, h1, tm)

    hh = h1.astype(BF16)
    hl = (h1 - hh.astype(F32)).astype(BF16)
    wrh = wr_ref[:, :LANES]
    logits = _dot(hh, wrh) + _dot(hl, wrh) + _dot(hh, wr_ref[:, LANES:]) + br_ref[...]
    lane = lax.broadcasted_iota(jnp.int32, (tm, LANES), 1)
    lane_f = lane.astype(F32)
    neg = jnp.float32(-jnp.inf)
    big = jnp.float32(1e9)

    def first_argmax(vals):
        m = jnp.max(vals, axis=-1, keepdims=True)
        idx = jnp.min(jnp.where(vals == m, lane_f, big), axis=-1, keepdims=True)
        return m, idx

    is_g = (lane >= N_EXPERTS) & (lane < N_EXPERTS + N_GROUPS)
    gmax, gidx = first_argmax(jnp.where(is_g, logits, neg))
    g_w = 1.0 / jnp.sum(jnp.where(is_g, jnp.exp(logits - gmax), 0.0), axis=-1, keepdims=True)
    gsel = gidx.astype(jnp.int32) - N_EXPERTS
    in_grp = (lane >> 3) == gsel
    le = jnp.where(in_grp, logits, neg)
    m1, i1 = first_argmax(le)
    m2, i2 = first_argmax(jnp.where(lane_f == i1, neg, le))
    t = jnp.exp(m2 - m1)
    den = 1.0 + t
    w1 = g_w / den
    w2 = g_w * t / den

    sel1 = lane_f == i1
    sel2 = lane_f == i2
    onehot = (sel1 | sel2).astype(BF16)
    row_i = lax.broadcasted_iota(jnp.int32, (tm, tm), 0)
    col_i = lax.broadcasted_iota(jnp.int32, (tm, tm), 1)
    before = (col_i < row_i).astype(BF16)
    prior = _dot(before, onehot) + carry[0:1, :]
    r1 = jnp.sum(jnp.where(sel1, prior, 0.0), axis=-1, keepdims=True)
    r2 = jnp.sum(jnp.where(sel2, prior, 0.0), axis=-1, keepdims=True)
    carry[...] = carry[...] + jnp.sum(onehot.astype(F32), axis=0, keepdims=True)
    cnt_ref[...] = carry[...].astype(jnp.int32)

    route = jnp.where(lane == 0, i1, 0.0)
    for k, col in enumerate((i2, r1, r2, w1, w2), start=1):
        route = jnp.where(lane == k, col, route)
    route_ref[...] = route
    ri_ref[0] = route.T[0:8, :].astype(jnp.int32)


def _mixer(x2, w_in_b, b_gate3, conv_w, wco_b, wpool_b, pool_scale, kt, v, wxo_b, wo_b, ln_g, ln_b, wr, br,
           bn, seq):
    n, d = x2.shape
    tm = TM_MIX
    spb = seq // tm
    const2 = lambda b, s: (0, 0)
    const3 = lambda b, s: (0, 0, 0)
    one = pl.Buffered(1)
    row_map = lambda b, s: (b * spb + s, 0)
    return pl.pallas_call(
        _mixer_kernel,
        grid=(bn, spb),
        in_specs=[
            pl.BlockSpec((tm, d), row_map),
            pl.BlockSpec(w_in_b.shape, const2, pipeline_mode=one),
            pl.BlockSpec(b_gate3.shape, const2, pipeline_mode=one),
            pl.BlockSpec(conv_w.shape, const2, pipeline_mode=one),
            pl.BlockSpec(wco_b.shape, const2, pipeline_mode=one),
            pl.BlockSpec(wpool_b.shape, const3, pipeline_mode=one),
            pl.BlockSpec(pool_scale.shape, const2, pipeline_mode=one),
            pl.BlockSpec((1,) + kt.shape[1:], lambda b, s: (b, 0, 0)),
            pl.BlockSpec((1,) + v.shape[1:], lambda b, s: (b, 0, 0)),
            pl.BlockSpec(wxo_b.shape, const2, pipeline_mode=one),
            pl.BlockSpec(wo_b.shape, const2, pipeline_mode=one),
            pl.BlockSpec(ln_g.shape, const2, pipeline_mode=one),
            pl.BlockSpec(ln_b.shape, const2, pipeline_mode=one),
            pl.BlockSpec(wr.shape, const2, pipeline_mode=one),
            pl.BlockSpec(br.shape, const2, pipeline_mode=one),
        ],
        out_specs=[
            pl.BlockSpec((tm * SUB, LANES), row_map),
            pl.BlockSpec((tm, LANES), row_map),
            pl.BlockSpec((1, 8, tm), lambda b, s: (b * spb + s, 0, 0)),
            pl.BlockSpec((8, LANES), const2),
        ],
        out_shape=[
            jax.ShapeDtypeStruct((n * SUB, LANES), F32),
            jax.ShapeDtypeStruct((n, LANES), F32),
            jax.ShapeDtypeStruct((n // tm, 8, tm), jnp.int32),
            jax.ShapeDtypeStruct((8, LANES), jnp.int32),
        ],
        scratch_shapes=[
            pltpu.VMEM((tm + CONV_HALO, d), F32),
            pltpu.VMEM((POOL_PAD + POOL_HALO + tm, d), F32),
            pltpu.VMEM((POOL_PAD + POOL_HALO + tm, d - POOL_GROUP_DIM), F32),
            pltpu.VMEM((POOL_PAD + POOL_HALO + tm, d - 2 * POOL_GROUP_DIM), F32),
            pltpu.VMEM((POOL_PAD + POOL_HALO + tm, d - 3 * POOL_GROUP_DIM), F32),
            pltpu.VMEM((8, LANES), F32),
        ],
        compiler_params=pltpu.CompilerParams(
            dimension_semantics=("arbitrary", "arbitrary"), vmem_limit_bytes=VMEM_LIMIT),
        name="mixer",
    )(x2, w_in_b, b_gate3, conv_w, wco_b, wpool_b, pool_scale, kt, v, wxo_b, wo_b, ln_g, ln_b, wr, br)


def _dest_kernel(ps_ref, ri_ref, dest_ref):
    dest_ref[...] = jnp.zeros(dest_ref.shape, jnp.int32)
    for t in range(dest_ref.shape[0]):
        ri = ri_ref[t]
        e = ri[0:TOP_K, :]
        start = jnp.zeros_like(e)
        for j in range(N_EXPERTS):
            start = jnp.where(e == j, ps_ref[j], start)
        dest_ref[t, 0:TOP_K, :] = start + ri[TOP_K:2 * TOP_K, :]


def _dest(pstarts, route_i):
    nt, rows, tm = route_i.shape
    blk = 8
    spec = pl.BlockSpec((blk, rows, tm), lambda i, ps: (i, 0, 0))
    return pl.pallas_call(
        _dest_kernel,
        grid_spec=pltpu.PrefetchScalarGridSpec(
            num_scalar_prefetch=1, grid=(nt // blk,), in_specs=[spec], out_specs=spec),
        out_shape=jax.ShapeDtypeStruct(route_i.shape, jnp.int32),
        compiler_params=pltpu.CompilerParams(dimension_semantics=("arbitrary",)),
        name="dest_rows",
    )(pstarts, route_i)


def _scatter_kernel(zt_ref, dest_ref, h_ref, wup_ref, wdn_ref, xbuf_ref, wupb_ref, wdnb_ref, zbuf, sem):
    i = pl.program_id(0)
    tm = TM_ROW

    def zero_copy(e):
        start = pl.multiple_of(zt_ref[e] * SUB, TM_EXP * SUB)
        return pltpu.make_async_copy(zbuf, xbuf_ref.at[pl.ds(start, TM_EXP * SUB), :], sem.at[2])

    @pl.when(i == 0)
    def _():
        zbuf[...] = jnp.zeros_like(zbuf)

        def start(e, c):
            @pl.when(zt_ref[e] >= 0)
            def _():
                zero_copy(e).start()
            return c

        def wait(e, c):
            @pl.when(zt_ref[e] >= 0)
            def _():
                zero_copy(e).wait()
            return c

        lax.fori_loop(0, 2 * N_EXPERTS, start, 0)
        lax.fori_loop(0, 2 * N_EXPERTS, wait, 0)

    def issue(r, c):
        for k in range(TOP_K):
            _tile_copy(h_ref, r, xbuf_ref, dest_ref[k * tm + r], sem.at[k]).start(priority=k)
        return c

    lax.fori_loop(0, tm, issue, 0, unroll=8)
    wupb_ref[...] = wup_ref[...].astype(BF16)
    wdnb_ref[...] = wdn_ref[...].astype(BF16)
    for k in range(TOP_K):
        pltpu.make_async_copy(h_ref, xbuf_ref.at[pl.ds(0, tm * SUB), :], sem.at[k]).wait()


def _scatter(zero_tiles, dest_flat, h1t, w_up, w_down, p_rows):
    tm = TM_ROW
    nt = dest_flat.shape[0] // (TOP_K * tm)
    assert nt >= N_EXPERTS, "one expert's weights are converted per grid step"
    w_map = lambda i, zt: (jnp.minimum(i, N_EXPERTS - 1), 0, 0)
    wup_blk = (1,) + w_up.shape[1:]
    wdn_blk = (1,) + w_down.shape[1:]
    return pl.pallas_call(
        _scatter_kernel,
        grid_spec=pltpu.PrefetchScalarGridSpec(
            num_scalar_prefetch=1,
            grid=(nt,),
            in_specs=[
                pl.BlockSpec((TOP_K * tm,), lambda i, zt: (i,), memory_space=pltpu.SMEM),
                pl.BlockSpec((tm * SUB, LANES), lambda i, zt: (i, 0)),
                pl.BlockSpec(wup_blk, w_map),
                pl.BlockSpec(wdn_blk, w_map),
            ],
            out_specs=[
                pl.BlockSpec(memory_space=pl.ANY),
                pl.BlockSpec(wup_blk, w_map),
                pl.BlockSpec(wdn_blk, w_map),
            ],
            scratch_shapes=[pltpu.VMEM((TM_EXP * SUB, LANES), F32), pltpu.SemaphoreType.DMA((3,))],
        ),
        out_shape=[
            jax.ShapeDtypeStruct((p_rows * SUB, LANES), F32),
            jax.ShapeDtypeStruct(w_up.shape, BF16),
            jax.ShapeDtypeStruct(w_down.shape, BF16),
        ],
        compiler_params=pltpu.CompilerParams(dimension_semantics=("arbitrary",), vmem_limit_bytes=VMEM_LIMIT),
        name="scatter_rows",
    )(zero_tiles, dest_flat, h1t, w_up, w_down)


def _expert_kernel(t0_ref, nt_ref, tot_ref, wup_ref, wdn_ref, xbuf_ref, ybuf_ref, xs, ys, sem_in, sem_out):
    e = pl.program_id(0)
    t0 = t0_ref[e]
    nt = nt_ref[e]
    total = tot_ref[0]
    rows = TM_EXP * SUB
    max_tiles = xbuf_ref.shape[0] // rows

    def hbm_tile(ref, g):
        return ref.at[pl.ds(pl.multiple_of(g * rows, rows), rows), :]

    def x_copy(g, slot):
        return pltpu.make_async_copy(hbm_tile(xbuf_ref, g), xs.at[slot], sem_in.at[slot])

    def y_copy(g, slot):
        return pltpu.make_async_copy(ys.at[slot], hbm_tile(ybuf_ref, g), sem_out.at[slot])

    @pl.when(e == 0)
    def _():
        x_copy(0, 0).start(priority=ROW_DMA_PRIORITY)

    def tile(j, c):
        g = t0 + j
        slot = lax.rem(g, 2)

        @pl.when(g + 1 < total)
        def _():
            x_copy(g + 1, 1 - slot).start(priority=ROW_DMA_PRIORITY)

        x_copy(g, slot).wait()

        @pl.when(g >= 2)
        def _():
            y_copy(g - 2, slot).wait()

        xb = _tok_load(xs.at[slot], TM_EXP).astype(BF16)
        hgv = _dot(xb, wup_ref[0])
        hg = hgv[:, :D_EXPERT]
        hv = hgv[:, D_EXPERT:]
        act = (hg * jax.nn.sigmoid(hg)) * hv
        _tok_store(ys.at[slot], _dot(act.astype(BF16), wdn_ref[0]), TM_EXP)
        y_copy(g, slot).start(priority=ROW_DMA_PRIORITY)
        return c

    lax.fori_loop(0, nt, tile, 0)

    @pl.when(e == pl.num_programs(0) - 1)
    def _():
        @pl.when(total >= 2)
        def _():
            y_copy(total - 2, lax.rem(total, 2)).wait()

        y_copy(total - 1, lax.rem(total - 1, 2)).wait()
        ys[0] = jnp.zeros(ys.shape[1:], F32)

        def zero_tail(g, c):
            y_copy(g, 0).start()
            y_copy(g, 0).wait()
            return c

        lax.fori_loop(total, max_tiles, zero_tail, 0)


def _experts(tile_start, tile_count, n_tiles, xbuf, w_up, w_down):
    w_map = lambda e, t0, nt, tot: (e, 0, 0)
    rows = TM_EXP * SUB
    return pl.pallas_call(
        _expert_kernel,
        grid_spec=pltpu.PrefetchScalarGridSpec(
            num_scalar_prefetch=3,
            grid=(N_EXPERTS,),
            in_specs=[
                pl.BlockSpec((1,) + w_up.shape[1:], w_map),
                pl.BlockSpec((1,) + w_down.shape[1:], w_map),
                pl.BlockSpec(memory_space=pl.ANY),
            ],
            out_specs=pl.BlockSpec(memory_space=pl.ANY),
            scratch_shapes=[
                pltpu.VMEM((2, rows, LANES), F32), pltpu.VMEM((2, rows, LANES), F32),
                pltpu.SemaphoreType.DMA((2,)), pltpu.SemaphoreType.DMA((2,)),
            ],
        ),
        out_shape=jax.ShapeDtypeStruct(xbuf.shape, F32),
        compiler_params=pltpu.CompilerParams(dimension_semantics=("arbitrary",), vmem_limit_bytes=VMEM_LIMIT),
        name="experts",
    )(tile_start, tile_count, n_tiles, w_up, w_down, xbuf)


def _combine_kernel(dcur_ref, dnext_ref, h_ref, route_ref, ybuf_ref, g_ref, b_ref, out_ref, ybufs, sem):
    tm = TM_ROW
    i = pl.program_id(0)
    slot = lax.rem(i, 2)

    def gather(d_ref, s):
        def issue(r, c):
            for k in range(TOP_K):
                _tile_copy(ybuf_ref, d_ref[k * tm + r], ybufs.at[s, k], r, sem.at[s, k]).start(priority=k)
            return c

        lax.fori_loop(0, tm, issue, 0, unroll=8)

    @pl.when(i == 0)
    def _():
        gather(dcur_ref, 0)

    @pl.when(i + 1 < pl.num_programs(0))
    def _():
        gather(dnext_ref, 1 - slot)

    for k in range(TOP_K):
        pltpu.make_async_copy(ybuf_ref.at[pl.ds(0, tm * SUB), :], ybufs.at[slot, k], sem.at[slot, k]).wait()
    route = route_ref[...]
    ffn = route[:, 4:5] * _tok_load(ybufs.at[slot, 0], tm) + route[:, 5:6] * _tok_load(ybufs.at[slot, 1], tm)
    out_ref[...] = _layer_norm(ALPHA * _tok_load(h_ref, tm) + ffn, g_ref[...], b_ref[...])


def _combine(dest_flat, h1t, route, ybuf, ln_g, ln_b):
    tm = TM_ROW
    nt = dest_flat.shape[0] // (TOP_K * tm)
    n = nt * tm
    return pl.pallas_call(
        _combine_kernel,
        grid=(nt,),
        in_specs=[
            pl.BlockSpec((TOP_K * tm,), lambda i: (i,), memory_space=pltpu.SMEM),
            pl.BlockSpec((TOP_K * tm,), lambda i: (jnp.minimum(i + 1, nt - 1),), memory_space=pltpu.SMEM),
            pl.BlockSpec((tm * SUB, LANES), lambda i: (i, 0)),
            pl.BlockSpec((tm, LANES), lambda i: (i, 0)),
            pl.BlockSpec(memory_space=pl.ANY),
            pl.BlockSpec(ln_g.shape, lambda i: (0, 0)),
            pl.BlockSpec(ln_b.shape, lambda i: (0, 0)),
        ],
        out_specs=pl.BlockSpec((tm, D_MODEL), lambda i: (i, 0)),
        out_shape=jax.ShapeDtypeStruct((n, D_MODEL), F32),
        scratch_shapes=[pltpu.VMEM((2, TOP_K, tm * SUB, LANES), F32), pltpu.SemaphoreType.DMA((2, TOP_K))],
        compiler_params=pltpu.CompilerParams(dimension_semantics=("arbitrary",)),
        name="combine_ln2",
    )(dest_flat, dest_flat, h1t, route, ybuf, ln_g, ln_b)


def _layer(h, mem, w_in, b_gate, conv_w, w_conv_out, w_pool, pool_scale, w_kv, w_xo, w_o, ln1_g, ln1_b,
           w_rg, b_rg, w_re, b_re, w_up, w_down, ln2_g, ln2_b):
    bn, seq, d = h.shape
    n = bn * seq
    xw = X_HEADS * X_HEAD_DIM

    kt, v = _kv_proj(mem, w_kv[:, :xw].T.astype(BF16), w_kv[:, xw:].astype(BF16))

    pad = LANES - N_EXPERTS - N_GROUPS
    w_r = jnp.concatenate([w_re, w_rg, jnp.zeros((d, pad), F32)], axis=1)
    wr_hi = w_r.astype(BF16)
    wr_lo = (w_r - wr_hi.astype(F32)).astype(BF16)
    b_r = jnp.concatenate([b_re, b_rg, jnp.zeros((pad,), F32)])[None, :]

    h1t, route, route_i, cnt = _mixer(
        h.reshape(n, d), w_in.astype(BF16), b_gate.reshape(3, d), conv_w, w_conv_out.astype(BF16),
        w_pool.astype(BF16), pool_scale[None, :], kt, v, w_xo.astype(BF16), w_o.astype(BF16),
        ln1_g[None, :], ln1_b[None, :], jnp.concatenate([wr_hi, wr_lo], axis=1), b_r, bn, seq)

    counts = cnt[0, :N_EXPERTS]
    padded = (counts + TM_EXP - 1) // TM_EXP * TM_EXP
    pends = jnp.cumsum(padded)
    pstarts = pends - padded
    max_tiles = (n * TOP_K + N_EXPERTS * (TM_EXP - 1)) // TM_EXP
    p_rows = max_tiles * TM_EXP
    n_tiles = (pends[-1] // TM_EXP).astype(jnp.int32)
    tail_ids = n_tiles + jnp.arange(N_EXPERTS, dtype=jnp.int32)
    zero_tiles = jnp.concatenate([
        jnp.where(counts % TM_EXP != 0, pends - TM_EXP, -1),
        jnp.where(tail_ids < max_tiles, tail_ids * TM_EXP, -1)]).astype(jnp.int32)

    dest = _dest(pstarts.astype(jnp.int32), route_i)[:, :TOP_K, :].reshape(-1)
    xbuf, w_up_b, w_down_b = _scatter(zero_tiles, dest, h1t, w_up, w_down, p_rows)
    ybuf = _experts((pstarts // TM_EXP).astype(jnp.int32), (padded // TM_EXP).astype(jnp.int32), n_tiles.reshape(1),
                    xbuf, w_up_b, w_down_b)
    out = _combine(dest, h1t, route, ybuf, ln2_g[None, :], ln2_b[None, :])
    return out.reshape(bn, seq, d)


def kernel(x, mem, w_in, b_gate, conv_w, w_conv_out, w_pool, pool_scale, w_kv, w_xo, w_o, ln1_g, ln1_b,
           w_router_group, b_router_group, w_router_expert, b_router_expert, w_up, w_down, ln2_g, ln2_b):
    h = x
    for l in range(DEPTH):
        h = _layer(h, mem, w_in[l], b_gate[l], conv_w[l], w_conv_out[l], w_pool[l], pool_scale[l], w_kv[l],
                   w_xo[l], w_o[l], ln1_g[l], ln1_b[l], w_router_group[l], b_router_group[l],
                   w_router_expert[l], b_router_expert[l], w_up[l], w_down[l], ln2_g[l], ln2_b[l])
    return h
```

```python
import functools

import jax
import jax.numpy as jnp
from jax import lax
from jax.experimental import pallas as pl
from jax.experimental.pallas import tpu as pltpu

D_MODEL = 1024
CONV_K = 3
POOL_WINDOWS = (2, 4, 8, 16)
POOL_GROUP_DIM = 256
X_HEADS = 4
X_HEAD_DIM = 256
N_GROUPS = 8
EXPERTS_PER_GROUP = 8
N_EXPERTS = 64
TOP_K = 2
D_EXPERT = 512
DEPTH = 1
ALPHA = (2.0 * DEPTH) ** 0.25
LN_EPS = 1e-5

LANES = 128
SUB = 8
assert D_MODEL == SUB * LANES
POOL_HALO = 16
POOL_PAD = 8
assert POOL_WINDOWS[0] == 2 and all(b == 2 * a for a, b in zip(POOL_WINDOWS, POOL_WINDOWS[1:]))
assert POOL_PAD >= POOL_WINDOWS[-2] and POOL_HALO >= POOL_WINDOWS[-1] and len(POOL_WINDOWS) == 4
CONV_HALO = 8
TM_MIX = 512
TM_EXP = 256
TM_ROW = TM_MIX
VMEM_LIMIT = 58 * 1024 * 1024
RING = 4
ROW_DMA_PRIORITY = 1

F32 = jnp.float32
BF16 = jnp.bfloat16


def _dot(a, b):
    return jnp.dot(a, b, preferred_element_type=F32)


def _tok_load(ref, tm):
    return jnp.concatenate([ref[pl.ds(c, tm, stride=SUB), :] for c in range(SUB)], axis=1)


def _tok_store(ref, val, tm):
    for c in range(SUB):
        ref[pl.ds(c, tm, stride=SUB), :] = val[:, c * LANES:(c + 1) * LANES]


def _tile_copy(src, src_tok, dst, dst_tok, sem):
    s = pl.multiple_of(src_tok * SUB, SUB)
    t = pl.multiple_of(dst_tok * SUB, SUB)
    return pltpu.make_async_copy(src.at[pl.ds(s, SUB), :], dst.at[pl.ds(t, SUB), :], sem)


def _layer_norm(h, g, b):
    mu = jnp.mean(h, axis=-1, keepdims=True)
    c = h - mu
    var = jnp.mean(c * c, axis=-1, keepdims=True)
    return c * lax.rsqrt(var + LN_EPS) * g + b


def _kv_kernel(mem_ref, wkt_ref, wv_ref, kt_ref, v_ref):
    mb = mem_ref[0].astype(BF16)
    kt = lax.dot_general(wkt_ref[...], mb, (((1,), (1,)), ((), ())), preferred_element_type=F32)
    kt_ref[0] = kt.astype(BF16)
    v_ref[0] = _dot(mb, wv_ref[...]).astype(BF16)


def _kv_proj(mem, wkt, wv):
    bn, m, d = mem.shape
    return pl.pallas_call(
        _kv_kernel,
        grid=(bn,),
        in_specs=[
            pl.BlockSpec((1, m, d), lambda b: (b, 0, 0)),
            pl.BlockSpec((d, d), lambda b: (0, 0)),
            pl.BlockSpec((d, d), lambda b: (0, 0)),
        ],
        out_specs=[
            pl.BlockSpec((1, d, m), lambda b: (b, 0, 0)),
            pl.BlockSpec((1, m, d), lambda b: (b, 0, 0)),
        ],
        out_shape=[jax.ShapeDtypeStruct((bn, d, m), BF16), jax.ShapeDtypeStruct((bn, m, d), BF16)],
        compiler_params=pltpu.CompilerParams(dimension_semantics=("arbitrary",)),
        name="kv_proj",
    )(mem, wkt, wv)


def _mixer_kernel(x_ref, win_ref, bg_ref, cw_ref, wco_ref, wpool_ref, ps_ref, kt_ref, v_ref, wxo_ref, wo_ref,
                  g1_ref, b1_ref, wr_ref, br_ref,
                  h1_ref, route_ref, ri_ref, cnt_ref,
                  ubuf, pbuf, s1buf, s2buf, s3buf, carry):
    tm = TM_MIX
    d = D_MODEL
    b = pl.program_id(0)
    s = pl.program_id(1)

    @pl.when(s == 0)
    def _():
        ubuf[0:CONV_HALO, :] = jnp.zeros((CONV_HALO, d), F32)
        pbuf[0:POOL_PAD + POOL_HALO, :] = jnp.zeros((POOL_PAD + POOL_HALO, d), F32)
        for sbuf in (s1buf, s2buf, s3buf):
            sbuf[0:POOL_PAD, :] = jnp.zeros((POOL_PAD, sbuf.shape[1]), F32)

    @pl.when((s == 0) & (b == 0))
    def _():
        carry[...] = jnp.zeros_like(carry)

    x = x_ref[...]
    xb = x.astype(BF16)

    def proj(sec):
        return _dot(xb, win_ref[:, sec * d:(sec + 1) * d])

    def gate(i):
        return jax.nn.sigmoid(proj(5 + i) + bg_ref[i:i + 1, :])

    u = proj(1) * proj(2)
    ubuf[CONV_HALO:CONV_HALO + tm, :] = u
    v = cw_ref[CONV_K - 1:CONV_K, :] * u
    for k in range(CONV_K - 1):
        off = CONV_HALO - (CONV_K - 1) + k
        v = v + cw_ref[k:k + 1, :] * ubuf[off:off + tm, :]
    ubuf[0:CONV_HALO, :] = ubuf[tm:tm + CONV_HALO, :]
    y_conv = _dot((proj(0) * v).astype(BF16), wco_ref[...])
    acc = gate(0) * y_conv

    p = proj(3)
    base = POOL_PAD + POOL_HALO
    n = POOL_HALO + tm
    pbuf[base:base + tm, :] = p
    lvl = pbuf[POOL_PAD:POOL_PAD + n, :] + pbuf[POOL_PAD - 1:POOL_PAD - 1 + n, :]
    sums = [lvl[POOL_HALO:, :POOL_GROUP_DIM]]
    for gi, sbuf in enumerate((s1buf, s2buf, s3buf), start=1):
        shift = POOL_WINDOWS[gi - 1]
        rest = lvl[:, POOL_GROUP_DIM:]
        sbuf[POOL_PAD:POOL_PAD + n, :] = rest
        lvl = rest + sbuf[POOL_PAD - shift:POOL_PAD - shift + n, :]
        sums.append(lvl[POOL_HALO:, :POOL_GROUP_DIM])
    pos = (s * tm + 1 + lax.broadcasted_iota(jnp.int32, (tm, 1), 0)).astype(F32)
    ys = []
    for gi, w in enumerate(POOL_WINDOWS):
        c0 = gi * POOL_GROUP_DIM
        inv_cnt = 1.0 / jnp.minimum(pos, float(w))
        dg = (sums[gi] * inv_cnt - p[:, c0:c0 + POOL_GROUP_DIM]).astype(BF16)
        ys.append(_dot(dg, wpool_ref[gi]))
    pbuf[POOL_PAD:base, :] = pbuf[POOL_PAD + tm:base + tm, :]
    y_pool = jnp.concatenate(ys, axis=1) * ps_ref[...]
    acc = acc + gate(1) * y_pool

    qb = proj(4).astype(BF16)
    scale = X_HEAD_DIM ** -0.5
    os_ = []
    for h in range(X_HEADS):
        c0 = h * X_HEAD_DIM
        c1 = c0 + X_HEAD_DIM
        sc = _dot(qb[:, c0:c1], kt_ref[0, c0:c1, :]) * scale
        e = jnp.exp(sc - jnp.max(sc, axis=-1, keepdims=True))
        a = e * (1.0 / jnp.sum(e, axis=-1, keepdims=True))
        os_.append(_dot(a.astype(BF16), v_ref[0, :, c0:c1]))
    y_mem = _dot(jnp.concatenate(os_, axis=1).astype(BF16), wxo_ref[...])
    acc = acc + gate(2) * y_mem

    mix = _dot(acc.astype(BF16), wo_ref[...])
    h1 = _layer_norm(ALPHA * x + mix, g1_ref[...], b1_ref[...])
    _tok_store(h1_ref, h1, tm)

    hh = h1.astype(BF16)
    hl = (h1 - hh.astype(F32)).astype(BF16)
    wrh = wr_ref[:, :LANES]
    logits = _dot(hh, wrh) + _dot(hl, wrh) + _dot(hh, wr_ref[:, LANES:]) + br_ref[...]
    lane = lax.broadcasted_iota(jnp.int32, (tm, LANES), 1)
    lane_f = lane.astype(F32)
    neg = jnp.float32(-jnp.inf)
    big = jnp.float32(1e9)

    def first_argmax(vals):
        m = jnp.max(vals, axis=-1, keepdims=True)
        idx = jnp.min(jnp.where(vals == m, lane_f, big), axis=-1, keepdims=True)
        return m, idx

    is_g = (lane >= N_EXPERTS) & (lane < N_EXPERTS + N_GROUPS)
    gmax, gidx = first_argmax(jnp.where(is_g, logits, neg))
    g_w = 1.0 / jnp.sum(jnp.where(is_g, jnp.exp(logits - gmax), 0.0), axis=-1, keepdims=True)
    gsel = gidx.astype(jnp.int32) - N_EXPERTS
    in_grp = (lane >> 3) == gsel
    le = jnp.where(in_grp, logits, neg)
    m1, i1 = first_argmax(le)
    m2, i2 = first_argmax(jnp.where(lane_f == i1, neg, le))
    t = jnp.exp(m2 - m1)
    den = 1.0 + t
    w1 = g_w / den
    w2 = g_w * t / den

    sel1 = lane_f == i1
    sel2 = lane_f == i2
    onehot = (sel1 | sel2).astype(BF16)
    row_i = lax.broadcasted_iota(jnp.int32, (tm, tm), 0)
    col_i = lax.broadcasted_iota(jnp.int32, (tm, tm), 1)
    before = (col_i < row_i).astype(BF16)
    prior = _dot(before, onehot) + carry[0:1, :]
    r1 = jnp.sum(jnp.where(sel1, prior, 0.0), axis=-1, keepdims=True)
    r2 = jnp.sum(jnp.where(sel2, prior, 0.0), axis=-1, keepdims=True)
    carry[...] = carry[...] + jnp.sum(onehot.astype(F32), axis=0, keepdims=True)
    cnt_ref[...] = carry[...].astype(jnp.int32)

    route = jnp.where(lane == 0, i1, 0.0)
    for k, col in enumerate((i2, r1, r2, w1, w2), start=1):
        route = jnp.where(lane == k, col, route)
    route_ref[...] = route
    ri_ref[0] = route.T[0:8, :].astype(jnp.int32)


def _mixer(x2, w_in_b, b_gate3, conv_w, wco_b, wpool_b, pool_scale, kt, v, wxo_b, wo_b, ln_g, ln_b, wr, br,
           bn, seq):
    n, d = x2.shape
    tm = TM_MIX
    spb = seq // tm
    const2 = lambda b, s: (0, 0)
    const3 = lambda b, s: (0, 0, 0)
    one = pl.Buffered(1)
    row_map = lambda b, s: (b * spb + s, 0)
    return pl.pallas_call(
        _mixer_kernel,
        grid=(bn, spb),
        in_specs=[
            pl.BlockSpec((tm, d), row_map),
            pl.BlockSpec(w_in_b.shape, const2, pipeline_mode=one),
            pl.BlockSpec(b_gate3.shape, const2, pipeline_mode=one),
            pl.BlockSpec(conv_w.shape, const2, pipeline_mode=one),
            pl.BlockSpec(wco_b.shape, const2, pipeline_mode=one),
            pl.BlockSpec(wpool_b.shape, const3, pipeline_mode=one),
            pl.BlockSpec(pool_scale.shape, const2, pipeline_mode=one),
            pl.BlockSpec((1,) + kt.shape[1:], lambda b, s: (b, 0, 0)),
            pl.BlockSpec((1,) + v.shape[1:], lambda b, s: (b, 0, 0)),
            pl.BlockSpec(wxo_b.shape, const2, pipeline_mode=one),
            pl.BlockSpec(wo_b.shape, const2, pipeline_mode=one),
            pl.BlockSpec(ln_g.shape, const2, pipeline_mode=one),
            pl.BlockSpec(ln_b.shape, const2, pipeline_mode=one),
            pl.BlockSpec(wr.shape, const2, pipeline_mode=one),
            pl.BlockSpec(br.shape, const2, pipeline_mode=one),
        ],
        out_specs=[
            pl.BlockSpec((tm * SUB, LANES), row_map),
            pl.BlockSpec((tm, LANES), row_map),
            pl.BlockSpec((1, 8, tm), lambda b, s: (b * spb + s, 0, 0)),
            pl.BlockSpec((8, LANES), const2),
        ],
        out_shape=[
            jax.ShapeDtypeStruct((n * SUB, LANES), F32),
            jax.ShapeDtypeStruct((n, LANES), F32),
            jax.ShapeDtypeStruct((n // tm, 8, tm), jnp.int32),
            jax.ShapeDtypeStruct((8, LANES), jnp.int32),
        ],
        scratch_shapes=[
            pltpu.VMEM((tm + CONV_HALO, d), F32),
            pltpu.VMEM((POOL_PAD + POOL_HALO + tm, d), F32),
            pltpu.VMEM((POOL_PAD + POOL_HALO + tm, d - POOL_GROUP_DIM), F32),
            pltpu.VMEM((POOL_PAD + POOL_HALO + tm, d - 2 * POOL_GROUP_DIM), F32),
            pltpu.VMEM((POOL_PAD + POOL_HALO + tm, d - 3 * POOL_GROUP_DIM), F32),
            pltpu.VMEM((8, LANES), F32),
        ],
        compiler_params=pltpu.CompilerParams(
            dimension_semantics=("arbitrary", "arbitrary"), vmem_limit_bytes=VMEM_LIMIT),
        name="mixer",
    )(x2, w_in_b, b_gate3, conv_w, wco_b, wpool_b, pool_scale, kt, v, wxo_b, wo_b, ln_g, ln_b, wr, br)


def _dest_kernel(ps_ref, ri_ref, dest_ref):
    dest_ref[...] = jnp.zeros(dest_ref.shape, jnp.int32)
    for t in range(dest_ref.shape[0]):
        ri = ri_ref[t]
        e = ri[0:TOP_K, :]
        start = jnp.zeros_like(e)
        for j in range(N_EXPERTS):
            start = jnp.where(e == j, ps_ref[j], start)
        dest_ref[t, 0:TOP_K, :] = start + ri[TOP_K:2 * TOP_K, :]


def _dest(pstarts, route_i):
    nt, rows, tm = route_i.shape
    blk = 8
    spec = pl.BlockSpec((blk, rows, tm), lambda i, ps: (i, 0, 0))
    return pl.pallas_call(
        _dest_kernel,
        grid_spec=pltpu.PrefetchScalarGridSpec(
            num_scalar_prefetch=1, grid=(nt // blk,), in_specs=[spec], out_specs=spec),
        out_shape=jax.ShapeDtypeStruct(route_i.shape, jnp.int32),
        compiler_params=pltpu.CompilerParams(dimension_semantics=("arbitrary",)),
        name="dest_rows",
    )(pstarts, route_i)


def _scatter_kernel(zt_ref, dest_ref, h_ref, wup_ref, wdn_ref, xbuf_ref, wupb_ref, wdnb_ref, zbuf, sem):
    i = pl.program_id(0)
    tm = TM_ROW

    def zero_copy(e):
        start = pl.multiple_of(zt_ref[e] * SUB, TM_EXP * SUB)
        return pltpu.make_async_copy(zbuf, xbuf_ref.at[pl.ds(start, TM_EXP * SUB), :], sem.at[2])

    @pl.when(i == 0)
    def _():
        zbuf[...] = jnp.zeros_like(zbuf)

        def start(e, c):
            @pl.when(zt_ref[e] >= 0)
            def _():
                zero_copy(e).start()
            return c

        def wait(e, c):
            @pl.when(zt_ref[e] >= 0)
            def _():
                zero_copy(e).wait()
            return c

        lax.fori_loop(0, 2 * N_EXPERTS, start, 0)
        lax.fori_loop(0, 2 * N_EXPERTS, wait, 0)

    def issue(r, c):
        for k in range(TOP_K):
            _tile_copy(h_ref, r, xbuf_ref, dest_ref[k * tm + r], sem.at[k]).start(priority=k)
        return c

    lax.fori_loop(0, tm, issue, 0, unroll=8)
    wupb_ref[...] = wup_ref[...].astype(BF16)
    wdnb_ref[...] = wdn_ref[...].astype(BF16)
    for k in range(TOP_K):
        pltpu.make_async_copy(h_ref, xbuf_ref.at[pl.ds(0, tm * SUB), :], sem.at[k]).wait()


def _scatter(zero_tiles, dest_flat, h1t, w_up, w_down, p_rows):
    tm = TM_ROW
    nt = dest_flat.shape[0] // (TOP_K * tm)
    assert nt >= N_EXPERTS, "one expert's weights are converted per grid step"
    w_map = lambda i, zt: (jnp.minimum(i, N_EXPERTS - 1), 0, 0)
    wup_blk = (1,) + w_up.shape[1:]
    wdn_blk = (1,) + w_down.shape[1:]
    return pl.pallas_call(
        _scatter_kernel,
        grid_spec=pltpu.PrefetchScalarGridSpec(
            num_scalar_prefetch=1,
            grid=(nt,),
            in_specs=[
                pl.BlockSpec((TOP_K * tm,), lambda i, zt: (i,), memory_space=pltpu.SMEM),
                pl.BlockSpec((tm * SUB, LANES), lambda i, zt: (i, 0)),
                pl.BlockSpec(wup_blk, w_map),
                pl.BlockSpec(wdn_blk, w_map),
            ],
            out_specs=[
                pl.BlockSpec(memory_space=pl.ANY),
                pl.BlockSpec(wup_blk, w_map),
                pl.BlockSpec(wdn_blk, w_map),
            ],
            scratch_shapes=[pltpu.VMEM((TM_EXP * SUB, LANES), F32), pltpu.SemaphoreType.DMA((3,))],
        ),
        out_shape=[
            jax.ShapeDtypeStruct((p_rows * SUB, LANES), F32),
            jax.ShapeDtypeStruct(w_up.shape, BF16),
            jax.ShapeDtypeStruct(w_down.shape, BF16),
        ],
        compiler_params=pltpu.CompilerParams(dimension_semantics=("arbitrary",), vmem_limit_bytes=VMEM_LIMIT),
        name="scatter_rows",
    )(zero_tiles, dest_flat, h1t, w_up, w_down)


def _expert_kernel(t0_ref, nt_ref, tot_ref, wup_ref, wdn_ref, xbuf_ref, ybuf_ref, xs, ys, sem_in, sem_out):
    e = pl.program_id(0)
    t0 = t0_ref[e]
    nt = nt_ref[e]
    total = tot_ref[0]
    rows = TM_EXP * SUB
    max_tiles = xbuf_ref.shape[0] // rows

    def hbm_tile(ref, g):
        return ref.at[pl.ds(pl.multiple_of(g * rows, rows), rows), :]

    def x_copy(g, slot):
        return pltpu.make_async_copy(hbm_tile(xbuf_ref, g), xs.at[slot], sem_in.at[slot])

    def y_copy(g, slot):
        return pltpu.make_async_copy(ys.at[slot], hbm_tile(ybuf_ref, g), sem_out.at[slot])

    ahead = RING - 1

    @pl.when(e == 0)
    def _():
        for g0 in range(ahead):
            @pl.when(g0 < total)
            def _():
                x_copy(g0, g0).start(priority=ROW_DMA_PRIORITY)

    def tile(j, c):
        g = t0 + j
        slot = lax.rem(g, RING)

        @pl.when(g + ahead < total)
        def _():
            x_copy(g + ahead, lax.rem(g + ahead, RING)).start(priority=ROW_DMA_PRIORITY)

        x_copy(g, slot).wait()

        @pl.when(g >= RING)
        def _():
            y_copy(g - RING, slot).wait()

        xb = _tok_load(xs.at[slot], TM_EXP).astype(BF16)
        hgv = _dot(xb, wup_ref[0])
        hg = hgv[:, :D_EXPERT]
        hv = hgv[:, D_EXPERT:]
        act = (hg * jax.nn.sigmoid(hg)) * hv
        _tok_store(ys.at[slot], _dot(act.astype(BF16), wdn_ref[0]), TM_EXP)
        y_copy(g, slot).start(priority=ROW_DMA_PRIORITY)
        return c

    lax.fori_loop(0, nt, tile, 0)

    @pl.when(e == pl.num_programs(0) - 1)
    def _():
        for back in range(RING, 0, -1):
            @pl.when(total >= back)
            def _():
                y_copy(total - back, lax.rem(total - back, RING)).wait()

        ys[0] = jnp.zeros(ys.shape[1:], F32)

        def zero_tail(g, c):
            y_copy(g, 0).start()
            y_copy(g, 0).wait()
            return c

        lax.fori_loop(total, max_tiles, zero_tail, 0)


def _experts(tile_start, tile_count, n_tiles, xbuf, w_up, w_down):
    w_map = lambda e, t0, nt, tot: (e, 0, 0)
    rows = TM_EXP * SUB
    return pl.pallas_call(
        _expert_kernel,
        grid_spec=pltpu.PrefetchScalarGridSpec(
            num_scalar_prefetch=3,
            grid=(N_EXPERTS,),
            in_specs=[
                pl.BlockSpec((1,) + w_up.shape[1:], w_map),
                pl.BlockSpec((1,) + w_down.shape[1:], w_map),
                pl.BlockSpec(memory_space=pl.ANY),
            ],
            out_specs=pl.BlockSpec(memory_space=pl.ANY),
            scratch_shapes=[
                pltpu.VMEM((RING, rows, LANES), F32), pltpu.VMEM((RING, rows, LANES), F32),
                pltpu.SemaphoreType.DMA((RING,)), pltpu.SemaphoreType.DMA((RING,)),
            ],
        ),
        out_shape=jax.ShapeDtypeStruct(xbuf.shape, F32),
        compiler_params=pltpu.CompilerParams(dimension_semantics=("arbitrary",), vmem_limit_bytes=VMEM_LIMIT),
        name="experts",
    )(tile_start, tile_count, n_tiles, w_up, w_down, xbuf)


def _combine_kernel(dcur_ref, dnext_ref, h_ref, route_ref, ybuf_ref, g_ref, b_ref, out_ref, ybufs, sem):
    tm = TM_ROW
    i = pl.program_id(0)
    slot = lax.rem(i, 2)

    def gather(d_ref, s):
        def issue(r, c):
            for k in range(TOP_K):
                _tile_copy(ybuf_ref, d_ref[k * tm + r], ybufs.at[s, k], r, sem.at[s, k]).start(priority=k)
            return c

        lax.fori_loop(0, tm, issue, 0, unroll=8)

    @pl.when(i == 0)
    def _():
        gather(dcur_ref, 0)

    @pl.when(i + 1 < pl.num_programs(0))
    def _():
        gather(dnext_ref, 1 - slot)

    for k in range(TOP_K):
        pltpu.make_async_copy(ybuf_ref.at[pl.ds(0, tm * SUB), :], ybufs.at[slot, k], sem.at[slot, k]).wait()
    route = route_ref[...]
    ffn = route[:, 4:5] * _tok_load(ybufs.at[slot, 0], tm) + route[:, 5:6] * _tok_load(ybufs.at[slot, 1], tm)
    out_ref[...] = _layer_norm(ALPHA * _tok_load(h_ref, tm) + ffn, g_ref[...], b_ref[...])


def _combine(dest_flat, h1t, route, ybuf, ln_g, ln_b):
    tm = TM_ROW
    nt = dest_flat.shape[0] // (TOP_K * tm)
    n = nt * tm
    return pl.pallas_call(
        _combine_kernel,
        grid=(nt,),
        in_specs=[
            pl.BlockSpec((TOP_K * tm,), lambda i: (i,), memory_space=pltpu.SMEM),
            pl.BlockSpec((TOP_K * tm,), lambda i: (jnp.minimum(i + 1, nt - 1),), memory_space=pltpu.SMEM),
            pl.BlockSpec((tm * SUB, LANES), lambda i: (i, 0)),
            pl.BlockSpec((tm, LANES), lambda i: (i, 0)),
            pl.BlockSpec(memory_space=pl.ANY),
            pl.BlockSpec(ln_g.shape, lambda i: (0, 0)),
            pl.BlockSpec(ln_b.shape, lambda i: (0, 0)),
        ],
        out_specs=pl.BlockSpec((tm, D_MODEL), lambda i: (i, 0)),
        out_shape=jax.ShapeDtypeStruct((n, D_MODEL), F32),
        scratch_shapes=[pltpu.VMEM((2, TOP_K, tm * SUB, LANES), F32), pltpu.SemaphoreType.DMA((2, TOP_K))],
        compiler_params=pltpu.CompilerParams(dimension_semantics=("arbitrary",)),
        name="combine_ln2",
    )(dest_flat, dest_flat, h1t, route, ybuf, ln_g, ln_b)


def _layer(h, mem, w_in, b_gate, conv_w, w_conv_out, w_pool, pool_scale, w_kv, w_xo, w_o, ln1_g, ln1_b,
           w_rg, b_rg, w_re, b_re, w_up, w_down, ln2_g, ln2_b):
    bn, seq, d = h.shape
    n = bn * seq
    xw = X_HEADS * X_HEAD_DIM

    kt, v = _kv_proj(mem, w_kv[:, :xw].T.astype(BF16), w_kv[:, xw:].astype(BF16))

    pad = LANES - N_EXPERTS - N_GROUPS
    w_r = jnp.concatenate([w_re, w_rg, jnp.zeros((d, pad), F32)], axis=1)
    wr_hi = w_r.astype(BF16)
    wr_lo = (w_r - wr_hi.astype(F32)).astype(BF16)
    b_r = jnp.concatenate([b_re, b_rg, jnp.zeros((pad,), F32)])[None, :]

    h1t, route, route_i, cnt = _mixer(
        h.reshape(n, d), w_in.astype(BF16), b_gate.reshape(3, d), conv_w, w_conv_out.astype(BF16),
        w_pool.astype(BF16), pool_scale[None, :], kt, v, w_xo.astype(BF16), w_o.astype(BF16),
        ln1_g[None, :], ln1_b[None, :], jnp.concatenate([wr_hi, wr_lo], axis=1), b_r, bn, seq)

    counts = cnt[0, :N_EXPERTS]
    padded = (counts + TM_EXP - 1) // TM_EXP * TM_EXP
    pends = jnp.cumsum(padded)
    pstarts = pends - padded
    max_tiles = (n * TOP_K + N_EXPERTS * (TM_EXP - 1)) // TM_EXP
    p_rows = max_tiles * TM_EXP
    n_tiles = (pends[-1] // TM_EXP).astype(jnp.int32)
    tail_ids = n_tiles + jnp.arange(N_EXPERTS, dtype=jnp.int32)
    zero_tiles = jnp.concatenate([
        jnp.where(counts % TM_EXP != 0, pends - TM_EXP, -1),
        jnp.where(tail_ids < max_tiles, tail_ids * TM_EXP, -1)]).astype(jnp.int32)

    dest = _dest(pstarts.astype(jnp.int32), route_i)[:, :TOP_K, :].reshape(-1)
    xbuf, w_up_b, w_down_b = _scatter(zero_tiles, dest, h1t, w_up, w_down, p_rows)
    ybuf = _experts((pstarts // TM_EXP).astype(jnp.int32), (padded // TM_EXP).astype(jnp.int32), n_tiles.reshape(1),
                    xbuf, w_up_b, w_down_b)
    out = _combine(dest, h1t, route, ybuf, ln2_g[None, :], ln2_b[None, :])
    return out.reshape(bn, seq, d)


def kernel(x, mem, w_in, b_gate, conv_w, w_conv_out, w_pool, pool_scale, w_kv, w_xo, w_o, ln1_g, ln1_b,
           w_router_group, b_router_group, w_router_expert, b_router_expert, w_up, w_down, ln2_g, ln2_b):
    h = x
    for l in range(DEPTH):
        h = _layer(h, mem, w_in[l], b_gate[l], conv_w[l], w_conv_out[l], w_pool[l], pool_scale[l], w_kv[l],
                   w_xo[l], w_o[l], ln1_g[l], ln1_b[l], w_router_group[l], b_router_group[l],
                   w_router_expert[l], b_router_expert[l], w_up[l], w_down[l], ln2_g[l], ln2_b[l])
    return h
```

```python
import functools

import jax
import jax.numpy as jnp
from jax import lax
from jax.experimental import pallas as pl
from jax.experimental.pallas import tpu as pltpu

D_MODEL = 1024
CONV_K = 3
POOL_WINDOWS = (2, 4, 8, 16)
POOL_GROUP_DIM = 256
X_HEADS = 4
X_HEAD_DIM = 256
N_GROUPS = 8
EXPERTS_PER_GROUP = 8
N_EXPERTS = 64
TOP_K = 2
D_EXPERT = 512
DEPTH = 1
ALPHA = (2.0 * DEPTH) ** 0.25
LN_EPS = 1e-5

LANES = 128
SUB = 8
assert D_MODEL == SUB * LANES
POOL_HALO = 16
POOL_PAD = 8
assert POOL_WINDOWS[0] == 2 and all(b == 2 * a for a, b in zip(POOL_WINDOWS, POOL_WINDOWS[1:]))
assert POOL_PAD >= POOL_WINDOWS[-2] and POOL_HALO >= POOL_WINDOWS[-1] and len(POOL_WINDOWS) == 4
CONV_HALO = 8
TM_MIX = 512
TM_EXP = 256
TM_ROW = TM_MIX
VMEM_LIMIT = 58 * 1024 * 1024
SCATTER_RING = 3
RING = 4
ROW_DMA_PRIORITY = 1

F32 = jnp.float32
BF16 = jnp.bfloat16


def _dot(a, b):
    return jnp.dot(a, b, preferred_element_type=F32)


def _tok_load(ref, tm):
    return jnp.concatenate([ref[pl.ds(c, tm, stride=SUB), :] for c in range(SUB)], axis=1)


def _tok_store(ref, val, tm):
    for c in range(SUB):
        ref[pl.ds(c, tm, stride=SUB), :] = val[:, c * LANES:(c + 1) * LANES]


def _tile_copy(src, src_tok, dst, dst_tok, sem):
    s = pl.multiple_of(src_tok * SUB, SUB)
    t = pl.multiple_of(dst_tok * SUB, SUB)
    return pltpu.make_async_copy(src.at[pl.ds(s, SUB), :], dst.at[pl.ds(t, SUB), :], sem)


def _layer_norm(h, g, b):
    mu = jnp.mean(h, axis=-1, keepdims=True)
    c = h - mu
    var = jnp.mean(c * c, axis=-1, keepdims=True)
    return c * lax.rsqrt(var + LN_EPS) * g + b


def _kv_kernel(mem_ref, wkt_ref, wv_ref, kt_ref, v_ref):
    mb = mem_ref[0].astype(BF16)
    kt = lax.dot_general(wkt_ref[...], mb, (((1,), (1,)), ((), ())), preferred_element_type=F32)
    kt_ref[0] = kt.astype(BF16)
    v_ref[0] = _dot(mb, wv_ref[...]).astype(BF16)


def _kv_proj(mem, wkt, wv):
    bn, m, d = mem.shape
    return pl.pallas_call(
        _kv_kernel,
        grid=(bn,),
        in_specs=[
            pl.BlockSpec((1, m, d), lambda b: (b, 0, 0)),
            pl.BlockSpec((d, d), lambda b: (0, 0)),
            pl.BlockSpec((d, d), lambda b: (0, 0)),
        ],
        out_specs=[
            pl.BlockSpec((1, d, m), lambda b: (b, 0, 0)),
            pl.BlockSpec((1, m, d), lambda b: (b, 0, 0)),
        ],
        out_shape=[jax.ShapeDtypeStruct((bn, d, m), BF16), jax.ShapeDtypeStruct((bn, m, d), BF16)],
        compiler_params=pltpu.CompilerParams(dimension_semantics=("arbitrary",)),
        name="kv_proj",
    )(mem, wkt, wv)


def _mixer_kernel(x_ref, win_ref, bg_ref, cw_ref, wco_ref, wpool_ref, ps_ref, kt_ref, v_ref, wxo_ref, wo_ref,
                  g1_ref, b1_ref, wr_ref, br_ref,
                  h1_ref, route_ref, ri_ref, cnt_ref,
                  ubuf, pbuf, s1buf, s2buf, s3buf, carry):
    tm = TM_MIX
    d = D_MODEL
    b = pl.program_id(0)
    s = pl.program_id(1)

    @pl.when(s == 0)
    def _():
        ubuf[0:CONV_HALO, :] = jnp.zeros((CONV_HALO, d), F32)
        pbuf[0:POOL_PAD + POOL_HALO, :] = jnp.zeros((POOL_PAD + POOL_HALO, d), F32)
        for sbuf in (s1buf, s2buf, s3buf):
            sbuf[0:POOL_PAD, :] = jnp.zeros((POOL_PAD, sbuf.shape[1]), F32)

    @pl.when((s == 0) & (b == 0))
    def _():
        carry[...] = jnp.zeros_like(carry)

    x = x_ref[...]
    xb = x.astype(BF16)

    def proj(sec):
        return _dot(xb, win_ref[:, sec * d:(sec + 1) * d])

    def gate(i):
        return jax.nn.sigmoid(proj(5 + i) + bg_ref[i:i + 1, :])

    u = proj(1) * proj(2)
    ubuf[CONV_HALO:CONV_HALO + tm, :] = u
    v = cw_ref[CONV_K - 1:CONV_K, :] * u
    for k in range(CONV_K - 1):
        off = CONV_HALO - (CONV_K - 1) + k
        v = v + cw_ref[k:k + 1, :] * ubuf[off:off + tm, :]
    ubuf[0:CONV_HALO, :] = ubuf[tm:tm + CONV_HALO, :]
    y_conv = _dot((proj(0) * v).astype(BF16), wco_ref[...])
    acc = gate(0) * y_conv

    p = proj(3)
    base = POOL_PAD + POOL_HALO
    n = POOL_HALO + tm
    pbuf[base:base + tm, :] = p
    lvl = pbuf[POOL_PAD:POOL_PAD + n, :] + pbuf[POOL_PAD - 1:POOL_PAD - 1 + n, :]
    sums = [lvl[POOL_HALO:, :POOL_GROUP_DIM]]
    for gi, sbuf in enumerate((s1buf, s2buf, s3buf), start=1):
        shift = POOL_WINDOWS[gi - 1]
        rest = lvl[:, POOL_GROUP_DIM:]
        sbuf[POOL_PAD:POOL_PAD + n, :] = rest
        lvl = rest + sbuf[POOL_PAD - shift:POOL_PAD - shift + n, :]
        sums.append(lvl[POOL_HALO:, :POOL_GROUP_DIM])
    pos = (s * tm + 1 + lax.broadcasted_iota(jnp.int32, (tm, 1), 0)).astype(F32)
    ys = []
    for gi, w in enumerate(POOL_WINDOWS):
        c0 = gi * POOL_GROUP_DIM
        inv_cnt = 1.0 / jnp.minimum(pos, float(w))
        dg = (sums[gi] * inv_cnt - p[:, c0:c0 + POOL_GROUP_DIM]).astype(BF16)
        ys.append(_dot(dg, wpool_ref[gi]))
    pbuf[POOL_PAD:base, :] = pbuf[POOL_PAD + tm:base + tm, :]
    y_pool = jnp.concatenate(ys, axis=1) * ps_ref[...]
    acc = acc + gate(1) * y_pool

    qb = proj(4).astype(BF16)
    scale = X_HEAD_DIM ** -0.5
    os_ = []
    for h in range(X_HEADS):
        c0 = h * X_HEAD_DIM
        c1 = c0 + X_HEAD_DIM
        sc = _dot(qb[:, c0:c1], kt_ref[0, c0:c1, :]) * scale
        e = jnp.exp(sc - jnp.max(sc, axis=-1, keepdims=True))
        a = e * (1.0 / jnp.sum(e, axis=-1, keepdims=True))
        os_.append(_dot(a.astype(BF16), v_ref[0, :, c0:c1]))
    y_mem = _dot(jnp.concatenate(os_, axis=1).astype(BF16), wxo_ref[...])
    acc = acc + gate(2) * y_mem

    mix = _dot(acc.astype(BF16), wo_ref[...])
    h1 = _layer_norm(ALPHA * x + mix, g1_ref[...], b1_ref[...])
    _tok_store(h1_ref, h1, tm)

    hh = h1.astype(BF16)
    hl = (h1 - hh.astype(F32)).astype(BF16)
    wrh = wr_ref[:, :LANES]
    logits = _dot(hh, wrh) + _dot(hl, wrh) + _dot(hh, wr_ref[:, LANES:]) + br_ref[...]
    lane = lax.broadcasted_iota(jnp.int32, (tm, LANES), 1)
    lane_f = lane.astype(F32)
    neg = jnp.float32(-jnp.inf)
    big = jnp.float32(1e9)

    def first_argmax(vals):
        m = jnp.max(vals, axis=-1, keepdims=True)
        idx = jnp.min(jnp.where(vals == m, lane_f, big), axis=-1, keepdims=True)
        return m, idx

    is_g = (lane >= N_EXPERTS) & (lane < N_EXPERTS + N_GROUPS)
    gmax, gidx = first_argmax(jnp.where(is_g, logits, neg))
    g_w = 1.0 / jnp.sum(jnp.where(is_g, jnp.exp(logits - gmax), 0.0), axis=-1, keepdims=True)
    gsel = gidx.astype(jnp.int32) - N_EXPERTS
    in_grp = (lane >> 3) == gsel
    le = jnp.where(in_grp, logits, neg)
    m1, i1 = first_argmax(le)
    m2, i2 = first_argmax(jnp.where(lane_f == i1, neg, le))
    t = jnp.exp(m2 - m1)
    den = 1.0 + t
    w1 = g_w / den
    w2 = g_w * t / den

    sel1 = lane_f == i1
    sel2 = lane_f == i2
    onehot = (sel1 | sel2).astype(BF16)
    row_i = lax.broadcasted_iota(jnp.int32, (tm, tm), 0)
    col_i = lax.broadcasted_iota(jnp.int32, (tm, tm), 1)
    before = (col_i < row_i).astype(BF16)
    prior = _dot(before, onehot) + carry[0:1, :]
    r1 = jnp.sum(jnp.where(sel1, prior, 0.0), axis=-1, keepdims=True)
    r2 = jnp.sum(jnp.where(sel2, prior, 0.0), axis=-1, keepdims=True)
    carry[...] = carry[...] + jnp.sum(onehot.astype(F32), axis=0, keepdims=True)
    cnt_ref[...] = carry[...].astype(jnp.int32)

    route = jnp.where(lane == 0, i1, 0.0)
    for k, col in enumerate((i2, r1, r2, w1, w2), start=1):
        route = jnp.where(lane == k, col, route)
    route_ref[...] = route
    ri_ref[0] = route.T[0:8, :].astype(jnp.int32)


def _mixer(x2, w_in_b, b_gate3, conv_w, wco_b, wpool_b, pool_scale, kt, v, wxo_b, wo_b, ln_g, ln_b, wr, br,
           bn, seq):
    n, d = x2.shape
    tm = TM_MIX
    spb = seq // tm
    const2 = lambda b, s: (0, 0)
    const3 = lambda b, s: (0, 0, 0)
    one = pl.Buffered(1)
    row_map = lambda b, s: (b * spb + s, 0)
    return pl.pallas_call(
        _mixer_kernel,
        grid=(bn, spb),
        in_specs=[
            pl.BlockSpec((tm, d), row_map),
            pl.BlockSpec(w_in_b.shape, const2, pipeline_mode=one),
            pl.BlockSpec(b_gate3.shape, const2, pipeline_mode=one),
            pl.BlockSpec(conv_w.shape, const2, pipeline_mode=one),
            pl.BlockSpec(wco_b.shape, const2, pipeline_mode=one),
            pl.BlockSpec(wpool_b.shape, const3, pipeline_mode=one),
            pl.BlockSpec(pool_scale.shape, const2, pipeline_mode=one),
            pl.BlockSpec((1,) + kt.shape[1:], lambda b, s: (b, 0, 0)),
            pl.BlockSpec((1,) + v.shape[1:], lambda b, s: (b, 0, 0)),
            pl.BlockSpec(wxo_b.shape, const2, pipeline_mode=one),
            pl.BlockSpec(wo_b.shape, const2, pipeline_mode=one),
            pl.BlockSpec(ln_g.shape, const2, pipeline_mode=one),
            pl.BlockSpec(ln_b.shape, const2, pipeline_mode=one),
            pl.BlockSpec(wr.shape, const2, pipeline_mode=one),
            pl.BlockSpec(br.shape, const2, pipeline_mode=one),
        ],
        out_specs=[
            pl.BlockSpec((tm * SUB, LANES), row_map),
            pl.BlockSpec((tm, LANES), row_map),
            pl.BlockSpec((1, 8, tm), lambda b, s: (b * spb + s, 0, 0)),
            pl.BlockSpec((8, LANES), const2),
        ],
        out_shape=[
            jax.ShapeDtypeStruct((n * SUB, LANES), F32),
            jax.ShapeDtypeStruct((n, LANES), F32),
            jax.ShapeDtypeStruct((n // tm, 8, tm), jnp.int32),
            jax.ShapeDtypeStruct((8, LANES), jnp.int32),
        ],
        scratch_shapes=[
            pltpu.VMEM((tm + CONV_HALO, d), F32),
            pltpu.VMEM((POOL_PAD + POOL_HALO + tm, d), F32),
            pltpu.VMEM((POOL_PAD + POOL_HALO + tm, d - POOL_GROUP_DIM), F32),
            pltpu.VMEM((POOL_PAD + POOL_HALO + tm, d - 2 * POOL_GROUP_DIM), F32),
            pltpu.VMEM((POOL_PAD + POOL_HALO + tm, d - 3 * POOL_GROUP_DIM), F32),
            pltpu.VMEM((8, LANES), F32),
        ],
        compiler_params=pltpu.CompilerParams(
            dimension_semantics=("arbitrary", "arbitrary"), vmem_limit_bytes=VMEM_LIMIT),
        name="mixer",
    )(x2, w_in_b, b_gate3, conv_w, wco_b, wpool_b, pool_scale, kt, v, wxo_b, wo_b, ln_g, ln_b, wr, br)


def _dest_kernel(ps_ref, ri_ref, dest_ref):
    dest_ref[...] = jnp.zeros(dest_ref.shape, jnp.int32)
    for t in range(dest_ref.shape[0]):
        ri = ri_ref[t]
        e = ri[0:TOP_K, :]
        start = jnp.zeros_like(e)
        for j in range(N_EXPERTS):
            start = jnp.where(e == j, ps_ref[j], start)
        dest_ref[t, 0:TOP_K, :] = start + ri[TOP_K:2 * TOP_K, :]


def _dest(pstarts, route_i):
    nt, rows, tm = route_i.shape
    blk = 8
    spec = pl.BlockSpec((blk, rows, tm), lambda i, ps: (i, 0, 0))
    return pl.pallas_call(
        _dest_kernel,
        grid_spec=pltpu.PrefetchScalarGridSpec(
            num_scalar_prefetch=1, grid=(nt // blk,), in_specs=[spec], out_specs=spec),
        out_shape=jax.ShapeDtypeStruct(route_i.shape, jnp.int32),
        compiler_params=pltpu.CompilerParams(dimension_semantics=("arbitrary",)),
        name="dest_rows",
    )(pstarts, route_i)


def _scatter_kernel(zt_ref, dest_ref, h_ref, wup_ref, wdn_ref, xbuf_ref, wupb_ref, wdnb_ref,
                    zbuf, stage, zsem, fsem, ssem):
    i = pl.program_id(0)
    nsteps = pl.num_programs(0)
    tm = TM_ROW
    rows = tm * SUB
    slot = lax.rem(i, SCATTER_RING)

    def zero_copy(e):
        start = pl.multiple_of(zt_ref[e] * SUB, TM_EXP * SUB)
        return pltpu.make_async_copy(zbuf, xbuf_ref.at[pl.ds(start, TM_EXP * SUB), :], zsem.at[0])

    def fetch(t, s):
        src = h_ref.at[pl.ds(pl.multiple_of(t * rows, rows), rows), :]
        return pltpu.make_async_copy(src, stage.at[s], fsem.at[s])

    def drain(s):
        for k in range(TOP_K):
            pltpu.make_async_copy(stage.at[s], xbuf_ref.at[pl.ds(0, rows), :], ssem.at[s, k]).wait()

    @pl.when(i == 0)
    def _():
        zbuf[...] = jnp.zeros_like(zbuf)

        def start(e, c):
            @pl.when(zt_ref[e] >= 0)
            def _():
                zero_copy(e).start()
            return c

        def wait(e, c):
            @pl.when(zt_ref[e] >= 0)
            def _():
                zero_copy(e).wait()
            return c

        lax.fori_loop(0, 2 * N_EXPERTS, start, 0)
        fetch(0, 0).start()
        lax.fori_loop(0, 2 * N_EXPERTS, wait, 0)

    @pl.when(i >= 2)
    def _():
        drain(lax.rem(i + 1, SCATTER_RING))

    @pl.when(i + 1 < nsteps)
    def _():
        fetch(i + 1, lax.rem(i + 1, SCATTER_RING)).start()

    fetch(i, slot).wait()

    def issue(r, c):
        for k in range(TOP_K):
            _tile_copy(stage.at[slot], r, xbuf_ref, dest_ref[k * tm + r], ssem.at[slot, k]).start(priority=k)
        return c

    lax.fori_loop(0, tm, issue, 0, unroll=8)
    wupb_ref[...] = wup_ref[...].astype(BF16)
    wdnb_ref[...] = wdn_ref[...].astype(BF16)

    @pl.when(i == nsteps - 1)
    def _():
        @pl.when(i >= 1)
        def _():
            drain(lax.rem(i + 2, SCATTER_RING))

        drain(slot)


def _scatter(zero_tiles, dest_flat, h1t, w_up, w_down, p_rows):
    tm = TM_ROW
    nt = dest_flat.shape[0] // (TOP_K * tm)
    assert nt >= N_EXPERTS, "one expert's weights are converted per grid step"
    w_map = lambda i, zt: (jnp.minimum(i, N_EXPERTS - 1), 0, 0)
    wup_blk = (1,) + w_up.shape[1:]
    wdn_blk = (1,) + w_down.shape[1:]
    return pl.pallas_call(
        _scatter_kernel,
        grid_spec=pltpu.PrefetchScalarGridSpec(
            num_scalar_prefetch=1,
            grid=(nt,),
            in_specs=[
                pl.BlockSpec((TOP_K * tm,), lambda i, zt: (i,), memory_space=pltpu.SMEM),
                pl.BlockSpec(memory_space=pl.ANY),
                pl.BlockSpec(wup_blk, w_map),
                pl.BlockSpec(wdn_blk, w_map),
            ],
            out_specs=[
                pl.BlockSpec(memory_space=pl.ANY),
                pl.BlockSpec(wup_blk, w_map),
                pl.BlockSpec(wdn_blk, w_map),
            ],
            scratch_shapes=[
                pltpu.VMEM((TM_EXP * SUB, LANES), F32),
                pltpu.VMEM((SCATTER_RING, tm * SUB, LANES), F32),
                pltpu.SemaphoreType.DMA((1,)),
                pltpu.SemaphoreType.DMA((SCATTER_RING,)),
                pltpu.SemaphoreType.DMA((SCATTER_RING, TOP_K)),
            ],
        ),
        out_shape=[
            jax.ShapeDtypeStruct((p_rows * SUB, LANES), F32),
            jax.ShapeDtypeStruct(w_up.shape, BF16),
            jax.ShapeDtypeStruct(w_down.shape, BF16),
        ],
        compiler_params=pltpu.CompilerParams(dimension_semantics=("arbitrary",), vmem_limit_bytes=VMEM_LIMIT),
        name="scatter_rows",
    )(zero_tiles, dest_flat, h1t, w_up, w_down)


def _expert_kernel(t0_ref, nt_ref, tot_ref, wup_ref, wdn_ref, xbuf_ref, ybuf_ref, xs, ys, sem_in, sem_out):
    e = pl.program_id(0)
    t0 = t0_ref[e]
    nt = nt_ref[e]
    total = tot_ref[0]
    rows = TM_EXP * SUB
    max_tiles = xbuf_ref.shape[0] // rows

    def hbm_tile(ref, g):
        return ref.at[pl.ds(pl.multiple_of(g * rows, rows), rows), :]

    def x_copy(g, slot):
        return pltpu.make_async_copy(hbm_tile(xbuf_ref, g), xs.at[slot], sem_in.at[slot])

    def y_copy(g, slot):
        return pltpu.make_async_copy(ys.at[slot], hbm_tile(ybuf_ref, g), sem_out.at[slot])

    ahead = RING - 1

    @pl.when(e == 0)
    def _():
        for g0 in range(ahead):
            @pl.when(g0 < total)
            def _():
                x_copy(g0, g0).start(priority=ROW_DMA_PRIORITY)

    def tile(j, c):
        g = t0 + j
        slot = lax.rem(g, RING)

        @pl.when(g + ahead < total)
        def _():
            x_copy(g + ahead, lax.rem(g + ahead, RING)).start(priority=ROW_DMA_PRIORITY)

        x_copy(g, slot).wait()

        @pl.when(g >= RING)
        def _():
            y_copy(g - RING, slot).wait()

        xb = _tok_load(xs.at[slot], TM_EXP).astype(BF16)
        hgv = _dot(xb, wup_ref[0])
        hg = hgv[:, :D_EXPERT]
        hv = hgv[:, D_EXPERT:]
        act = (hg * jax.nn.sigmoid(hg)) * hv
        _tok_store(ys.at[slot], _dot(act.astype(BF16), wdn_ref[0]), TM_EXP)
        y_copy(g, slot).start(priority=ROW_DMA_PRIORITY)
        return c

    lax.fori_loop(0, nt, tile, 0)

    @pl.when(e == pl.num_programs(0) - 1)
    def _():
        for back in range(RING, 0, -1):
            @pl.when(total >= back)
            def _():
                y_copy(total - back, lax.rem(total - back, RING)).wait()

        ys[0] = jnp.zeros(ys.shape[1:], F32)

        def zero_tail(g, c):
            y_copy(g, 0).start()
            y_copy(g, 0).wait()
            return c

        lax.fori_loop(total, max_tiles, zero_tail, 0)


def _experts(tile_start, tile_count, n_tiles, xbuf, w_up, w_down):
    w_map = lambda e, t0, nt, tot: (e, 0, 0)
    rows = TM_EXP * SUB
    return pl.pallas_call(
        _expert_kernel,
        grid_spec=pltpu.PrefetchScalarGridSpec(
            num_scalar_prefetch=3,
            grid=(N_EXPERTS,),
            in_specs=[
                pl.BlockSpec((1,) + w_up.shape[1:], w_map),
                pl.BlockSpec((1,) + w_down.shape[1:], w_map),
                pl.BlockSpec(memory_space=pl.ANY),
            ],
            out_specs=pl.BlockSpec(memory_space=pl.ANY),
            scratch_shapes=[
                pltpu.VMEM((RING, rows, LANES), F32), pltpu.VMEM((RING, rows, LANES), F32),
                pltpu.SemaphoreType.DMA((RING,)), pltpu.SemaphoreType.DMA((RING,)),
            ],
        ),
        out_shape=jax.ShapeDtypeStruct(xbuf.shape, F32),
        compiler_params=pltpu.CompilerParams(dimension_semantics=("arbitrary",), vmem_limit_bytes=VMEM_LIMIT),
        name="experts",
    )(tile_start, tile_count, n_tiles, w_up, w_down, xbuf)


def _combine_kernel(dcur_ref, dnext_ref, h_ref, route_ref, ybuf_ref, g_ref, b_ref, out_ref, ybufs, sem):
    tm = TM_ROW
    i = pl.program_id(0)
    slot = lax.rem(i, 2)

    def gather(d_ref, s):
        def issue(r, c):
            for k in range(TOP_K):
                _tile_copy(ybuf_ref, d_ref[k * tm + r], ybufs.at[s, k], r, sem.at[s, k]).start(priority=k)
            return c

        lax.fori_loop(0, tm, issue, 0, unroll=8)

    @pl.when(i == 0)
    def _():
        gather(dcur_ref, 0)

    @pl.when(i + 1 < pl.num_programs(0))
    def _():
        gather(dnext_ref, 1 - slot)

    for k in range(TOP_K):
        pltpu.make_async_copy(ybuf_ref.at[pl.ds(0, tm * SUB), :], ybufs.at[slot, k], sem.at[slot, k]).wait()
    route = route_ref[...]
    ffn = route[:, 4:5] * _tok_load(ybufs.at[slot, 0], tm) + route[:, 5:6] * _tok_load(ybufs.at[slot, 1], tm)
    out_ref[...] = _layer_norm(ALPHA * _tok_load(h_ref, tm) + ffn, g_ref[...], b_ref[...])


def _combine(dest_flat, h1t, route, ybuf, ln_g, ln_b):
    tm = TM_ROW
    nt = dest_flat.shape[0] // (TOP_K * tm)
    n = nt * tm
    return pl.pallas_call(
        _combine_kernel,
        grid=(nt,),
        in_specs=[
            pl.BlockSpec((TOP_K * tm,), lambda i: (i,), memory_space=pltpu.SMEM),
            pl.BlockSpec((TOP_K * tm,), lambda i: (jnp.minimum(i + 1, nt - 1),), memory_space=pltpu.SMEM),
            pl.BlockSpec((tm * SUB, LANES), lambda i: (i, 0)),
            pl.BlockSpec((tm, LANES), lambda i: (i, 0)),
            pl.BlockSpec(memory_space=pl.ANY),
            pl.BlockSpec(ln_g.shape, lambda i: (0, 0)),
            pl.BlockSpec(ln_b.shape, lambda i: (0, 0)),
        ],
        out_specs=pl.BlockSpec((tm, D_MODEL), lambda i: (i, 0)),
        out_shape=jax.ShapeDtypeStruct((n, D_MODEL), F32),
        scratch_shapes=[pltpu.VMEM((2, TOP_K, tm * SUB, LANES), F32), pltpu.SemaphoreType.DMA((2, TOP_K))],
        compiler_params=pltpu.CompilerParams(dimension_semantics=("arbitrary",)),
        name="combine_ln2",
    )(dest_flat, dest_flat, h1t, route, ybuf, ln_g, ln_b)


def _layer(h, mem, w_in, b_gate, conv_w, w_conv_out, w_pool, pool_scale, w_kv, w_xo, w_o, ln1_g, ln1_b,
           w_rg, b_rg, w_re, b_re, w_up, w_down, ln2_g, ln2_b):
    bn, seq, d = h.shape
    n = bn * seq
    xw = X_HEADS * X_HEAD_DIM

    kt, v = _kv_proj(mem, w_kv[:, :xw].T.astype(BF16), w_kv[:, xw:].astype(BF16))

    pad = LANES - N_EXPERTS - N_GROUPS
    w_r = jnp.concatenate([w_re, w_rg, jnp.zeros((d, pad), F32)], axis=1)
    wr_hi = w_r.astype(BF16)
    wr_lo = (w_r - wr_hi.astype(F32)).astype(BF16)
    b_r = jnp.concatenate([b_re, b_rg, jnp.zeros((pad,), F32)])[None, :]

    h1t, route, route_i, cnt = _mixer(
        h.reshape(n, d), w_in.astype(BF16), b_gate.reshape(3, d), conv_w, w_conv_out.astype(BF16),
        w_pool.astype(BF16), pool_scale[None, :], kt, v, w_xo.astype(BF16), w_o.astype(BF16),
        ln1_g[None, :], ln1_b[None, :], jnp.concatenate([wr_hi, wr_lo], axis=1), b_r, bn, seq)

    counts = cnt[0, :N_EXPERTS]
    padded = (counts + TM_EXP - 1) // TM_EXP * TM_EXP
    pends = jnp.cumsum(padded)
    pstarts = pends - padded
    max_tiles = (n * TOP_K + N_EXPERTS * (TM_EXP - 1)) // TM_EXP
    p_rows = max_tiles * TM_EXP
    n_tiles = (pends[-1] // TM_EXP).astype(jnp.int32)
    tail_ids = n_tiles + jnp.arange(N_EXPERTS, dtype=jnp.int32)
    zero_tiles = jnp.concatenate([
        jnp.where(counts % TM_EXP != 0, pends - TM_EXP, -1),
        jnp.where(tail_ids < max_tiles, tail_ids * TM_EXP, -1)]).astype(jnp.int32)

    dest = _dest(pstarts.astype(jnp.int32), route_i)[:, :TOP_K, :].reshape(-1)
    xbuf, w_up_b, w_down_b = _scatter(zero_tiles, dest, h1t, w_up, w_down, p_rows)
    ybuf = _experts((pstarts // TM_EXP).astype(jnp.int32), (padded // TM_EXP).astype(jnp.int32), n_tiles.reshape(1),
                    xbuf, w_up_b, w_down_b)
    out = _combine(dest, h1t, route, ybuf, ln2_g[None, :], ln2_b[None, :])
    return out.reshape(bn, seq, d)


def kernel(x, mem, w_in, b_gate, conv_w, w_conv_out, w_pool, pool_scale, w_kv, w_xo, w_o, ln1_g, ln1_b,
           w_router_group, b_router_group, w_router_expert, b_router_expert, w_up, w_down, ln2_g, ln2_b):
    h = x
    for l in range(DEPTH):
        h = _layer(h, mem, w_in[l], b_gate[l], conv_w[l], w_conv_out[l], w_pool[l], pool_scale[l], w_kv[l],
                   w_xo[l], w_o[l], ln1_g[l], ln1_b[l], w_router_group[l], b_router_group[l],
                   w_router_expert[l], b_router_expert[l], w_up[l], w_down[l], ln2_g[l], ln2_b[l])
    return h
```

```python
import functools

import jax
import jax.numpy as jnp
from jax import lax
from jax.experimental import pallas as pl
from jax.experimental.pallas import tpu as pltpu

D_MODEL = 1024
CONV_K = 3
POOL_WINDOWS = (2, 4, 8, 16)
POOL_GROUP_DIM = 256
X_HEADS = 4
X_HEAD_DIM = 256
N_GROUPS = 8
EXPERTS_PER_GROUP = 8
N_EXPERTS = 64
TOP_K = 2
D_EXPERT = 512
DEPTH = 1
ALPHA = (2.0 * DEPTH) ** 0.25
LN_EPS = 1e-5

LANES = 128
SUB = 8
assert D_MODEL == SUB * LANES
POOL_HALO = 16
POOL_PAD = 8
assert POOL_WINDOWS[0] == 2 and all(b == 2 * a for a, b in zip(POOL_WINDOWS, POOL_WINDOWS[1:]))
assert POOL_PAD >= POOL_WINDOWS[-2] and POOL_HALO >= POOL_WINDOWS[-1] and len(POOL_WINDOWS) == 4
CONV_HALO = 8
TM_MIX = 512
TM_EXP = 256
TM_ROW = TM_MIX
VMEM_LIMIT = 58 * 1024 * 1024
SCATTER_RING = 3
RING = 4
ROW_DMA_PRIORITY = 1

F32 = jnp.float32
BF16 = jnp.bfloat16


def _dot(a, b):
    return jnp.dot(a, b, preferred_element_type=F32)


def _tok_load(ref, tm):
    return jnp.concatenate([ref[pl.ds(c, tm, stride=SUB), :] for c in range(SUB)], axis=1)


def _tok_store(ref, val, tm):
    for c in range(SUB):
        ref[pl.ds(c, tm, stride=SUB), :] = val[:, c * LANES:(c + 1) * LANES]


def _tile_copy(src, src_tok, dst, dst_tok, sem):
    s = pl.multiple_of(src_tok * SUB, SUB)
    t = pl.multiple_of(dst_tok * SUB, SUB)
    return pltpu.make_async_copy(src.at[pl.ds(s, SUB), :], dst.at[pl.ds(t, SUB), :], sem)


def _layer_norm(h, g, b):
    mu = jnp.mean(h, axis=-1, keepdims=True)
    c = h - mu
    var = jnp.mean(c * c, axis=-1, keepdims=True)
    return c * lax.rsqrt(var + LN_EPS) * g + b


def _kv_kernel(mem_ref, wkt_ref, wv_ref, kt_ref, v_ref):
    mb = mem_ref[0].astype(BF16)
    kt = lax.dot_general(wkt_ref[...], mb, (((1,), (1,)), ((), ())), preferred_element_type=F32)
    kt_ref[0] = kt.astype(BF16)
    v_ref[0] = _dot(mb, wv_ref[...]).astype(BF16)


def _kv_proj(mem, wkt, wv):
    bn, m, d = mem.shape
    return pl.pallas_call(
        _kv_kernel,
        grid=(bn,),
        in_specs=[
            pl.BlockSpec((1, m, d), lambda b: (b, 0, 0)),
            pl.BlockSpec((d, d), lambda b: (0, 0)),
            pl.BlockSpec((d, d), lambda b: (0, 0)),
        ],
        out_specs=[
            pl.BlockSpec((1, d, m), lambda b: (b, 0, 0)),
            pl.BlockSpec((1, m, d), lambda b: (b, 0, 0)),
        ],
        out_shape=[jax.ShapeDtypeStruct((bn, d, m), BF16), jax.ShapeDtypeStruct((bn, m, d), BF16)],
        compiler_params=pltpu.CompilerParams(dimension_semantics=("arbitrary",)),
        name="kv_proj",
    )(mem, wkt, wv)


def _mixer_kernel(x_ref, win_ref, bg_ref, cw_ref, wco_ref, wpool_ref, ps_ref, kt_ref, v_ref, wxo_ref, wo_ref,
                  g1_ref, b1_ref, wr_ref, br_ref,
                  h1_ref, route_ref, ri_ref, cnt_ref,
                  ubuf, pbuf, s1buf, s2buf, s3buf, carry):
    tm = TM_MIX
    d = D_MODEL
    b = pl.program_id(0)
    s = pl.program_id(1)

    @pl.when(s == 0)
    def _():
        ubuf[0:CONV_HALO, :] = jnp.zeros((CONV_HALO, d), F32)
        pbuf[0:POOL_PAD + POOL_HALO, :] = jnp.zeros((POOL_PAD + POOL_HALO, d), F32)
        for sbuf in (s1buf, s2buf, s3buf):
            sbuf[0:POOL_PAD, :] = jnp.zeros((POOL_PAD, sbuf.shape[1]), F32)

    @pl.when((s == 0) & (b == 0))
    def _():
        carry[...] = jnp.zeros_like(carry)

    x = x_ref[...]
    xb = x.astype(BF16)

    def proj(sec):
        return _dot(xb, win_ref[:, sec * d:(sec + 1) * d])

    def gate(i):
        return jax.nn.sigmoid(proj(5 + i) + bg_ref[i:i + 1, :])

    u = proj(1) * proj(2)
    ubuf[CONV_HALO:CONV_HALO + tm, :] = u
    v = cw_ref[CONV_K - 1:CONV_K, :] * u
    for k in range(CONV_K - 1):
        off = CONV_HALO - (CONV_K - 1) + k
        v = v + cw_ref[k:k + 1, :] * ubuf[off:off + tm, :]
    ubuf[0:CONV_HALO, :] = ubuf[tm:tm + CONV_HALO, :]
    y_conv = _dot((proj(0) * v).astype(BF16), wco_ref[...])
    acc = gate(0) * y_conv

    p = proj(3)
    base = POOL_PAD + POOL_HALO
    n = POOL_HALO + tm
    pbuf[base:base + tm, :] = p
    lvl = pbuf[POOL_PAD:POOL_PAD + n, :] + pbuf[POOL_PAD - 1:POOL_PAD - 1 + n, :]
    sums = [lvl[POOL_HALO:, :POOL_GROUP_DIM]]
    for gi, sbuf in enumerate((s1buf, s2buf, s3buf), start=1):
        shift = POOL_WINDOWS[gi - 1]
        rest = lvl[:, POOL_GROUP_DIM:]
        sbuf[POOL_PAD:POOL_PAD + n, :] = rest
        lvl = rest + sbuf[POOL_PAD - shift:POOL_PAD - shift + n, :]
        sums.append(lvl[POOL_HALO:, :POOL_GROUP_DIM])
    pos = (s * tm + 1 + lax.broadcasted_iota(jnp.int32, (tm, 1), 0)).astype(F32)
    ys = []
    for gi, w in enumerate(POOL_WINDOWS):
        c0 = gi * POOL_GROUP_DIM
        inv_cnt = 1.0 / jnp.minimum(pos, float(w))
        dg = (sums[gi] * inv_cnt - p[:, c0:c0 + POOL_GROUP_DIM]).astype(BF16)
        ys.append(_dot(dg, wpool_ref[gi]))
    pbuf[POOL_PAD:base, :] = pbuf[POOL_PAD + tm:base + tm, :]
    y_pool = jnp.concatenate(ys, axis=1) * ps_ref[...]
    acc = acc + gate(1) * y_pool

    qb = proj(4).astype(BF16)
    scale = X_HEAD_DIM ** -0.5
    os_ = []
    for h in range(X_HEADS):
        c0 = h * X_HEAD_DIM
        c1 = c0 + X_HEAD_DIM
        sc = _dot(qb[:, c0:c1], kt_ref[0, c0:c1, :]) * scale
        e = jnp.exp(sc - jnp.max(sc, axis=-1, keepdims=True))
        a = e * (1.0 / jnp.sum(e, axis=-1, keepdims=True))
        os_.append(_dot(a.astype(BF16), v_ref[0, :, c0:c1]))
    y_mem = _dot(jnp.concatenate(os_, axis=1).astype(BF16), wxo_ref[...])
    acc = acc + gate(2) * y_mem

    mix = _dot(acc.astype(BF16), wo_ref[...])
    h1 = _layer_norm(ALPHA * x + mix, g1_ref[...], b1_ref[...])
    _tok_store(h1_ref, h1, tm)

    hh = h1.astype(BF16)
    hl = (h1 - hh.astype(F32)).astype(BF16)
    wrh = wr_ref[:, :LANES]
    logits = _dot(hh, wrh) + _dot(hl, wrh) + _dot(hh, wr_ref[:, LANES:]) + br_ref[...]
    lane = lax.broadcasted_iota(jnp.int32, (tm, LANES), 1)
    lane_f = lane.astype(F32)
    neg = jnp.float32(-jnp.inf)
    big = jnp.float32(1e9)

    def first_argmax(vals):
        m = jnp.max(vals, axis=-1, keepdims=True)
        idx = jnp.min(jnp.where(vals == m, lane_f, big), axis=-1, keepdims=True)
        return m, idx

    is_g = (lane >= N_EXPERTS) & (lane < N_EXPERTS + N_GROUPS)
    gmax, gidx = first_argmax(jnp.where(is_g, logits, neg))
    g_w = 1.0 / jnp.sum(jnp.where(is_g, jnp.exp(logits - gmax), 0.0), axis=-1, keepdims=True)
    gsel = gidx.astype(jnp.int32) - N_EXPERTS
    in_grp = (lane >> 3) == gsel
    le = jnp.where(in_grp, logits, neg)
    m1, i1 = first_argmax(le)
    m2, i2 = first_argmax(jnp.where(lane_f == i1, neg, le))
    t = jnp.exp(m2 - m1)
    den = 1.0 + t
    w1 = g_w / den
    w2 = g_w * t / den

    sel1 = lane_f == i1
    sel2 = lane_f == i2
    onehot = (sel1 | sel2).astype(BF16)
    row_i = lax.broadcasted_iota(jnp.int32, (tm, tm), 0)
    col_i = lax.broadcasted_iota(jnp.int32, (tm, tm), 1)
    before = (col_i < row_i).astype(BF16)
    prior = _dot(before, onehot) + carry[0:1, :]
    r1 = jnp.sum(jnp.where(sel1, prior, 0.0), axis=-1, keepdims=True)
    r2 = jnp.sum(jnp.where(sel2, prior, 0.0), axis=-1, keepdims=True)
    carry[...] = carry[...] + jnp.sum(onehot.astype(F32), axis=0, keepdims=True)
    cnt_ref[...] = carry[...].astype(jnp.int32)

    route = jnp.where(lane == 0, i1, 0.0)
    for k, col in enumerate((i2, r1, r2, w1, w2), start=1):
        route = jnp.where(lane == k, col, route)
    route_ref[...] = route
    ri_ref[0] = route.T[0:8, :].astype(jnp.int32)


def _mixer(x2, w_in_b, b_gate3, conv_w, wco_b, wpool_b, pool_scale, kt, v, wxo_b, wo_b, ln_g, ln_b, wr, br,
           bn, seq):
    n, d = x2.shape
    tm = TM_MIX
    spb = seq // tm
    const2 = lambda b, s: (0, 0)
    const3 = lambda b, s: (0, 0, 0)
    one = pl.Buffered(1)
    row_map = lambda b, s: (b * spb + s, 0)
    return pl.pallas_call(
        _mixer_kernel,
        grid=(bn, spb),
        in_specs=[
            pl.BlockSpec((tm, d), row_map),
            pl.BlockSpec(w_in_b.shape, const2, pipeline_mode=one),
            pl.BlockSpec(b_gate3.shape, const2, pipeline_mode=one),
            pl.BlockSpec(conv_w.shape, const2, pipeline_mode=one),
            pl.BlockSpec(wco_b.shape, const2, pipeline_mode=one),
            pl.BlockSpec(wpool_b.shape, const3, pipeline_mode=one),
            pl.BlockSpec(pool_scale.shape, const2, pipeline_mode=one),
            pl.BlockSpec((1,) + kt.shape[1:], lambda b, s: (b, 0, 0)),
            pl.BlockSpec((1,) + v.shape[1:], lambda b, s: (b, 0, 0)),
            pl.BlockSpec(wxo_b.shape, const2, pipeline_mode=one),
            pl.BlockSpec(wo_b.shape, const2, pipeline_mode=one),
            pl.BlockSpec(ln_g.shape, const2, pipeline_mode=one),
            pl.BlockSpec(ln_b.shape, const2, pipeline_mode=one),
            pl.BlockSpec(wr.shape, const2, pipeline_mode=one),
            pl.BlockSpec(br.shape, const2, pipeline_mode=one),
        ],
        out_specs=[
            pl.BlockSpec((tm * SUB, LANES), row_map),
            pl.BlockSpec((tm, LANES), row_map),
            pl.BlockSpec((1, 8, tm), lambda b, s: (b * spb + s, 0, 0)),
            pl.BlockSpec((8, LANES), const2),
        ],
        out_shape=[
            jax.ShapeDtypeStruct((n * SUB, LANES), F32),
            jax.ShapeDtypeStruct((n, LANES), F32),
            jax.ShapeDtypeStruct((n // tm, 8, tm), jnp.int32),
            jax.ShapeDtypeStruct((8, LANES), jnp.int32),
        ],
        scratch_shapes=[
            pltpu.VMEM((tm + CONV_HALO, d), F32),
            pltpu.VMEM((POOL_PAD + POOL_HALO + tm, d), F32),
            pltpu.VMEM((POOL_PAD + POOL_HALO + tm, d - POOL_GROUP_DIM), F32),
            pltpu.VMEM((POOL_PAD + POOL_HALO + tm, d - 2 * POOL_GROUP_DIM), F32),
            pltpu.VMEM((POOL_PAD + POOL_HALO + tm, d - 3 * POOL_GROUP_DIM), F32),
            pltpu.VMEM((8, LANES), F32),
        ],
        compiler_params=pltpu.CompilerParams(
            dimension_semantics=("arbitrary", "arbitrary"), vmem_limit_bytes=VMEM_LIMIT),
        name="mixer",
    )(x2, w_in_b, b_gate3, conv_w, wco_b, wpool_b, pool_scale, kt, v, wxo_b, wo_b, ln_g, ln_b, wr, br)


def _dest_kernel(ps_ref, ri_ref, dest_ref):
    dest_ref[...] = jnp.zeros(dest_ref.shape, jnp.int32)
    for t in range(dest_ref.shape[0]):
        ri = ri_ref[t]
        e = ri[0:TOP_K, :]
        start = jnp.zeros_like(e)
        for j in range(N_EXPERTS):
            start = jnp.where(e == j, ps_ref[j], start)
        dest_ref[t, 0:TOP_K, :] = start + ri[TOP_K:2 * TOP_K, :]


def _dest(pstarts, route_i):
    nt, rows, tm = route_i.shape
    blk = 8
    spec = pl.BlockSpec((blk, rows, tm), lambda i, ps: (i, 0, 0))
    return pl.pallas_call(
        _dest_kernel,
        grid_spec=pltpu.PrefetchScalarGridSpec(
            num_scalar_prefetch=1, grid=(nt // blk,), in_specs=[spec], out_specs=spec),
        out_shape=jax.ShapeDtypeStruct(route_i.shape, jnp.int32),
        compiler_params=pltpu.CompilerParams(dimension_semantics=("arbitrary",)),
        name="dest_rows",
    )(pstarts, route_i)


def _scatter_kernel(zt_ref, dest_ref, h_ref, xbuf_ref, zbuf, stage, zsem, fsem, ssem):
    i = pl.program_id(0)
    nsteps = pl.num_programs(0)
    tm = TM_ROW
    rows = tm * SUB
    slot = lax.rem(i, SCATTER_RING)

    def zero_copy(e):
        start = pl.multiple_of(zt_ref[e] * SUB, TM_EXP * SUB)
        return pltpu.make_async_copy(zbuf, xbuf_ref.at[pl.ds(start, TM_EXP * SUB), :], zsem.at[0])

    def fetch(t, s):
        src = h_ref.at[pl.ds(pl.multiple_of(t * rows, rows), rows), :]
        return pltpu.make_async_copy(src, stage.at[s], fsem.at[s])

    def drain(s):
        for k in range(TOP_K):
            pltpu.make_async_copy(stage.at[s], xbuf_ref.at[pl.ds(0, rows), :], ssem.at[s, k]).wait()

    @pl.when(i == 0)
    def _():
        zbuf[...] = jnp.zeros_like(zbuf)

        def start(e, c):
            @pl.when(zt_ref[e] >= 0)
            def _():
                zero_copy(e).start()
            return c

        def wait(e, c):
            @pl.when(zt_ref[e] >= 0)
            def _():
                zero_copy(e).wait()
            return c

        lax.fori_loop(0, 2 * N_EXPERTS, start, 0)
        fetch(0, 0).start()
        lax.fori_loop(0, 2 * N_EXPERTS, wait, 0)

    @pl.when(i >= 2)
    def _():
        drain(lax.rem(i + 1, SCATTER_RING))

    @pl.when(i + 1 < nsteps)
    def _():
        fetch(i + 1, lax.rem(i + 1, SCATTER_RING)).start()

    fetch(i, slot).wait()

    def issue(r, c):
        for k in range(TOP_K):
            _tile_copy(stage.at[slot], r, xbuf_ref, dest_ref[k * tm + r], ssem.at[slot, k]).start(priority=k)
        return c

    lax.fori_loop(0, tm, issue, 0, unroll=8)

    @pl.when(i == nsteps - 1)
    def _():
        @pl.when(i >= 1)
        def _():
            drain(lax.rem(i + 2, SCATTER_RING))

        drain(slot)


def _scatter(zero_tiles, dest_flat, h1t, p_rows):
    tm = TM_ROW
    nt = dest_flat.shape[0] // (TOP_K * tm)
    return pl.pallas_call(
        _scatter_kernel,
        grid_spec=pltpu.PrefetchScalarGridSpec(
            num_scalar_prefetch=1,
            grid=(nt,),
            in_specs=[
                pl.BlockSpec((TOP_K * tm,), lambda i, zt: (i,), memory_space=pltpu.SMEM),
                pl.BlockSpec(memory_space=pl.ANY),
            ],
            out_specs=pl.BlockSpec(memory_space=pl.ANY),
            scratch_shapes=[
                pltpu.VMEM((TM_EXP * SUB, LANES), F32),
                pltpu.VMEM((SCATTER_RING, tm * SUB, LANES), F32),
                pltpu.SemaphoreType.DMA((1,)),
                pltpu.SemaphoreType.DMA((SCATTER_RING,)),
                pltpu.SemaphoreType.DMA((SCATTER_RING, TOP_K)),
            ],
        ),
        out_shape=jax.ShapeDtypeStruct((p_rows * SUB, LANES), F32),
        compiler_params=pltpu.CompilerParams(dimension_semantics=("arbitrary",)),
        name="scatter_rows",
    )(zero_tiles, dest_flat, h1t)


def _expert_kernel(t0_ref, nt_ref, tot_ref, wup_ref, wdn_ref, xbuf_ref, ybuf_ref, wup_b, wdn_b, xs, ys, sem_in, sem_out):
    e = pl.program_id(0)
    t0 = t0_ref[e]
    nt = nt_ref[e]
    total = tot_ref[0]
    rows = TM_EXP * SUB
    max_tiles = xbuf_ref.shape[0] // rows

    def hbm_tile(ref, g):
        return ref.at[pl.ds(pl.multiple_of(g * rows, rows), rows), :]

    def x_copy(g, slot):
        return pltpu.make_async_copy(hbm_tile(xbuf_ref, g), xs.at[slot], sem_in.at[slot])

    def y_copy(g, slot):
        return pltpu.make_async_copy(ys.at[slot], hbm_tile(ybuf_ref, g), sem_out.at[slot])

    ahead = RING - 1

    @pl.when(e == 0)
    def _():
        for g0 in range(ahead):
            @pl.when(g0 < total)
            def _():
                x_copy(g0, g0).start(priority=ROW_DMA_PRIORITY)

    @pl.when(nt > 0)
    def _():
        wup_b[...] = wup_ref[0].astype(BF16)
        wdn_b[...] = wdn_ref[0].astype(BF16)

    def tile(j, c):
        g = t0 + j
        slot = lax.rem(g, RING)

        @pl.when(g + ahead < total)
        def _():
            x_copy(g + ahead, lax.rem(g + ahead, RING)).start(priority=ROW_DMA_PRIORITY)

        x_copy(g, slot).wait()

        @pl.when(g >= RING)
        def _():
            y_copy(g - RING, slot).wait()

        xb = _tok_load(xs.at[slot], TM_EXP).astype(BF16)
        hgv = _dot(xb, wup_b[...])
        hg = hgv[:, :D_EXPERT]
        hv = hgv[:, D_EXPERT:]
        act = (hg * jax.nn.sigmoid(hg)) * hv
        _tok_store(ys.at[slot], _dot(act.astype(BF16), wdn_b[...]), TM_EXP)
        y_copy(g, slot).start(priority=ROW_DMA_PRIORITY)
        return c

    lax.fori_loop(0, nt, tile, 0)

    @pl.when(e == pl.num_programs(0) - 1)
    def _():
        for back in range(RING, 0, -1):
            @pl.when(total >= back)
            def _():
                y_copy(total - back, lax.rem(total - back, RING)).wait()

        ys[0] = jnp.zeros(ys.shape[1:], F32)

        def zero_tail(g, c):
            y_copy(g, 0).start()
            y_copy(g, 0).wait()
            return c

        lax.fori_loop(total, max_tiles, zero_tail, 0)


def _experts(tile_start, tile_count, n_tiles, xbuf, w_up, w_down):
    w_map = lambda e, t0, nt, tot: (e, 0, 0)
    rows = TM_EXP * SUB
    return pl.pallas_call(
        _expert_kernel,
        grid_spec=pltpu.PrefetchScalarGridSpec(
            num_scalar_prefetch=3,
            grid=(N_EXPERTS,),
            in_specs=[
                pl.BlockSpec((1,) + w_up.shape[1:], w_map),
                pl.BlockSpec((1,) + w_down.shape[1:], w_map),
                pl.BlockSpec(memory_space=pl.ANY),
            ],
            out_specs=pl.BlockSpec(memory_space=pl.ANY),
            scratch_shapes=[
                pltpu.VMEM(w_up.shape[1:], BF16), pltpu.VMEM(w_down.shape[1:], BF16),
                pltpu.VMEM((RING, rows, LANES), F32), pltpu.VMEM((RING, rows, LANES), F32),
                pltpu.SemaphoreType.DMA((RING,)), pltpu.SemaphoreType.DMA((RING,)),
            ],
        ),
        out_shape=jax.ShapeDtypeStruct(xbuf.shape, F32),
        compiler_params=pltpu.CompilerParams(dimension_semantics=("arbitrary",), vmem_limit_bytes=VMEM_LIMIT),
        name="experts",
    )(tile_start, tile_count, n_tiles, w_up, w_down, xbuf)


def _combine_kernel(dcur_ref, dnext_ref, h_ref, route_ref, ybuf_ref, g_ref, b_ref, out_ref, ybufs, sem):
    tm = TM_ROW
    i = pl.program_id(0)
    slot = lax.rem(i, 2)

    def gather(d_ref, s):
        def issue(r, c):
            for k in range(TOP_K):
                _tile_copy(ybuf_ref, d_ref[k * tm + r], ybufs.at[s, k], r, sem.at[s, k]).start(priority=k)
            return c

        lax.fori_loop(0, tm, issue, 0, unroll=8)

    @pl.when(i == 0)
    def _():
        gather(dcur_ref, 0)

    @pl.when(i + 1 < pl.num_programs(0))
    def _():
        gather(dnext_ref, 1 - slot)

    for k in range(TOP_K):
        pltpu.make_async_copy(ybuf_ref.at[pl.ds(0, tm * SUB), :], ybufs.at[slot, k], sem.at[slot, k]).wait()
    route = route_ref[...]
    ffn = route[:, 4:5] * _tok_load(ybufs.at[slot, 0], tm) + route[:, 5:6] * _tok_load(ybufs.at[slot, 1], tm)
    out_ref[...] = _layer_norm(ALPHA * _tok_load(h_ref, tm) + ffn, g_ref[...], b_ref[...])


def _combine(dest_flat, h1t, route, ybuf, ln_g, ln_b):
    tm = TM_ROW
    nt = dest_flat.shape[0] // (TOP_K * tm)
    n = nt * tm
    return pl.pallas_call(
        _combine_kernel,
        grid=(nt,),
        in_specs=[
            pl.BlockSpec((TOP_K * tm,), lambda i: (i,), memory_space=pltpu.SMEM),
            pl.BlockSpec((TOP_K * tm,), lambda i: (jnp.minimum(i + 1, nt - 1),), memory_space=pltpu.SMEM),
            pl.BlockSpec((tm * SUB, LANES), lambda i: (i, 0)),
            pl.BlockSpec((tm, LANES), lambda i: (i, 0)),
            pl.BlockSpec(memory_space=pl.ANY),
            pl.BlockSpec(ln_g.shape, lambda i: (0, 0)),
            pl.BlockSpec(ln_b.shape, lambda i: (0, 0)),
        ],
        out_specs=pl.BlockSpec((tm, D_MODEL), lambda i: (i, 0)),
        out_shape=jax.ShapeDtypeStruct((n, D_MODEL), F32),
        scratch_shapes=[pltpu.VMEM((2, TOP_K, tm * SUB, LANES), F32), pltpu.SemaphoreType.DMA((2, TOP_K))],
        compiler_params=pltpu.CompilerParams(dimension_semantics=("arbitrary",)),
        name="combine_ln2",
    )(dest_flat, dest_flat, h1t, route, ybuf, ln_g, ln_b)


def _layer(h, mem, w_in, b_gate, conv_w, w_conv_out, w_pool, pool_scale, w_kv, w_xo, w_o, ln1_g, ln1_b,
           w_rg, b_rg, w_re, b_re, w_up, w_down, ln2_g, ln2_b):
    bn, seq, d = h.shape
    n = bn * seq
    xw = X_HEADS * X_HEAD_DIM

    kt, v = _kv_proj(mem, w_kv[:, :xw].T.astype(BF16), w_kv[:, xw:].astype(BF16))

    pad = LANES - N_EXPERTS - N_GROUPS
    w_r = jnp.concatenate([w_re, w_rg, jnp.zeros((d, pad), F32)], axis=1)
    wr_hi = w_r.astype(BF16)
    wr_lo = (w_r - wr_hi.astype(F32)).astype(BF16)
    b_r = jnp.concatenate([b_re, b_rg, jnp.zeros((pad,), F32)])[None, :]

    h1t, route, route_i, cnt = _mixer(
        h.reshape(n, d), w_in.astype(BF16), b_gate.reshape(3, d), conv_w, w_conv_out.astype(BF16),
        w_pool.astype(BF16), pool_scale[None, :], kt, v, w_xo.astype(BF16), w_o.astype(BF16),
        ln1_g[None, :], ln1_b[None, :], jnp.concatenate([wr_hi, wr_lo], axis=1), b_r, bn, seq)

    counts = cnt[0, :N_EXPERTS]
    padded = (counts + TM_EXP - 1) // TM_EXP * TM_EXP
    pends = jnp.cumsum(padded)
    pstarts = pends - padded
    max_tiles = (n * TOP_K + N_EXPERTS * (TM_EXP - 1)) // TM_EXP
    p_rows = max_tiles * TM_EXP
    n_tiles = (pends[-1] // TM_EXP).astype(jnp.int32)
    tail_ids = n_tiles + jnp.arange(N_EXPERTS, dtype=jnp.int32)
    zero_tiles = jnp.concatenate([
        jnp.where(counts % TM_EXP != 0, pends - TM_EXP, -1),
        jnp.where(tail_ids < max_tiles, tail_ids * TM_EXP, -1)]).astype(jnp.int32)

    dest = _dest(pstarts.astype(jnp.int32), route_i)[:, :TOP_K, :].reshape(-1)
    xbuf = _scatter(zero_tiles, dest, h1t, p_rows)
    ybuf = _experts((pstarts // TM_EXP).astype(jnp.int32), (padded // TM_EXP).astype(jnp.int32), n_tiles.reshape(1),
                    xbuf, w_up, w_down)
    out = _combine(dest, h1t, route, ybuf, ln2_g[None, :], ln2_b[None, :])
    return out.reshape(bn, seq, d)


def kernel(x, mem, w_in, b_gate, conv_w, w_conv_out, w_pool, pool_scale, w_kv, w_xo, w_o, ln1_g, ln1_b,
           w_router_group, b_router_group, w_router_expert, b_router_expert, w_up, w_down, ln2_g, ln2_b):
    h = x
    for l in range(DEPTH):
        h = _layer(h, mem, w_in[l], b_gate[l], conv_w[l], w_conv_out[l], w_pool[l], pool_scale[l], w_kv[l],
                   w_xo[l], w_o[l], ln1_g[l], ln1_b[l], w_router_group[l], b_router_group[l],
                   w_router_expert[l], b_router_expert[l], w_up[l], w_down[l], ln2_g[l], ln2_b[l])
    return h
```

```python
import functools

import jax
import jax.numpy as jnp
from jax import lax
from jax.experimental import pallas as pl
from jax.experimental.pallas import tpu as pltpu

D_MODEL = 1024
CONV_K = 3
POOL_WINDOWS = (2, 4, 8, 16)
POOL_GROUP_DIM = 256
X_HEADS = 4
X_HEAD_DIM = 256
N_GROUPS = 8
EXPERTS_PER_GROUP = 8
N_EXPERTS = 64
TOP_K = 2
D_EXPERT = 512
DEPTH = 1
ALPHA = (2.0 * DEPTH) ** 0.25
LN_EPS = 1e-5

LANES = 128
SUB = 8
assert D_MODEL == SUB * LANES
POOL_HALO = 16
POOL_PAD = 8
assert POOL_WINDOWS[0] == 2 and all(b == 2 * a for a, b in zip(POOL_WINDOWS, POOL_WINDOWS[1:]))
assert POOL_PAD >= POOL_WINDOWS[-2] and POOL_HALO >= POOL_WINDOWS[-1] and len(POOL_WINDOWS) == 4
CONV_HALO = 8
TM_MIX = 512
TM_EXP = 256
TM_ROW = TM_MIX
VMEM_LIMIT = 58 * 1024 * 1024
SCATTER_RING = 3
RING = 4
ROW_DMA_PRIORITY = 1

F32 = jnp.float32
BF16 = jnp.bfloat16


def _dot(a, b):
    return jnp.dot(a, b, preferred_element_type=F32)


def _tok_load(ref, tm):
    return jnp.concatenate([ref[pl.ds(c, tm, stride=SUB), :] for c in range(SUB)], axis=1)


def _tok_store(ref, val, tm):
    for c in range(SUB):
        ref[pl.ds(c, tm, stride=SUB), :] = val[:, c * LANES:(c + 1) * LANES]


def _tile_copy(src, src_tok, dst, dst_tok, sem):
    s = pl.multiple_of(src_tok * SUB, SUB)
    t = pl.multiple_of(dst_tok * SUB, SUB)
    return pltpu.make_async_copy(src.at[pl.ds(s, SUB), :], dst.at[pl.ds(t, SUB), :], sem)


def _layer_norm(h, g, b):
    mu = jnp.mean(h, axis=-1, keepdims=True)
    c = h - mu
    var = jnp.mean(c * c, axis=-1, keepdims=True)
    return c * lax.rsqrt(var + LN_EPS) * g + b


def _kv_kernel(mem_ref, wkt_ref, wv_ref, kt_ref, v_ref):
    mb = mem_ref[0].astype(BF16)
    kt = lax.dot_general(wkt_ref[...], mb, (((1,), (1,)), ((), ())), preferred_element_type=F32)
    kt_ref[0] = kt.astype(BF16)
    v_ref[0] = _dot(mb, wv_ref[...]).astype(BF16)


def _kv_proj(mem, wkt, wv):
    bn, m, d = mem.shape
    return pl.pallas_call(
        _kv_kernel,
        grid=(bn,),
        in_specs=[
            pl.BlockSpec((1, m, d), lambda b: (b, 0, 0)),
            pl.BlockSpec((d, d), lambda b: (0, 0)),
            pl.BlockSpec((d, d), lambda b: (0, 0)),
        ],
        out_specs=[
            pl.BlockSpec((1, d, m), lambda b: (b, 0, 0)),
            pl.BlockSpec((1, m, d), lambda b: (b, 0, 0)),
        ],
        out_shape=[jax.ShapeDtypeStruct((bn, d, m), BF16), jax.ShapeDtypeStruct((bn, m, d), BF16)],
        compiler_params=pltpu.CompilerParams(dimension_semantics=("arbitrary",)),
        name="kv_proj",
    )(mem, wkt, wv)


def _mixer_kernel(x_ref, win_ref, bg_ref, cw_ref, wco_ref, wpool_ref, ps_ref, kt_ref, v_ref, wxo_ref, wo_ref,
                  g1_ref, b1_ref, wr_ref, br_ref,
                  h1_ref, route_ref, ri_ref, cnt_ref,
                  ubuf, pbuf, s1buf, s2buf, s3buf, carry, hpre, *, tiles_per_seq):
    tm = TM_MIX
    d = D_MODEL
    t = pl.program_id(0)
    s = lax.rem(t, tiles_per_seq)

    @pl.when(s == 0)
    def _():
        ubuf[0:CONV_HALO, :] = jnp.zeros((CONV_HALO, d), F32)
        pbuf[0:POOL_PAD + POOL_HALO, :] = jnp.zeros((POOL_PAD + POOL_HALO, d), F32)
        for sbuf in (s1buf, s2buf, s3buf):
            sbuf[0:POOL_PAD, :] = jnp.zeros((POOL_PAD, sbuf.shape[1]), F32)

    @pl.when(t == 0)
    def _():
        carry[...] = jnp.zeros_like(carry)
        hpre[...] = jnp.zeros_like(hpre)

    routing = _route_stages(t >= 1, hpre, g1_ref, b1_ref, wr_ref, br_ref, h1_ref, route_ref, ri_ref, cnt_ref, carry)

    x = x_ref[...]
    xb = x.astype(BF16)

    def proj(sec):
        return _dot(xb, win_ref[:, sec * d:(sec + 1) * d])

    def gate(i):
        return jax.nn.sigmoid(proj(5 + i) + bg_ref[i:i + 1, :])

    u = proj(1) * proj(2)
    next(routing)
    ubuf[CONV_HALO:CONV_HALO + tm, :] = u
    v = cw_ref[CONV_K - 1:CONV_K, :] * u
    for k in range(CONV_K - 1):
        off = CONV_HALO - (CONV_K - 1) + k
        v = v + cw_ref[k:k + 1, :] * ubuf[off:off + tm, :]
    ubuf[0:CONV_HALO, :] = ubuf[tm:tm + CONV_HALO, :]
    y_conv = _dot((proj(0) * v).astype(BF16), wco_ref[...])
    acc = gate(0) * y_conv
    next(routing)

    p = proj(3)
    base = POOL_PAD + POOL_HALO
    n = POOL_HALO + tm
    pbuf[base:base + tm, :] = p
    lvl = pbuf[POOL_PAD:POOL_PAD + n, :] + pbuf[POOL_PAD - 1:POOL_PAD - 1 + n, :]
    sums = [lvl[POOL_HALO:, :POOL_GROUP_DIM]]
    for gi, sbuf in enumerate((s1buf, s2buf, s3buf), start=1):
        shift = POOL_WINDOWS[gi - 1]
        rest = lvl[:, POOL_GROUP_DIM:]
        sbuf[POOL_PAD:POOL_PAD + n, :] = rest
        lvl = rest + sbuf[POOL_PAD - shift:POOL_PAD - shift + n, :]
        sums.append(lvl[POOL_HALO:, :POOL_GROUP_DIM])
    pos = (s * tm + 1 + lax.broadcasted_iota(jnp.int32, (tm, 1), 0)).astype(F32)
    ys = []
    for gi, w in enumerate(POOL_WINDOWS):
        c0 = gi * POOL_GROUP_DIM
        inv_cnt = 1.0 / jnp.minimum(pos, float(w))
        dg = (sums[gi] * inv_cnt - p[:, c0:c0 + POOL_GROUP_DIM]).astype(BF16)
        ys.append(_dot(dg, wpool_ref[gi]))
    pbuf[POOL_PAD:base, :] = pbuf[POOL_PAD + tm:base + tm, :]
    y_pool = jnp.concatenate(ys, axis=1) * ps_ref[...]
    acc = acc + gate(1) * y_pool
    next(routing)

    qb = proj(4).astype(BF16)
    scale = X_HEAD_DIM ** -0.5
    os_ = []
    for h in range(X_HEADS):
        c0 = h * X_HEAD_DIM
        c1 = c0 + X_HEAD_DIM
        sc = _dot(qb[:, c0:c1], kt_ref[0, c0:c1, :]) * scale
        e = jnp.exp(sc - jnp.max(sc, axis=-1, keepdims=True))
        a = e * (1.0 / jnp.sum(e, axis=-1, keepdims=True))
        os_.append(_dot(a.astype(BF16), v_ref[0, :, c0:c1]))
    y_mem = _dot(jnp.concatenate(os_, axis=1).astype(BF16), wxo_ref[...])
    acc = acc + gate(2) * y_mem
    next(routing)

    hpre[...] = ALPHA * x + _dot(acc.astype(BF16), wo_ref[...])


def _route_stages(valid, hpre, g1_ref, b1_ref, wr_ref, br_ref, h1_ref, route_ref, ri_ref, cnt_ref, carry):
    tm = TM_MIX
    h1 = _layer_norm(hpre[...], g1_ref[...], b1_ref[...])
    _tok_store(h1_ref, h1, tm)
    yield

    hh = h1.astype(BF16)
    hl = (h1 - hh.astype(F32)).astype(BF16)
    wrh = wr_ref[:, :LANES]
    logits = _dot(hh, wrh) + _dot(hl, wrh) + _dot(hh, wr_ref[:, LANES:]) + br_ref[...]
    yield
    lane = lax.broadcasted_iota(jnp.int32, (tm, LANES), 1)
    lane_f = lane.astype(F32)
    neg = jnp.float32(-jnp.inf)
    big = jnp.float32(1e9)

    def first_argmax(vals):
        m = jnp.max(vals, axis=-1, keepdims=True)
        idx = jnp.min(jnp.where(vals == m, lane_f, big), axis=-1, keepdims=True)
        return m, idx

    is_g = (lane >= N_EXPERTS) & (lane < N_EXPERTS + N_GROUPS)
    gmax, gidx = first_argmax(jnp.where(is_g, logits, neg))
    g_w = 1.0 / jnp.sum(jnp.where(is_g, jnp.exp(logits - gmax), 0.0), axis=-1, keepdims=True)
    gsel = gidx.astype(jnp.int32) - N_EXPERTS
    in_grp = (lane >> 3) == gsel
    le = jnp.where(in_grp, logits, neg)
    m1, i1 = first_argmax(le)
    m2, i2 = first_argmax(jnp.where(lane_f == i1, neg, le))
    t = jnp.exp(m2 - m1)
    den = 1.0 + t
    w1 = g_w / den
    w2 = g_w * t / den
    yield

    sel1 = lane_f == i1
    sel2 = lane_f == i2
    onehot = (sel1 | sel2).astype(BF16)
    row_i = lax.broadcasted_iota(jnp.int32, (tm, tm), 0)
    col_i = lax.broadcasted_iota(jnp.int32, (tm, tm), 1)
    before = (col_i < row_i).astype(BF16)
    prior = _dot(before, onehot) + carry[0:1, :]
    r1 = jnp.sum(jnp.where(sel1, prior, 0.0), axis=-1, keepdims=True)
    r2 = jnp.sum(jnp.where(sel2, prior, 0.0), axis=-1, keepdims=True)
    carry[...] = carry[...] + jnp.where(valid, jnp.sum(onehot.astype(F32), axis=0, keepdims=True), 0.0)
    cnt_ref[...] = carry[...].astype(jnp.int32)

    route = jnp.where(lane == 0, i1, 0.0)
    for k, col in enumerate((i2, r1, r2, w1, w2), start=1):
        route = jnp.where(lane == k, col, route)
    route_ref[...] = route
    ri_ref[0] = route.T[0:8, :].astype(jnp.int32)
    yield


def _mixer(x2, w_in_b, b_gate3, conv_w, wco_b, wpool_b, pool_scale, kt, v, wxo_b, wo_b, ln_g, ln_b, wr, br,
           bn, seq):
    n, d = x2.shape
    tm = TM_MIX
    spb = seq // tm
    nt = bn * spb
    const2 = lambda t: (0, 0)
    const3 = lambda t: (0, 0, 0)
    one = pl.Buffered(1)
    mixed = lambda t: jnp.minimum(t, nt - 1)
    routed = lambda t: jnp.maximum(t - 1, 0)
    mem_map = lambda t: (mixed(t) // spb, 0, 0)
    row_map = lambda t: (routed(t), 0)
    return pl.pallas_call(
        functools.partial(_mixer_kernel, tiles_per_seq=spb),
        grid=(nt + 1,),
        in_specs=[
            pl.BlockSpec((tm, d), lambda t: (mixed(t), 0)),
            pl.BlockSpec(w_in_b.shape, const2, pipeline_mode=one),
            pl.BlockSpec(b_gate3.shape, const2, pipeline_mode=one),
            pl.BlockSpec(conv_w.shape, const2, pipeline_mode=one),
            pl.BlockSpec(wco_b.shape, const2, pipeline_mode=one),
            pl.BlockSpec(wpool_b.shape, const3, pipeline_mode=one),
            pl.BlockSpec(pool_scale.shape, const2, pipeline_mode=one),
            pl.BlockSpec((1,) + kt.shape[1:], mem_map),
            pl.BlockSpec((1,) + v.shape[1:], mem_map),
            pl.BlockSpec(wxo_b.shape, const2, pipeline_mode=one),
            pl.BlockSpec(wo_b.shape, const2, pipeline_mode=one),
            pl.BlockSpec(ln_g.shape, const2, pipeline_mode=one),
            pl.BlockSpec(ln_b.shape, const2, pipeline_mode=one),
            pl.BlockSpec(wr.shape, const2, pipeline_mode=one),
            pl.BlockSpec(br.shape, const2, pipeline_mode=one),
        ],
        out_specs=[
            pl.BlockSpec((tm * SUB, LANES), row_map),
            pl.BlockSpec((tm, LANES), row_map),
            pl.BlockSpec((1, 8, tm), lambda t: (routed(t), 0, 0)),
            pl.BlockSpec((8, LANES), const2),
        ],
        out_shape=[
            jax.ShapeDtypeStruct((n * SUB, LANES), F32),
            jax.ShapeDtypeStruct((n, LANES), F32),
            jax.ShapeDtypeStruct((n // tm, 8, tm), jnp.int32),
            jax.ShapeDtypeStruct((8, LANES), jnp.int32),
        ],
        scratch_shapes=[
            pltpu.VMEM((tm + CONV_HALO, d), F32),
            pltpu.VMEM((POOL_PAD + POOL_HALO + tm, d), F32),
            pltpu.VMEM((POOL_PAD + POOL_HALO + tm, d - POOL_GROUP_DIM), F32),
            pltpu.VMEM((POOL_PAD + POOL_HALO + tm, d - 2 * POOL_GROUP_DIM), F32),
            pltpu.VMEM((POOL_PAD + POOL_HALO + tm, d - 3 * POOL_GROUP_DIM), F32),
            pltpu.VMEM((8, LANES), F32),
            pltpu.VMEM((tm, d), F32),
        ],
        compiler_params=pltpu.CompilerParams(dimension_semantics=("arbitrary",), vmem_limit_bytes=VMEM_LIMIT),
        name="mixer",
    )(x2, w_in_b, b_gate3, conv_w, wco_b, wpool_b, pool_scale, kt, v, wxo_b, wo_b, ln_g, ln_b, wr, br)


def _dest_kernel(ps_ref, ri_ref, dest_ref):
    dest_ref[...] = jnp.zeros(dest_ref.shape, jnp.int32)
    for t in range(dest_ref.shape[0]):
        ri = ri_ref[t]
        e = ri[0:TOP_K, :]
        start = jnp.zeros_like(e)
        for j in range(N_EXPERTS):
            start = jnp.where(e == j, ps_ref[j], start)
        dest_ref[t, 0:TOP_K, :] = start + ri[TOP_K:2 * TOP_K, :]


def _dest(pstarts, route_i):
    nt, rows, tm = route_i.shape
    blk = 8
    spec = pl.BlockSpec((blk, rows, tm), lambda i, ps: (i, 0, 0))
    return pl.pallas_call(
        _dest_kernel,
        grid_spec=pltpu.PrefetchScalarGridSpec(
            num_scalar_prefetch=1, grid=(nt // blk,), in_specs=[spec], out_specs=spec),
        out_shape=jax.ShapeDtypeStruct(route_i.shape, jnp.int32),
        compiler_params=pltpu.CompilerParams(dimension_semantics=("arbitrary",)),
        name="dest_rows",
    )(pstarts, route_i)


def _scatter_kernel(zt_ref, dest_ref, h_ref, xbuf_ref, zbuf, stage, zsem, fsem, ssem):
    i = pl.program_id(0)
    nsteps = pl.num_programs(0)
    tm = TM_ROW
    rows = tm * SUB
    slot = lax.rem(i, SCATTER_RING)

    def zero_copy(e):
        start = pl.multiple_of(zt_ref[e] * SUB, TM_EXP * SUB)
        return pltpu.make_async_copy(zbuf, xbuf_ref.at[pl.ds(start, TM_EXP * SUB), :], zsem.at[0])

    def fetch(t, s):
        src = h_ref.at[pl.ds(pl.multiple_of(t * rows, rows), rows), :]
        return pltpu.make_async_copy(src, stage.at[s], fsem.at[s])

    def drain(s):
        for k in range(TOP_K):
            pltpu.make_async_copy(stage.at[s], xbuf_ref.at[pl.ds(0, rows), :], ssem.at[s, k]).wait()

    @pl.when(i == 0)
    def _():
        zbuf[...] = jnp.zeros_like(zbuf)

        def start(e, c):
            @pl.when(zt_ref[e] >= 0)
            def _():
                zero_copy(e).start()
            return c

        def wait(e, c):
            @pl.when(zt_ref[e] >= 0)
            def _():
                zero_copy(e).wait()
            return c

        lax.fori_loop(0, 2 * N_EXPERTS, start, 0)
        fetch(0, 0).start()
        lax.fori_loop(0, 2 * N_EXPERTS, wait, 0)

    @pl.when(i >= 2)
    def _():
        drain(lax.rem(i + 1, SCATTER_RING))

    @pl.when(i + 1 < nsteps)
    def _():
        fetch(i + 1, lax.rem(i + 1, SCATTER_RING)).start()

    fetch(i, slot).wait()

    def issue(r, c):
        for k in range(TOP_K):
            _tile_copy(stage.at[slot], r, xbuf_ref, dest_ref[k * tm + r], ssem.at[slot, k]).start(priority=k)
        return c

    lax.fori_loop(0, tm, issue, 0, unroll=8)

    @pl.when(i == nsteps - 1)
    def _():
        @pl.when(i >= 1)
        def _():
            drain(lax.rem(i + 2, SCATTER_RING))

        drain(slot)


def _scatter(zero_tiles, dest_flat, h1t, p_rows):
    tm = TM_ROW
    nt = dest_flat.shape[0] // (TOP_K * tm)
    return pl.pallas_call(
        _scatter_kernel,
        grid_spec=pltpu.PrefetchScalarGridSpec(
            num_scalar_prefetch=1,
            grid=(nt,),
            in_specs=[
                pl.BlockSpec((TOP_K * tm,), lambda i, zt: (i,), memory_space=pltpu.SMEM),
                pl.BlockSpec(memory_space=pl.ANY),
            ],
            out_specs=pl.BlockSpec(memory_space=pl.ANY),
            scratch_shapes=[
                pltpu.VMEM((TM_EXP * SUB, LANES), F32),
                pltpu.VMEM((SCATTER_RING, tm * SUB, LANES), F32),
                pltpu.SemaphoreType.DMA((1,)),
                pltpu.SemaphoreType.DMA((SCATTER_RING,)),
                pltpu.SemaphoreType.DMA((SCATTER_RING, TOP_K)),
            ],
        ),
        out_shape=jax.ShapeDtypeStruct((p_rows * SUB, LANES), F32),
        compiler_params=pltpu.CompilerParams(dimension_semantics=("arbitrary",)),
        name="scatter_rows",
    )(zero_tiles, dest_flat, h1t)


def _expert_kernel(t0_ref, nt_ref, tot_ref, wup_ref, wdn_ref, xbuf_ref, ybuf_ref, wup_b, wdn_b, xs, ys, sem_in, sem_out):
    e = pl.program_id(0)
    t0 = t0_ref[e]
    nt = nt_ref[e]
    total = tot_ref[0]
    rows = TM_EXP * SUB
    max_tiles = xbuf_ref.shape[0] // rows

    def hbm_tile(ref, g):
        return ref.at[pl.ds(pl.multiple_of(g * rows, rows), rows), :]

    def x_copy(g, slot):
        return pltpu.make_async_copy(hbm_tile(xbuf_ref, g), xs.at[slot], sem_in.at[slot])

    def y_copy(g, slot):
        return pltpu.make_async_copy(ys.at[slot], hbm_tile(ybuf_ref, g), sem_out.at[slot])

    ahead = RING - 1

    @pl.when(e == 0)
    def _():
        for g0 in range(ahead):
            @pl.when(g0 < total)
            def _():
                x_copy(g0, g0).start(priority=ROW_DMA_PRIORITY)

    @pl.when(nt > 0)
    def _():
        wup_b[...] = wup_ref[0].astype(BF16)
        wdn_b[...] = wdn_ref[0].astype(BF16)

    def tile(j, c):
        g = t0 + j
        slot = lax.rem(g, RING)

        @pl.when(g + ahead < total)
        def _():
            x_copy(g + ahead, lax.rem(g + ahead, RING)).start(priority=ROW_DMA_PRIORITY)

        x_copy(g, slot).wait()

        @pl.when(g >= RING)
        def _():
            y_copy(g - RING, slot).wait()

        xb = _tok_load(xs.at[slot], TM_EXP).astype(BF16)
        hgv = _dot(xb, wup_b[...])
        hg = hgv[:, :D_EXPERT]
        hv = hgv[:, D_EXPERT:]
        act = (hg * jax.nn.sigmoid(hg)) * hv
        _tok_store(ys.at[slot], _dot(act.astype(BF16), wdn_b[...]), TM_EXP)
        y_copy(g, slot).start(priority=ROW_DMA_PRIORITY)
        return c

    lax.fori_loop(0, nt, tile, 0)

    @pl.when(e == pl.num_programs(0) - 1)
    def _():
        for back in range(RING, 0, -1):
            @pl.when(total >= back)
            def _():
                y_copy(total - back, lax.rem(total - back, RING)).wait()

        ys[0] = jnp.zeros(ys.shape[1:], F32)

        def zero_tail(g, c):
            y_copy(g, 0).start()
            y_copy(g, 0).wait()
            return c

        lax.fori_loop(total, max_tiles, zero_tail, 0)


def _experts(tile_start, tile_count, n_tiles, xbuf, w_up, w_down):
    w_map = lambda e, t0, nt, tot: (e, 0, 0)
    rows = TM_EXP * SUB
    return pl.pallas_call(
        _expert_kernel,
        grid_spec=pltpu.PrefetchScalarGridSpec(
            num_scalar_prefetch=3,
            grid=(N_EXPERTS,),
            in_specs=[
                pl.BlockSpec((1,) + w_up.shape[1:], w_map),
                pl.BlockSpec((1,) + w_down.shape[1:], w_map),
                pl.BlockSpec(memory_space=pl.ANY),
            ],
            out_specs=pl.BlockSpec(memory_space=pl.ANY),
            scratch_shapes=[
                pltpu.VMEM(w_up.shape[1:], BF16), pltpu.VMEM(w_down.shape[1:], BF16),
                pltpu.VMEM((RING, rows, LANES), F32), pltpu.VMEM((RING, rows, LANES), F32),
                pltpu.SemaphoreType.DMA((RING,)), pltpu.SemaphoreType.DMA((RING,)),
            ],
        ),
        out_shape=jax.ShapeDtypeStruct(xbuf.shape, F32),
        compiler_params=pltpu.CompilerParams(dimension_semantics=("arbitrary",), vmem_limit_bytes=VMEM_LIMIT),
        name="experts",
    )(tile_start, tile_count, n_tiles, w_up, w_down, xbuf)


def _combine_kernel(dcur_ref, dnext_ref, h_ref, route_ref, ybuf_ref, g_ref, b_ref, out_ref, ybufs, sem):
    tm = TM_ROW
    i = pl.program_id(0)
    slot = lax.rem(i, 2)

    def gather(d_ref, s):
        def issue(r, c):
            for k in range(TOP_K):
                _tile_copy(ybuf_ref, d_ref[k * tm + r], ybufs.at[s, k], r, sem.at[s, k]).start(priority=k)
            return c

        lax.fori_loop(0, tm, issue, 0, unroll=8)

    @pl.when(i == 0)
    def _():
        gather(dcur_ref, 0)

    @pl.when(i + 1 < pl.num_programs(0))
    def _():
        gather(dnext_ref, 1 - slot)

    for k in range(TOP_K):
        pltpu.make_async_copy(ybuf_ref.at[pl.ds(0, tm * SUB), :], ybufs.at[slot, k], sem.at[slot, k]).wait()
    route = route_ref[...]
    ffn = route[:, 4:5] * _tok_load(ybufs.at[slot, 0], tm) + route[:, 5:6] * _tok_load(ybufs.at[slot, 1], tm)
    out_ref[...] = _layer_norm(ALPHA * _tok_load(h_ref, tm) + ffn, g_ref[...], b_ref[...])


def _combine(dest_flat, h1t, route, ybuf, ln_g, ln_b):
    tm = TM_ROW
    nt = dest_flat.shape[0] // (TOP_K * tm)
    n = nt * tm
    return pl.pallas_call(
        _combine_kernel,
        grid=(nt,),
        in_specs=[
            pl.BlockSpec((TOP_K * tm,), lambda i: (i,), memory_space=pltpu.SMEM),
            pl.BlockSpec((TOP_K * tm,), lambda i: (jnp.minimum(i + 1, nt - 1),), memory_space=pltpu.SMEM),
            pl.BlockSpec((tm * SUB, LANES), lambda i: (i, 0)),
            pl.BlockSpec((tm, LANES), lambda i: (i, 0)),
            pl.BlockSpec(memory_space=pl.ANY),
            pl.BlockSpec(ln_g.shape, lambda i: (0, 0)),
            pl.BlockSpec(ln_b.shape, lambda i: (0, 0)),
        ],
        out_specs=pl.BlockSpec((tm, D_MODEL), lambda i: (i, 0)),
        out_shape=jax.ShapeDtypeStruct((n, D_MODEL), F32),
        scratch_shapes=[pltpu.VMEM((2, TOP_K, tm * SUB, LANES), F32), pltpu.SemaphoreType.DMA((2, TOP_K))],
        compiler_params=pltpu.CompilerParams(dimension_semantics=("arbitrary",)),
        name="combine_ln2",
    )(dest_flat, dest_flat, h1t, route, ybuf, ln_g, ln_b)


def _layer(h, mem, w_in, b_gate, conv_w, w_conv_out, w_pool, pool_scale, w_kv, w_xo, w_o, ln1_g, ln1_b,
           w_rg, b_rg, w_re, b_re, w_up, w_down, ln2_g, ln2_b):
    bn, seq, d = h.shape
    n = bn * seq
    xw = X_HEADS * X_HEAD_DIM

    kt, v = _kv_proj(mem, w_kv[:, :xw].T.astype(BF16), w_kv[:, xw:].astype(BF16))

    pad = LANES - N_EXPERTS - N_GROUPS
    w_r = jnp.concatenate([w_re, w_rg, jnp.zeros((d, pad), F32)], axis=1)
    wr_hi = w_r.astype(BF16)
    wr_lo = (w_r - wr_hi.astype(F32)).astype(BF16)
    b_r = jnp.concatenate([b_re, b_rg, jnp.zeros((pad,), F32)])[None, :]

    h1t, route, route_i, cnt = _mixer(
        h.reshape(n, d), w_in.astype(BF16), b_gate.reshape(3, d), conv_w, w_conv_out.astype(BF16),
        w_pool.astype(BF16), pool_scale[None, :], kt, v, w_xo.astype(BF16), w_o.astype(BF16),
        ln1_g[None, :], ln1_b[None, :], jnp.concatenate([wr_hi, wr_lo], axis=1), b_r, bn, seq)

    counts = cnt[0, :N_EXPERTS]
    padded = (counts + TM_EXP - 1) // TM_EXP * TM_EXP
    pends = jnp.cumsum(padded)
    pstarts = pends - padded
    max_tiles = (n * TOP_K + N_EXPERTS * (TM_EXP - 1)) // TM_EXP
    p_rows = max_tiles * TM_EXP
    n_tiles = (pends[-1] // TM_EXP).astype(jnp.int32)
    tail_ids = n_tiles + jnp.arange(N_EXPERTS, dtype=jnp.int32)
    zero_tiles = jnp.concatenate([
        jnp.where(counts % TM_EXP != 0, pends - TM_EXP, -1),
        jnp.where(tail_ids < max_tiles, tail_ids * TM_EXP, -1)]).astype(jnp.int32)

    dest = _dest(pstarts.astype(jnp.int32), route_i)[:, :TOP_K, :].reshape(-1)
    xbuf = _scatter(zero_tiles, dest, h1t, p_rows)
    ybuf = _experts((pstarts // TM_EXP).astype(jnp.int32), (padded // TM_EXP).astype(jnp.int32), n_tiles.reshape(1),
                    xbuf, w_up, w_down)
    out = _combine(dest, h1t, route, ybuf, ln2_g[None, :], ln2_b[None, :])
    return out.reshape(bn, seq, d)


def kernel(x, mem, w_in, b_gate, conv_w, w_conv_out, w_pool, pool_scale, w_kv, w_xo, w_o, ln1_g, ln1_b,
           w_router_group, b_router_group, w_router_expert, b_router_expert, w_up, w_down, ln2_g, ln2_b):
    h = x
    for l in range(DEPTH):
        h = _layer(h, mem, w_in[l], b_gate[l], conv_w[l], w_conv_out[l], w_pool[l], pool_scale[l], w_kv[l],
                   w_xo[l], w_o[l], ln1_g[l], ln1_b[l], w_router_group[l], b_router_group[l],
                   w_router_expert[l], b_router_expert[l], w_up[l], w_down[l], ln2_g[l], ln2_b[l])
    return h
```

```python
import functools

import jax
import jax.numpy as jnp
from jax import lax
from jax.experimental import pallas as pl
from jax.experimental.pallas import tpu as pltpu

D_MODEL = 1024
CONV_K = 3
POOL_WINDOWS = (2, 4, 8, 16)
POOL_GROUP_DIM = 256
X_HEADS = 4
X_HEAD_DIM = 256
N_GROUPS = 8
EXPERTS_PER_GROUP = 8
N_EXPERTS = 64
TOP_K = 2
D_EXPERT = 512
DEPTH = 1
ALPHA = (2.0 * DEPTH) ** 0.25
LN_EPS = 1e-5

LANES = 128
SUB = 8
assert D_MODEL == SUB * LANES
POOL_HALO = 16
POOL_PAD = 8
assert POOL_WINDOWS[0] == 2 and all(b == 2 * a for a, b in zip(POOL_WINDOWS, POOL_WINDOWS[1:]))
assert POOL_PAD >= POOL_WINDOWS[-2] and POOL_HALO >= POOL_WINDOWS[-1] and len(POOL_WINDOWS) == 4
CONV_HALO = 8
TM_MIX = 512
TM_EXP = 256
TM_ROW = TM_MIX
VMEM_LIMIT = 58 * 1024 * 1024
ISSUE_UNROLL = 16
SCATTER_RING = 3
RING = 6
ROW_DMA_PRIORITY = 1

F32 = jnp.float32
BF16 = jnp.bfloat16


def _dot(a, b):
    return jnp.dot(a, b, preferred_element_type=F32)


def _tok_load(ref, tm):
    return jnp.concatenate([ref[pl.ds(c, tm, stride=SUB), :] for c in range(SUB)], axis=1)


def _tok_store(ref, val, tm):
    for c in range(SUB):
        ref[pl.ds(c, tm, stride=SUB), :] = val[:, c * LANES:(c + 1) * LANES]


def _tile_copy(src, src_tok, dst, dst_tok, sem):
    s = pl.multiple_of(src_tok * SUB, SUB)
    t = pl.multiple_of(dst_tok * SUB, SUB)
    return pltpu.make_async_copy(src.at[pl.ds(s, SUB), :], dst.at[pl.ds(t, SUB), :], sem)


def _layer_norm(h, g, b):
    mu = jnp.mean(h, axis=-1, keepdims=True)
    c = h - mu
    var = jnp.mean(c * c, axis=-1, keepdims=True)
    return c * lax.rsqrt(var + LN_EPS) * g + b


def _kv_kernel(mem_ref, wkt_ref, wv_ref, kt_ref, v_ref):
    mb = mem_ref[0].astype(BF16)
    kt = lax.dot_general(wkt_ref[...], mb, (((1,), (1,)), ((), ())), preferred_element_type=F32)
    kt_ref[0] = kt.astype(BF16)
    v_ref[0] = _dot(mb, wv_ref[...]).astype(BF16)


def _kv_proj(mem, wkt, wv):
    bn, m, d = mem.shape
    return pl.pallas_call(
        _kv_kernel,
        grid=(bn,),
        in_specs=[
            pl.BlockSpec((1, m, d), lambda b: (b, 0, 0)),
            pl.BlockSpec((d, d), lambda b: (0, 0)),
            pl.BlockSpec((d, d), lambda b: (0, 0)),
        ],
        out_specs=[
            pl.BlockSpec((1, d, m), lambda b: (b, 0, 0)),
            pl.BlockSpec((1, m, d), lambda b: (b, 0, 0)),
        ],
        out_shape=[jax.ShapeDtypeStruct((bn, d, m), BF16), jax.ShapeDtypeStruct((bn, m, d), BF16)],
        compiler_params=pltpu.CompilerParams(dimension_semantics=("arbitrary",)),
        name="kv_proj",
    )(mem, wkt, wv)


def _mixer_kernel(x_ref, win_ref, bg_ref, cw_ref, wco_ref, wpool_ref, ps_ref, kt_ref, v_ref, wxo_ref, wo_ref,
                  g1_ref, b1_ref, wr_ref, br_ref,
                  h1_ref, route_ref, ri_ref, cnt_ref,
                  ubuf, pbuf, s1buf, s2buf, s3buf, carry, hpre, *, tiles_per_seq):
    tm = TM_MIX
    d = D_MODEL
    t = pl.program_id(0)
    s = lax.rem(t, tiles_per_seq)

    @pl.when(s == 0)
    def _():
        ubuf[0:CONV_HALO, :] = jnp.zeros((CONV_HALO, d), F32)
        pbuf[0:POOL_PAD + POOL_HALO, :] = jnp.zeros((POOL_PAD + POOL_HALO, d), F32)
        for sbuf in (s1buf, s2buf, s3buf):
            sbuf[0:POOL_PAD, :] = jnp.zeros((POOL_PAD, sbuf.shape[1]), F32)

    @pl.when(t == 0)
    def _():
        carry[...] = jnp.zeros_like(carry)
        hpre[...] = jnp.zeros_like(hpre)

    routing = _route_stages(t >= 1, hpre, g1_ref, b1_ref, wr_ref, br_ref, h1_ref, route_ref, ri_ref, cnt_ref, carry)

    x = x_ref[...]
    xb = x.astype(BF16)

    def proj(sec):
        return _dot(xb, win_ref[:, sec * d:(sec + 1) * d])

    def gate(i):
        return jax.nn.sigmoid(proj(5 + i) + bg_ref[i:i + 1, :])

    u = proj(1) * proj(2)
    next(routing)
    ubuf[CONV_HALO:CONV_HALO + tm, :] = u
    v = cw_ref[CONV_K - 1:CONV_K, :] * u
    for k in range(CONV_K - 1):
        off = CONV_HALO - (CONV_K - 1) + k
        v = v + cw_ref[k:k + 1, :] * ubuf[off:off + tm, :]
    ubuf[0:CONV_HALO, :] = ubuf[tm:tm + CONV_HALO, :]
    y_conv = _dot((proj(0) * v).astype(BF16), wco_ref[...])
    acc = gate(0) * y_conv
    next(routing)

    p = proj(3)
    base = POOL_PAD + POOL_HALO
    n = POOL_HALO + tm
    pbuf[base:base + tm, :] = p
    lvl = pbuf[POOL_PAD:POOL_PAD + n, :] + pbuf[POOL_PAD - 1:POOL_PAD - 1 + n, :]
    sums = [lvl[POOL_HALO:, :POOL_GROUP_DIM]]
    for gi, sbuf in enumerate((s1buf, s2buf, s3buf), start=1):
        shift = POOL_WINDOWS[gi - 1]
        rest = lvl[:, POOL_GROUP_DIM:]
        sbuf[POOL_PAD:POOL_PAD + n, :] = rest
        lvl = rest + sbuf[POOL_PAD - shift:POOL_PAD - shift + n, :]
        sums.append(lvl[POOL_HALO:, :POOL_GROUP_DIM])
    pos = (s * tm + 1 + lax.broadcasted_iota(jnp.int32, (tm, 1), 0)).astype(F32)
    ys = []
    for gi, w in enumerate(POOL_WINDOWS):
        c0 = gi * POOL_GROUP_DIM
        inv_cnt = 1.0 / jnp.minimum(pos, float(w))
        dg = (sums[gi] * inv_cnt - p[:, c0:c0 + POOL_GROUP_DIM]).astype(BF16)
        ys.append(_dot(dg, wpool_ref[gi]))
    pbuf[POOL_PAD:base, :] = pbuf[POOL_PAD + tm:base + tm, :]
    y_pool = jnp.concatenate(ys, axis=1) * ps_ref[...]
    acc = acc + gate(1) * y_pool
    next(routing)

    qb = proj(4).astype(BF16)
    scale = X_HEAD_DIM ** -0.5
    os_ = []
    for h in range(X_HEADS):
        c0 = h * X_HEAD_DIM
        c1 = c0 + X_HEAD_DIM
        sc = _dot(qb[:, c0:c1], kt_ref[0, c0:c1, :]) * scale
        e = jnp.exp(sc - jnp.max(sc, axis=-1, keepdims=True))
        a = e * (1.0 / jnp.sum(e, axis=-1, keepdims=True))
        os_.append(_dot(a.astype(BF16), v_ref[0, :, c0:c1]))
    y_mem = _dot(jnp.concatenate(os_, axis=1).astype(BF16), wxo_ref[...])
    acc = acc + gate(2) * y_mem
    next(routing)

    hpre[...] = ALPHA * x + _dot(acc.astype(BF16), wo_ref[...])


def _route_stages(valid, hpre, g1_ref, b1_ref, wr_ref, br_ref, h1_ref, route_ref, ri_ref, cnt_ref, carry):
    tm = TM_MIX
    h1 = _layer_norm(hpre[...], g1_ref[...], b1_ref[...])
    _tok_store(h1_ref, h1, tm)
    yield

    hh = h1.astype(BF16)
    hl = (h1 - hh.astype(F32)).astype(BF16)
    wrh = wr_ref[:, :LANES]
    logits = _dot(hh, wrh) + _dot(hl, wrh) + _dot(hh, wr_ref[:, LANES:]) + br_ref[...]
    yield
    lane = lax.broadcasted_iota(jnp.int32, (tm, LANES), 1)
    lane_f = lane.astype(F32)
    neg = jnp.float32(-jnp.inf)
    big = jnp.float32(1e9)

    def first_argmax(vals):
        m = jnp.max(vals, axis=-1, keepdims=True)
        idx = jnp.min(jnp.where(vals == m, lane_f, big), axis=-1, keepdims=True)
        return m, idx

    is_g = (lane >= N_EXPERTS) & (lane < N_EXPERTS + N_GROUPS)
    gmax, gidx = first_argmax(jnp.where(is_g, logits, neg))
    g_w = 1.0 / jnp.sum(jnp.where(is_g, jnp.exp(logits - gmax), 0.0), axis=-1, keepdims=True)
    gsel = gidx.astype(jnp.int32) - N_EXPERTS
    in_grp = (lane >> 3) == gsel
    le = jnp.where(in_grp, logits, neg)
    m1, i1 = first_argmax(le)
    m2, i2 = first_argmax(jnp.where(lane_f == i1, neg, le))
    t = jnp.exp(m2 - m1)
    den = 1.0 + t
    w1 = g_w / den
    w2 = g_w * t / den
    yield

    sel1 = lane_f == i1
    sel2 = lane_f == i2
    onehot = (sel1 | sel2).astype(BF16)
    row_i = lax.broadcasted_iota(jnp.int32, (tm, tm), 0)
    col_i = lax.broadcasted_iota(jnp.int32, (tm, tm), 1)
    before = (col_i < row_i).astype(BF16)
    prior = _dot(before, onehot) + carry[0:1, :]
    r1 = jnp.sum(jnp.where(sel1, prior, 0.0), axis=-1, keepdims=True)
    r2 = jnp.sum(jnp.where(sel2, prior, 0.0), axis=-1, keepdims=True)
    carry[...] = carry[...] + jnp.where(valid, jnp.sum(onehot.astype(F32), axis=0, keepdims=True), 0.0)
    cnt_ref[...] = carry[...].astype(jnp.int32)

    route = jnp.where(lane == 0, i1, 0.0)
    for k, col in enumerate((i2, r1, r2, w1, w2), start=1):
        route = jnp.where(lane == k, col, route)
    route_ref[...] = route
    ri_ref[0] = route.T[0:8, :].astype(jnp.int32)
    yield


def _mixer(x2, w_in_b, b_gate3, conv_w, wco_b, wpool_b, pool_scale, kt, v, wxo_b, wo_b, ln_g, ln_b, wr, br,
           bn, seq):
    n, d = x2.shape
    tm = TM_MIX
    spb = seq // tm
    nt = bn * spb
    const2 = lambda t: (0, 0)
    const3 = lambda t: (0, 0, 0)
    one = pl.Buffered(1)
    mixed = lambda t: jnp.minimum(t, nt - 1)
    routed = lambda t: jnp.maximum(t - 1, 0)
    mem_map = lambda t: (mixed(t) // spb, 0, 0)
    row_map = lambda t: (routed(t), 0)
    return pl.pallas_call(
        functools.partial(_mixer_kernel, tiles_per_seq=spb),
        grid=(nt + 1,),
        in_specs=[
            pl.BlockSpec((tm, d), lambda t: (mixed(t), 0)),
            pl.BlockSpec(w_in_b.shape, const2, pipeline_mode=one),
            pl.BlockSpec(b_gate3.shape, const2, pipeline_mode=one),
            pl.BlockSpec(conv_w.shape, const2, pipeline_mode=one),
            pl.BlockSpec(wco_b.shape, const2, pipeline_mode=one),
            pl.BlockSpec(wpool_b.shape, const3, pipeline_mode=one),
            pl.BlockSpec(pool_scale.shape, const2, pipeline_mode=one),
            pl.BlockSpec((1,) + kt.shape[1:], mem_map),
            pl.BlockSpec((1,) + v.shape[1:], mem_map),
            pl.BlockSpec(wxo_b.shape, const2, pipeline_mode=one),
            pl.BlockSpec(wo_b.shape, const2, pipeline_mode=one),
            pl.BlockSpec(ln_g.shape, const2, pipeline_mode=one),
            pl.BlockSpec(ln_b.shape, const2, pipeline_mode=one),
            pl.BlockSpec(wr.shape, const2, pipeline_mode=one),
            pl.BlockSpec(br.shape, const2, pipeline_mode=one),
        ],
        out_specs=[
            pl.BlockSpec((tm * SUB, LANES), row_map),
            pl.BlockSpec((tm, LANES), row_map),
            pl.BlockSpec((1, 8, tm), lambda t: (routed(t), 0, 0)),
            pl.BlockSpec((8, LANES), const2),
        ],
        out_shape=[
            jax.ShapeDtypeStruct((n * SUB, LANES), F32),
            jax.ShapeDtypeStruct((n, LANES), F32),
            jax.ShapeDtypeStruct((n // tm, 8, tm), jnp.int32),
            jax.ShapeDtypeStruct((8, LANES), jnp.int32),
        ],
        scratch_shapes=[
            pltpu.VMEM((tm + CONV_HALO, d), F32),
            pltpu.VMEM((POOL_PAD + POOL_HALO + tm, d), F32),
            pltpu.VMEM((POOL_PAD + POOL_HALO + tm, d - POOL_GROUP_DIM), F32),
            pltpu.VMEM((POOL_PAD + POOL_HALO + tm, d - 2 * POOL_GROUP_DIM), F32),
            pltpu.VMEM((POOL_PAD + POOL_HALO + tm, d - 3 * POOL_GROUP_DIM), F32),
            pltpu.VMEM((8, LANES), F32),
            pltpu.VMEM((tm, d), F32),
        ],
        compiler_params=pltpu.CompilerParams(dimension_semantics=("arbitrary",), vmem_limit_bytes=VMEM_LIMIT),
        name="mixer",
    )(x2, w_in_b, b_gate3, conv_w, wco_b, wpool_b, pool_scale, kt, v, wxo_b, wo_b, ln_g, ln_b, wr, br)


def _dest_kernel(ps_ref, ri_ref, dest_ref):
    dest_ref[...] = jnp.zeros(dest_ref.shape, jnp.int32)
    for t in range(dest_ref.shape[0]):
        ri = ri_ref[t]
        e = ri[0:TOP_K, :]
        start = jnp.zeros_like(e)
        for j in range(N_EXPERTS):
            start = jnp.where(e == j, ps_ref[j], start)
        dest_ref[t, 0:TOP_K, :] = start + ri[TOP_K:2 * TOP_K, :]


def _dest(pstarts, route_i):
    nt, rows, tm = route_i.shape
    blk = 8
    spec = pl.BlockSpec((blk, rows, tm), lambda i, ps: (i, 0, 0))
    return pl.pallas_call(
        _dest_kernel,
        grid_spec=pltpu.PrefetchScalarGridSpec(
            num_scalar_prefetch=1, grid=(nt // blk,), in_specs=[spec], out_specs=spec),
        out_shape=jax.ShapeDtypeStruct(route_i.shape, jnp.int32),
        compiler_params=pltpu.CompilerParams(dimension_semantics=("arbitrary",)),
        name="dest_rows",
    )(pstarts, route_i)


def _scatter_kernel(zt_ref, dest_ref, h_ref, xbuf_ref, zbuf, stage, zsem, fsem, ssem):
    i = pl.program_id(0)
    nsteps = pl.num_programs(0)
    tm = TM_ROW
    rows = tm * SUB
    slot = lax.rem(i, SCATTER_RING)

    def zero_copy(e):
        start = pl.multiple_of(zt_ref[e] * SUB, TM_EXP * SUB)
        return pltpu.make_async_copy(zbuf, xbuf_ref.at[pl.ds(start, TM_EXP * SUB), :], zsem.at[0])

    def fetch(t, s):
        src = h_ref.at[pl.ds(pl.multiple_of(t * rows, rows), rows), :]
        return pltpu.make_async_copy(src, stage.at[s], fsem.at[s])

    def drain(s):
        for k in range(TOP_K):
            pltpu.make_async_copy(stage.at[s], xbuf_ref.at[pl.ds(0, rows), :], ssem.at[s, k]).wait()

    @pl.when(i == 0)
    def _():
        zbuf[...] = jnp.zeros_like(zbuf)

        def start(e, c):
            @pl.when(zt_ref[e] >= 0)
            def _():
                zero_copy(e).start()
            return c

        def wait(e, c):
            @pl.when(zt_ref[e] >= 0)
            def _():
                zero_copy(e).wait()
            return c

        lax.fori_loop(0, 2 * N_EXPERTS, start, 0)
        fetch(0, 0).start()
        lax.fori_loop(0, 2 * N_EXPERTS, wait, 0)

    @pl.when(i >= 2)
    def _():
        drain(lax.rem(i + 1, SCATTER_RING))

    @pl.when(i + 1 < nsteps)
    def _():
        fetch(i + 1, lax.rem(i + 1, SCATTER_RING)).start()

    fetch(i, slot).wait()

    def issue(r, c):
        for k in range(TOP_K):
            _tile_copy(stage.at[slot], r, xbuf_ref, dest_ref[k * tm + r], ssem.at[slot, k]).start(priority=k)
        return c

    lax.fori_loop(0, tm, issue, 0, unroll=ISSUE_UNROLL)

    @pl.when(i == nsteps - 1)
    def _():
        @pl.when(i >= 1)
        def _():
            drain(lax.rem(i + 2, SCATTER_RING))

        drain(slot)


def _scatter(zero_tiles, dest_flat, h1t, p_rows):
    tm = TM_ROW
    nt = dest_flat.shape[0] // (TOP_K * tm)
    return pl.pallas_call(
        _scatter_kernel,
        grid_spec=pltpu.PrefetchScalarGridSpec(
            num_scalar_prefetch=1,
            grid=(nt,),
            in_specs=[
                pl.BlockSpec((TOP_K * tm,), lambda i, zt: (i,), memory_space=pltpu.SMEM),
                pl.BlockSpec(memory_space=pl.ANY),
            ],
            out_specs=pl.BlockSpec(memory_space=pl.ANY),
            scratch_shapes=[
                pltpu.VMEM((TM_EXP * SUB, LANES), F32),
                pltpu.VMEM((SCATTER_RING, tm * SUB, LANES), F32),
                pltpu.SemaphoreType.DMA((1,)),
                pltpu.SemaphoreType.DMA((SCATTER_RING,)),
                pltpu.SemaphoreType.DMA((SCATTER_RING, TOP_K)),
            ],
        ),
        out_shape=jax.ShapeDtypeStruct((p_rows * SUB, LANES), F32),
        compiler_params=pltpu.CompilerParams(dimension_semantics=("arbitrary",)),
        name="scatter_rows",
    )(zero_tiles, dest_flat, h1t)


def _expert_kernel(t0_ref, nt_ref, tot_ref, wup_ref, wdn_ref, xbuf_ref, ybuf_ref, wup_b, wdn_b, xs, ys, sem_in, sem_out):
    e = pl.program_id(0)
    t0 = t0_ref[e]
    nt = nt_ref[e]
    total = tot_ref[0]
    rows = TM_EXP * SUB
    max_tiles = xbuf_ref.shape[0] // rows

    def hbm_tile(ref, g):
        return ref.at[pl.ds(pl.multiple_of(g * rows, rows), rows), :]

    def x_copy(g, slot):
        return pltpu.make_async_copy(hbm_tile(xbuf_ref, g), xs.at[slot], sem_in.at[slot])

    def y_copy(g, slot):
        return pltpu.make_async_copy(ys.at[slot], hbm_tile(ybuf_ref, g), sem_out.at[slot])

    ahead = RING - 1

    @pl.when(e == 0)
    def _():
        for g0 in range(ahead):
            @pl.when(g0 < total)
            def _():
                x_copy(g0, g0).start(priority=ROW_DMA_PRIORITY)

    @pl.when(nt > 0)
    def _():
        wup_b[...] = wup_ref[0].astype(BF16)
        wdn_b[...] = wdn_ref[0].astype(BF16)

    def tile(j, c):
        g = t0 + j
        slot = lax.rem(g, RING)

        @pl.when(g + ahead < total)
        def _():
            x_copy(g + ahead, lax.rem(g + ahead, RING)).start(priority=ROW_DMA_PRIORITY)

        x_copy(g, slot).wait()

        @pl.when(g >= RING)
        def _():
            y_copy(g - RING, slot).wait()

        xb = _tok_load(xs.at[slot], TM_EXP).astype(BF16)
        hgv = _dot(xb, wup_b[...])
        hg = hgv[:, :D_EXPERT]
        hv = hgv[:, D_EXPERT:]
        act = (hg * jax.nn.sigmoid(hg)) * hv
        _tok_store(ys.at[slot], _dot(act.astype(BF16), wdn_b[...]), TM_EXP)
        y_copy(g, slot).start(priority=ROW_DMA_PRIORITY)
        return c

    lax.fori_loop(0, nt, tile, 0)

    @pl.when(e == pl.num_programs(0) - 1)
    def _():
        for back in range(RING, 0, -1):
            @pl.when(total >= back)
            def _():
                y_copy(total - back, lax.rem(total - back, RING)).wait()

        ys[0] = jnp.zeros(ys.shape[1:], F32)

        def zero_tail(g, c):
            y_copy(g, 0).start()
            y_copy(g, 0).wait()
            return c

        lax.fori_loop(total, max_tiles, zero_tail, 0)


def _experts(tile_start, tile_count, n_tiles, xbuf, w_up, w_down):
    w_map = lambda e, t0, nt, tot: (e, 0, 0)
    rows = TM_EXP * SUB
    return pl.pallas_call(
        _expert_kernel,
        grid_spec=pltpu.PrefetchScalarGridSpec(
            num_scalar_prefetch=3,
            grid=(N_EXPERTS,),
            in_specs=[
                pl.BlockSpec((1,) + w_up.shape[1:], w_map),
                pl.BlockSpec((1,) + w_down.shape[1:], w_map),
                pl.BlockSpec(memory_space=pl.ANY),
            ],
            out_specs=pl.BlockSpec(memory_space=pl.ANY),
            scratch_shapes=[
                pltpu.VMEM(w_up.shape[1:], BF16), pltpu.VMEM(w_down.shape[1:], BF16),
                pltpu.VMEM((RING, rows, LANES), F32), pltpu.VMEM((RING, rows, LANES), F32),
                pltpu.SemaphoreType.DMA((RING,)), pltpu.SemaphoreType.DMA((RING,)),
            ],
        ),
        out_shape=jax.ShapeDtypeStruct(xbuf.shape, F32),
        compiler_params=pltpu.CompilerParams(dimension_semantics=("arbitrary",), vmem_limit_bytes=VMEM_LIMIT),
        name="experts",
    )(tile_start, tile_count, n_tiles, w_up, w_down, xbuf)


def _combine_kernel(dcur_ref, dnext_ref, h_ref, route_ref, ybuf_ref, g_ref, b_ref, out_ref, ybufs, sem):
    tm = TM_ROW
    i = pl.program_id(0)
    slot = lax.rem(i, 2)

    def gather(d_ref, s):
        def issue(r, c):
            for k in range(TOP_K):
                _tile_copy(ybuf_ref, d_ref[k * tm + r], ybufs.at[s, k], r, sem.at[s, k]).start(priority=k)
            return c

        lax.fori_loop(0, tm, issue, 0, unroll=ISSUE_UNROLL)

    @pl.when(i == 0)
    def _():
        gather(dcur_ref, 0)

    @pl.when(i + 1 < pl.num_programs(0))
    def _():
        gather(dnext_ref, 1 - slot)

    for k in range(TOP_K):
        pltpu.make_async_copy(ybuf_ref.at[pl.ds(0, tm * SUB), :], ybufs.at[slot, k], sem.at[slot, k]).wait()
    route = route_ref[...]
    ffn = route[:, 4:5] * _tok_load(ybufs.at[slot, 0], tm) + route[:, 5:6] * _tok_load(ybufs.at[slot, 1], tm)
    out_ref[...] = _layer_norm(ALPHA * _tok_load(h_ref, tm) + ffn, g_ref[...], b_ref[...])


def _combine(dest_flat, h1t, route, ybuf, ln_g, ln_b):
    tm = TM_ROW
    nt = dest_flat.shape[0] // (TOP_K * tm)
    n = nt * tm
    return pl.pallas_call(
        _combine_kernel,
        grid=(nt,),
        in_specs=[
            pl.BlockSpec((TOP_K * tm,), lambda i: (i,), memory_space=pltpu.SMEM),
            pl.BlockSpec((TOP_K * tm,), lambda i: (jnp.minimum(i + 1, nt - 1),), memory_space=pltpu.SMEM),
            pl.BlockSpec((tm * SUB, LANES), lambda i: (i, 0)),
            pl.BlockSpec((tm, LANES), lambda i: (i, 0)),
            pl.BlockSpec(memory_space=pl.ANY),
            pl.BlockSpec(ln_g.shape, lambda i: (0, 0)),
            pl.BlockSpec(ln_b.shape, lambda i: (0, 0)),
        ],
        out_specs=pl.BlockSpec((tm, D_MODEL), lambda i: (i, 0)),
        out_shape=jax.ShapeDtypeStruct((n, D_MODEL), F32),
        scratch_shapes=[pltpu.VMEM((2, TOP_K, tm * SUB, LANES), F32), pltpu.SemaphoreType.DMA((2, TOP_K))],
        compiler_params=pltpu.CompilerParams(dimension_semantics=("arbitrary",)),
        name="combine_ln2",
    )(dest_flat, dest_flat, h1t, route, ybuf, ln_g, ln_b)


def _layer(h, mem, w_in, b_gate, conv_w, w_conv_out, w_pool, pool_scale, w_kv, w_xo, w_o, ln1_g, ln1_b,
           w_rg, b_rg, w_re, b_re, w_up, w_down, ln2_g, ln2_b):
    bn, seq, d = h.shape
    n = bn * seq
    xw = X_HEADS * X_HEAD_DIM

    kt, v = _kv_proj(mem, w_kv[:, :xw].T.astype(BF16), w_kv[:, xw:].astype(BF16))

    pad = LANES - N_EXPERTS - N_GROUPS
    w_r = jnp.concatenate([w_re, w_rg, jnp.zeros((d, pad), F32)], axis=1)
    wr_hi = w_r.astype(BF16)
    wr_lo = (w_r - wr_hi.astype(F32)).astype(BF16)
    b_r = jnp.concatenate([b_re, b_rg, jnp.zeros((pad,), F32)])[None, :]

    h1t, route, route_i, cnt = _mixer(
        h.reshape(n, d), w_in.astype(BF16), b_gate.reshape(3, d), conv_w, w_conv_out.astype(BF16),
        w_pool.astype(BF16), pool_scale[None, :], kt, v, w_xo.astype(BF16), w_o.astype(BF16),
        ln1_g[None, :], ln1_b[None, :], jnp.concatenate([wr_hi, wr_lo], axis=1), b_r, bn, seq)

    counts = cnt[0, :N_EXPERTS]
    padded = (counts + TM_EXP - 1) // TM_EXP * TM_EXP
    pends = jnp.cumsum(padded)
    pstarts = pends - padded
    max_tiles = (n * TOP_K + N_EXPERTS * (TM_EXP - 1)) // TM_EXP
    p_rows = max_tiles * TM_EXP
    n_tiles = (pends[-1] // TM_EXP).astype(jnp.int32)
    tail_ids = n_tiles + jnp.arange(N_EXPERTS, dtype=jnp.int32)
    zero_tiles = jnp.concatenate([
        jnp.where(counts % TM_EXP != 0, pends - TM_EXP, -1),
        jnp.where(tail_ids < max_tiles, tail_ids * TM_EXP, -1)]).astype(jnp.int32)

    dest = _dest(pstarts.astype(jnp.int32), route_i)[:, :TOP_K, :].reshape(-1)
    xbuf = _scatter(zero_tiles, dest, h1t, p_rows)
    ybuf = _experts((pstarts // TM_EXP).astype(jnp.int32), (padded // TM_EXP).astype(jnp.int32), n_tiles.reshape(1),
                    xbuf, w_up, w_down)
    out = _combine(dest, h1t, route, ybuf, ln2_g[None, :], ln2_b[None, :])
    return out.reshape(bn, seq, d)


def kernel(x, mem, w_in, b_gate, conv_w, w_conv_out, w_pool, pool_scale, w_kv, w_xo, w_o, ln1_g, ln1_b,
           w_router_group, b_router_group, w_router_expert, b_router_expert, w_up, w_down, ln2_g, ln2_b):
    h = x
    for l in range(DEPTH):
        h = _layer(h, mem, w_in[l], b_gate[l], conv_w[l], w_conv_out[l], w_pool[l], pool_scale[l], w_kv[l],
                   w_xo[l], w_o[l], ln1_g[l], ln1_b[l], w_router_group[l], b_router_group[l],
                   w_router_expert[l], b_router_expert[l], w_up[l], w_down[l], ln2_g[l], ln2_b[l])
    return h
```

```python
import functools

import jax
import jax.numpy as jnp
from jax import lax
from jax.experimental import pallas as pl
from jax.experimental.pallas import tpu as pltpu

D_MODEL = 1024
CONV_K = 3
POOL_WINDOWS = (2, 4, 8, 16)
POOL_GROUP_DIM = 256
X_HEADS = 4
X_HEAD_DIM = 256
N_GROUPS = 8
EXPERTS_PER_GROUP = 8
N_EXPERTS = 64
TOP_K = 2
D_EXPERT = 512
DEPTH = 1
ALPHA = (2.0 * DEPTH) ** 0.25
LN_EPS = 1e-5

LANES = 128
SUB = 8
assert D_MODEL == SUB * LANES
POOL_HALO = 16
POOL_PAD = 8
assert POOL_WINDOWS[0] == 2 and all(b == 2 * a for a, b in zip(POOL_WINDOWS, POOL_WINDOWS[1:]))
assert POOL_PAD >= POOL_WINDOWS[-2] and POOL_HALO >= POOL_WINDOWS[-1] and len(POOL_WINDOWS) == 4
CONV_HALO = 8
TM_MIX = 512
TM_EXP = 256
TM_ROW = TM_MIX
VMEM_LIMIT = 58 * 1024 * 1024
SCATTER_RING = 3
RING = 4
ROW_DMA_PRIORITY = 1

F32 = jnp.float32
BF16 = jnp.bfloat16


def _dot(a, b):
    return jnp.dot(a, b, preferred_element_type=F32)


def _tok_load(ref, tm):
    return jnp.concatenate([ref[pl.ds(c, tm, stride=SUB), :] for c in range(SUB)], axis=1)


def _tok_store(ref, val, tm):
    for c in range(SUB):
        ref[pl.ds(c, tm, stride=SUB), :] = val[:, c * LANES:(c + 1) * LANES]


def _tile_copy(src, src_tok, dst, dst_tok, sem):
    s = pl.multiple_of(src_tok * SUB, SUB)
    t = pl.multiple_of(dst_tok * SUB, SUB)
    return pltpu.make_async_copy(src.at[pl.ds(s, SUB), :], dst.at[pl.ds(t, SUB), :], sem)


def _layer_norm(h, g, b):
    mu = jnp.mean(h, axis=-1, keepdims=True)
    c = h - mu
    var = jnp.mean(c * c, axis=-1, keepdims=True)
    return c * lax.rsqrt(var + LN_EPS) * g + b


def _kv_kernel(mem_ref, wkt_ref, wv_ref, kt_ref, v_ref):
    mb = mem_ref[0].astype(BF16)
    kt = lax.dot_general(wkt_ref[...], mb, (((1,), (1,)), ((), ())), preferred_element_type=F32)
    kt_ref[0] = kt.astype(BF16)
    v_ref[0] = _dot(mb, wv_ref[...]).astype(BF16)


def _kv_proj(mem, wkt, wv):
    bn, m, d = mem.shape
    return pl.pallas_call(
        _kv_kernel,
        grid=(bn,),
        in_specs=[
            pl.BlockSpec((1, m, d), lambda b: (b, 0, 0)),
            pl.BlockSpec((d, d), lambda b: (0, 0)),
            pl.BlockSpec((d, d), lambda b: (0, 0)),
        ],
        out_specs=[
            pl.BlockSpec((1, d, m), lambda b: (b, 0, 0)),
            pl.BlockSpec((1, m, d), lambda b: (b, 0, 0)),
        ],
        out_shape=[jax.ShapeDtypeStruct((bn, d, m), BF16), jax.ShapeDtypeStruct((bn, m, d), BF16)],
        compiler_params=pltpu.CompilerParams(dimension_semantics=("arbitrary",)),
        name="kv_proj",
    )(mem, wkt, wv)


def _mixer_kernel(x_ref, win_ref, bg_ref, cw_ref, wco_ref, wpool_ref, ps_ref, kt_ref, v_ref, wxo_ref, wo_ref,
                  g1_ref, b1_ref, wr_ref, br_ref,
                  h1_ref, route_ref, ri_ref, cnt_ref,
                  ubuf, pbuf, s1buf, s2buf, s3buf, carry, hpre, *, tiles_per_seq):
    tm = TM_MIX
    d = D_MODEL
    t = pl.program_id(0)
    s = lax.rem(t, tiles_per_seq)

    @pl.when(s == 0)
    def _():
        ubuf[0:CONV_HALO, :] = jnp.zeros((CONV_HALO, d), F32)
        pbuf[0:POOL_PAD + POOL_HALO, :] = jnp.zeros((POOL_PAD + POOL_HALO, d), F32)
        for sbuf in (s1buf, s2buf, s3buf):
            sbuf[0:POOL_PAD, :] = jnp.zeros((POOL_PAD, sbuf.shape[1]), F32)

    @pl.when(t == 0)
    def _():
        carry[...] = jnp.zeros_like(carry)
        hpre[...] = jnp.zeros_like(hpre)

    route_args = (hpre, g1_ref, b1_ref, wr_ref, br_ref, h1_ref, route_ref, ri_ref, cnt_ref, carry)
    last = pl.num_programs(0) - 1

    @pl.when(t < last)
    def _():
        _mix_tile(t, s, x_ref, win_ref, bg_ref, cw_ref, wco_ref, wpool_ref, ps_ref, kt_ref, v_ref, wxo_ref, wo_ref,
                  ubuf, pbuf, s1buf, s2buf, s3buf, route_args)

    @pl.when(t == last)
    def _():
        for _ in _route_stages(True, *route_args):
            pass


def _mix_tile(t, s, x_ref, win_ref, bg_ref, cw_ref, wco_ref, wpool_ref, ps_ref, kt_ref, v_ref, wxo_ref, wo_ref,
              ubuf, pbuf, s1buf, s2buf, s3buf, route_args):
    tm = TM_MIX
    d = D_MODEL
    hpre = route_args[0]
    routing = _route_stages(t >= 1, *route_args)

    x = x_ref[...]
    xb = x.astype(BF16)

    def proj(sec):
        return _dot(xb, win_ref[:, sec * d:(sec + 1) * d])

    def gate(i):
        return jax.nn.sigmoid(proj(5 + i) + bg_ref[i:i + 1, :])

    u = proj(1) * proj(2)
    next(routing)
    ubuf[CONV_HALO:CONV_HALO + tm, :] = u
    v = cw_ref[CONV_K - 1:CONV_K, :] * u
    for k in range(CONV_K - 1):
        off = CONV_HALO - (CONV_K - 1) + k
        v = v + cw_ref[k:k + 1, :] * ubuf[off:off + tm, :]
    ubuf[0:CONV_HALO, :] = ubuf[tm:tm + CONV_HALO, :]
    y_conv = _dot((proj(0) * v).astype(BF16), wco_ref[...])
    acc = gate(0) * y_conv
    next(routing)

    p = proj(3)
    base = POOL_PAD + POOL_HALO
    n = POOL_HALO + tm
    pbuf[base:base + tm, :] = p
    lvl = pbuf[POOL_PAD:POOL_PAD + n, :] + pbuf[POOL_PAD - 1:POOL_PAD - 1 + n, :]
    sums = [lvl[POOL_HALO:, :POOL_GROUP_DIM]]
    for gi, sbuf in enumerate((s1buf, s2buf, s3buf), start=1):
        shift = POOL_WINDOWS[gi - 1]
        rest = lvl[:, POOL_GROUP_DIM:]
        sbuf[POOL_PAD:POOL_PAD + n, :] = rest
        lvl = rest + sbuf[POOL_PAD - shift:POOL_PAD - shift + n, :]
        sums.append(lvl[POOL_HALO:, :POOL_GROUP_DIM])
    pos = (s * tm + 1 + lax.broadcasted_iota(jnp.int32, (tm, 1), 0)).astype(F32)
    ys = []
    for gi, w in enumerate(POOL_WINDOWS):
        c0 = gi * POOL_GROUP_DIM
        inv_cnt = 1.0 / jnp.minimum(pos, float(w))
        dg = (sums[gi] * inv_cnt - p[:, c0:c0 + POOL_GROUP_DIM]).astype(BF16)
        ys.append(_dot(dg, wpool_ref[gi]))
    pbuf[POOL_PAD:base, :] = pbuf[POOL_PAD + tm:base + tm, :]
    y_pool = jnp.concatenate(ys, axis=1) * ps_ref[...]
    acc = acc + gate(1) * y_pool
    next(routing)

    qb = proj(4).astype(BF16)
    scale = X_HEAD_DIM ** -0.5
    os_ = []
    for h in range(X_HEADS):
        c0 = h * X_HEAD_DIM
        c1 = c0 + X_HEAD_DIM
        sc = _dot(qb[:, c0:c1], kt_ref[0, c0:c1, :]) * scale
        e = jnp.exp(sc - jnp.max(sc, axis=-1, keepdims=True))
        a = e * (1.0 / jnp.sum(e, axis=-1, keepdims=True))
        os_.append(_dot(a.astype(BF16), v_ref[0, :, c0:c1]))
    y_mem = _dot(jnp.concatenate(os_, axis=1).astype(BF16), wxo_ref[...])
    acc = acc + gate(2) * y_mem
    next(routing)

    hpre[...] = ALPHA * x + _dot(acc.astype(BF16), wo_ref[...])


def _route_stages(valid, hpre, g1_ref, b1_ref, wr_ref, br_ref, h1_ref, route_ref, ri_ref, cnt_ref, carry):
    tm = TM_MIX
    h1 = _layer_norm(hpre[...], g1_ref[...], b1_ref[...])
    _tok_store(h1_ref, h1, tm)
    yield

    hh = h1.astype(BF16)
    hl = (h1 - hh.astype(F32)).astype(BF16)
    wrh = wr_ref[:, :LANES]
    logits = _dot(hh, wrh) + _dot(hl, wrh) + _dot(hh, wr_ref[:, LANES:]) + br_ref[...]
    yield
    lane = lax.broadcasted_iota(jnp.int32, (tm, LANES), 1)
    lane_f = lane.astype(F32)
    neg = jnp.float32(-jnp.inf)
    big = jnp.float32(1e9)

    def first_argmax(vals):
        m = jnp.max(vals, axis=-1, keepdims=True)
        idx = jnp.min(jnp.where(vals == m, lane_f, big), axis=-1, keepdims=True)
        return m, idx

    is_g = (lane >= N_EXPERTS) & (lane < N_EXPERTS + N_GROUPS)
    gmax, gidx = first_argmax(jnp.where(is_g, logits, neg))
    g_w = 1.0 / jnp.sum(jnp.where(is_g, jnp.exp(logits - gmax), 0.0), axis=-1, keepdims=True)
    gsel = gidx.astype(jnp.int32) - N_EXPERTS
    in_grp = (lane >> 3) == gsel
    le = jnp.where(in_grp, logits, neg)
    m1, i1 = first_argmax(le)
    m2, i2 = first_argmax(jnp.where(lane_f == i1, neg, le))
    t = jnp.exp(m2 - m1)
    den = 1.0 + t
    w1 = g_w / den
    w2 = g_w * t / den
    yield

    sel1 = lane_f == i1
    sel2 = lane_f == i2
    onehot = (sel1 | sel2).astype(BF16)
    row_i = lax.broadcasted_iota(jnp.int32, (tm, tm), 0)
    col_i = lax.broadcasted_iota(jnp.int32, (tm, tm), 1)
    before = (col_i < row_i).astype(BF16)
    prior = _dot(before, onehot) + carry[0:1, :]
    r1 = jnp.sum(jnp.where(sel1, prior, 0.0), axis=-1, keepdims=True)
    r2 = jnp.sum(jnp.where(sel2, prior, 0.0), axis=-1, keepdims=True)
    carry[...] = carry[...] + jnp.where(valid, jnp.sum(onehot.astype(F32), axis=0, keepdims=True), 0.0)
    cnt_ref[...] = carry[...].astype(jnp.int32)

    route = jnp.where(lane == 0, i1, 0.0)
    for k, col in enumerate((i2, r1, r2, w1, w2), start=1):
        route = jnp.where(lane == k, col, route)
    route_ref[...] = route
    ri_ref[0] = route.T[0:8, :].astype(jnp.int32)
    yield


def _mixer(x2, w_in_b, b_gate3, conv_w, wco_b, wpool_b, pool_scale, kt, v, wxo_b, wo_b, ln_g, ln_b, wr, br,
           bn, seq):
    n, d = x2.shape
    tm = TM_MIX
    spb = seq // tm
    nt = bn * spb
    const2 = lambda t: (0, 0)
    const3 = lambda t: (0, 0, 0)
    one = pl.Buffered(1)
    mixed = lambda t: jnp.minimum(t, nt - 1)
    routed = lambda t: jnp.maximum(t - 1, 0)
    mem_map = lambda t: (mixed(t) // spb, 0, 0)
    row_map = lambda t: (routed(t), 0)
    return pl.pallas_call(
        functools.partial(_mixer_kernel, tiles_per_seq=spb),
        grid=(nt + 1,),
        in_specs=[
            pl.BlockSpec((tm, d), lambda t: (mixed(t), 0)),
            pl.BlockSpec(w_in_b.shape, const2, pipeline_mode=one),
            pl.BlockSpec(b_gate3.shape, const2, pipeline_mode=one),
            pl.BlockSpec(conv_w.shape, const2, pipeline_mode=one),
            pl.BlockSpec(wco_b.shape, const2, pipeline_mode=one),
            pl.BlockSpec(wpool_b.shape, const3, pipeline_mode=one),
            pl.BlockSpec(pool_scale.shape, const2, pipeline_mode=one),
            pl.BlockSpec((1,) + kt.shape[1:], mem_map),
            pl.BlockSpec((1,) + v.shape[1:], mem_map),
            pl.BlockSpec(wxo_b.shape, const2, pipeline_mode=one),
            pl.BlockSpec(wo_b.shape, const2, pipeline_mode=one),
            pl.BlockSpec(ln_g.shape, const2, pipeline_mode=one),
            pl.BlockSpec(ln_b.shape, const2, pipeline_mode=one),
            pl.BlockSpec(wr.shape, const2, pipeline_mode=one),
            pl.BlockSpec(br.shape, const2, pipeline_mode=one),
        ],
        out_specs=[
            pl.BlockSpec((tm * SUB, LANES), row_map),
            pl.BlockSpec((tm, LANES), row_map),
            pl.BlockSpec((1, 8, tm), lambda t: (routed(t), 0, 0)),
            pl.BlockSpec((8, LANES), const2),
        ],
        out_shape=[
            jax.ShapeDtypeStruct((n * SUB, LANES), F32),
            jax.ShapeDtypeStruct((n, LANES), F32),
            jax.ShapeDtypeStruct((n // tm, 8, tm), jnp.int32),
            jax.ShapeDtypeStruct((8, LANES), jnp.int32),
        ],
        scratch_shapes=[
            pltpu.VMEM((tm + CONV_HALO, d), F32),
            pltpu.VMEM((POOL_PAD + POOL_HALO + tm, d), F32),
            pltpu.VMEM((POOL_PAD + POOL_HALO + tm, d - POOL_GROUP_DIM), F32),
            pltpu.VMEM((POOL_PAD + POOL_HALO + tm, d - 2 * POOL_GROUP_DIM), F32),
            pltpu.VMEM((POOL_PAD + POOL_HALO + tm, d - 3 * POOL_GROUP_DIM), F32),
            pltpu.VMEM((8, LANES), F32),
            pltpu.VMEM((tm, d), F32),
        ],
        compiler_params=pltpu.CompilerParams(dimension_semantics=("arbitrary",), vmem_limit_bytes=VMEM_LIMIT),
        name="mixer",
    )(x2, w_in_b, b_gate3, conv_w, wco_b, wpool_b, pool_scale, kt, v, wxo_b, wo_b, ln_g, ln_b, wr, br)


def _dest_kernel(ps_ref, ri_ref, dest_ref):
    dest_ref[...] = jnp.zeros(dest_ref.shape, jnp.int32)
    for t in range(dest_ref.shape[0]):
        ri = ri_ref[t]
        e = ri[0:TOP_K, :]
        start = jnp.zeros_like(e)
        for j in range(N_EXPERTS):
            start = jnp.where(e == j, ps_ref[j], start)
        dest_ref[t, 0:TOP_K, :] = start + ri[TOP_K:2 * TOP_K, :]


def _dest(pstarts, route_i):
    nt, rows, tm = route_i.shape
    blk = 8
    spec = pl.BlockSpec((blk, rows, tm), lambda i, ps: (i, 0, 0))
    return pl.pallas_call(
        _dest_kernel,
        grid_spec=pltpu.PrefetchScalarGridSpec(
            num_scalar_prefetch=1, grid=(nt // blk,), in_specs=[spec], out_specs=spec),
        out_shape=jax.ShapeDtypeStruct(route_i.shape, jnp.int32),
        compiler_params=pltpu.CompilerParams(dimension_semantics=("arbitrary",)),
        name="dest_rows",
    )(pstarts, route_i)


def _scatter_kernel(zt_ref, dest_ref, h_ref, xbuf_ref, zbuf, stage, zsem, fsem, ssem):
    i = pl.program_id(0)
    nsteps = pl.num_programs(0)
    tm = TM_ROW
    rows = tm * SUB
    slot = lax.rem(i, SCATTER_RING)

    def zero_copy(e):
        start = pl.multiple_of(zt_ref[e] * SUB, TM_EXP * SUB)
        return pltpu.make_async_copy(zbuf, xbuf_ref.at[pl.ds(start, TM_EXP * SUB), :], zsem.at[0])

    def fetch(t, s):
        src = h_ref.at[pl.ds(pl.multiple_of(t * rows, rows), rows), :]
        return pltpu.make_async_copy(src, stage.at[s], fsem.at[s])

    def drain(s):
        for k in range(TOP_K):
            pltpu.make_async_copy(stage.at[s], xbuf_ref.at[pl.ds(0, rows), :], ssem.at[s, k]).wait()

    @pl.when(i == 0)
    def _():
        zbuf[...] = jnp.zeros_like(zbuf)

        def start(e, c):
            @pl.when(zt_ref[e] >= 0)
            def _():
                zero_copy(e).start()
            return c

        def wait(e, c):
            @pl.when(zt_ref[e] >= 0)
            def _():
                zero_copy(e).wait()
            return c

        lax.fori_loop(0, 2 * N_EXPERTS, start, 0)
        fetch(0, 0).start()
        lax.fori_loop(0, 2 * N_EXPERTS, wait, 0)

    @pl.when(i >= 2)
    def _():
        drain(lax.rem(i + 1, SCATTER_RING))

    @pl.when(i + 1 < nsteps)
    def _():
        fetch(i + 1, lax.rem(i + 1, SCATTER_RING)).start()

    fetch(i, slot).wait()

    def issue(r, c):
        for k in range(TOP_K):
            _tile_copy(stage.at[slot], r, xbuf_ref, dest_ref[k * tm + r], ssem.at[slot, k]).start(priority=k)
        return c

    lax.fori_loop(0, tm, issue, 0, unroll=8)

    @pl.when(i == nsteps - 1)
    def _():
        @pl.when(i >= 1)
        def _():
            drain(lax.rem(i + 2, SCATTER_RING))

        drain(slot)


def _scatter(zero_tiles, dest_flat, h1t, p_rows):
    tm = TM_ROW
    nt = dest_flat.shape[0] // (TOP_K * tm)
    return pl.pallas_call(
        _scatter_kernel,
        grid_spec=pltpu.PrefetchScalarGridSpec(
            num_scalar_prefetch=1,
            grid=(nt,),
            in_specs=[
                pl.BlockSpec((TOP_K * tm,), lambda i, zt: (i,), memory_space=pltpu.SMEM),
                pl.BlockSpec(memory_space=pl.ANY),
            ],
            out_specs=pl.BlockSpec(memory_space=pl.ANY),
            scratch_shapes=[
                pltpu.VMEM((TM_EXP * SUB, LANES), F32),
                pltpu.VMEM((SCATTER_RING, tm * SUB, LANES), F32),
                pltpu.SemaphoreType.DMA((1,)),
                pltpu.SemaphoreType.DMA((SCATTER_RING,)),
                pltpu.SemaphoreType.DMA((SCATTER_RING, TOP_K)),
            ],
        ),
        out_shape=jax.ShapeDtypeStruct((p_rows * SUB, LANES), F32),
        compiler_params=pltpu.CompilerParams(dimension_semantics=("arbitrary",)),
        name="scatter_rows",
    )(zero_tiles, dest_flat, h1t)


def _expert_kernel(t0_ref, nt_ref, tot_ref, wup_ref, wdn_ref, xbuf_ref, ybuf_ref, wup_b, wdn_b, xs, ys, sem_in, sem_out):
    e = pl.program_id(0)
    t0 = t0_ref[e]
    nt = nt_ref[e]
    total = tot_ref[0]
    rows = TM_EXP * SUB
    max_tiles = xbuf_ref.shape[0] // rows

    def hbm_tile(ref, g):
        return ref.at[pl.ds(pl.multiple_of(g * rows, rows), rows), :]

    def x_copy(g, slot):
        return pltpu.make_async_copy(hbm_tile(xbuf_ref, g), xs.at[slot], sem_in.at[slot])

    def y_copy(g, slot):
        return pltpu.make_async_copy(ys.at[slot], hbm_tile(ybuf_ref, g), sem_out.at[slot])

    ahead = RING - 1

    @pl.when(e == 0)
    def _():
        for g0 in range(ahead):
            @pl.when(g0 < total)
            def _():
                x_copy(g0, g0).start(priority=ROW_DMA_PRIORITY)

    @pl.when(nt > 0)
    def _():
        wup_b[...] = wup_ref[0].astype(BF16)
        wdn_b[...] = wdn_ref[0].astype(BF16)

    def tile(j, c):
        g = t0 + j
        slot = lax.rem(g, RING)

        @pl.when(g + ahead < total)
        def _():
            x_copy(g + ahead, lax.rem(g + ahead, RING)).start(priority=ROW_DMA_PRIORITY)

        x_copy(g, slot).wait()

        @pl.when(g >= RING)
        def _():
            y_copy(g - RING, slot).wait()

        xb = _tok_load(xs.at[slot], TM_EXP).astype(BF16)
        hgv = _dot(xb, wup_b[...])
        hg = hgv[:, :D_EXPERT]
        hv = hgv[:, D_EXPERT:]
        act = (hg * jax.nn.sigmoid(hg)) * hv
        _tok_store(ys.at[slot], _dot(act.astype(BF16), wdn_b[...]), TM_EXP)
        y_copy(g, slot).start(priority=ROW_DMA_PRIORITY)
        return c

    lax.fori_loop(0, nt, tile, 0)

    @pl.when(e == pl.num_programs(0) - 1)
    def _():
        for back in range(RING, 0, -1):
            @pl.when(total >= back)
            def _():
                y_copy(total - back, lax.rem(total - back, RING)).wait()

        ys[0] = jnp.zeros(ys.shape[1:], F32)

        def zero_tail(g, c):
            y_copy(g, 0).start()
            y_copy(g, 0).wait()
            return c

        lax.fori_loop(total, max_tiles, zero_tail, 0)


def _experts(tile_start, tile_count, n_tiles, xbuf, w_up, w_down):
    w_map = lambda e, t0, nt, tot: (e, 0, 0)
    rows = TM_EXP * SUB
    return pl.pallas_call(
        _expert_kernel,
        grid_spec=pltpu.PrefetchScalarGridSpec(
            num_scalar_prefetch=3,
            grid=(N_EXPERTS,),
            in_specs=[
                pl.BlockSpec((1,) + w_up.shape[1:], w_map),
                pl.BlockSpec((1,) + w_down.shape[1:], w_map),
                pl.BlockSpec(memory_space=pl.ANY),
            ],
            out_specs=pl.BlockSpec(memory_space=pl.ANY),
            scratch_shapes=[
                pltpu.VMEM(w_up.shape[1:], BF16), pltpu.VMEM(w_down.shape[1:], BF16),
                pltpu.VMEM((RING, rows, LANES), F32), pltpu.VMEM((RING, rows, LANES), F32),
                pltpu.SemaphoreType.DMA((RING,)), pltpu.SemaphoreType.DMA((RING,)),
            ],
        ),
        out_shape=jax.ShapeDtypeStruct(xbuf.shape, F32),
        compiler_params=pltpu.CompilerParams(dimension_semantics=("arbitrary",), vmem_limit_bytes=VMEM_LIMIT),
        name="experts",
    )(tile_start, tile_count, n_tiles, w_up, w_down, xbuf)


def _combine_kernel(dcur_ref, dnext_ref, h_ref, route_ref, ybuf_ref, g_ref, b_ref, out_ref, ybufs, sem):
    tm = TM_ROW
    i = pl.program_id(0)
    slot = lax.rem(i, 2)

    def gather(d_ref, s):
        def issue(r, c):
            for k in range(TOP_K):
                _tile_copy(ybuf_ref, d_ref[k * tm + r], ybufs.at[s, k], r, sem.at[s, k]).start(priority=k)
            return c

        lax.fori_loop(0, tm, issue, 0, unroll=8)

    @pl.when(i == 0)
    def _():
        gather(dcur_ref, 0)

    @pl.when(i + 1 < pl.num_programs(0))
    def _():
        gather(dnext_ref, 1 - slot)

    for k in range(TOP_K):
        pltpu.make_async_copy(ybuf_ref.at[pl.ds(0, tm * SUB), :], ybufs.at[slot, k], sem.at[slot, k]).wait()
    route = route_ref[...]
    ffn = route[:, 4:5] * _tok_load(ybufs.at[slot, 0], tm) + route[:, 5:6] * _tok_load(ybufs.at[slot, 1], tm)
    out_ref[...] = _layer_norm(ALPHA * _tok_load(h_ref, tm) + ffn, g_ref[...], b_ref[...])


def _combine(dest_flat, h1t, route, ybuf, ln_g, ln_b):
    tm = TM_ROW
    nt = dest_flat.shape[0] // (TOP_K * tm)
    n = nt * tm
    return pl.pallas_call(
        _combine_kernel,
        grid=(nt,),
        in_specs=[
            pl.BlockSpec((TOP_K * tm,), lambda i: (i,), memory_space=pltpu.SMEM),
            pl.BlockSpec((TOP_K * tm,), lambda i: (jnp.minimum(i + 1, nt - 1),), memory_space=pltpu.SMEM),
            pl.BlockSpec((tm * SUB, LANES), lambda i: (i, 0)),
            pl.BlockSpec((tm, LANES), lambda i: (i, 0)),
            pl.BlockSpec(memory_space=pl.ANY),
            pl.BlockSpec(ln_g.shape, lambda i: (0, 0)),
            pl.BlockSpec(ln_b.shape, lambda i: (0, 0)),
        ],
        out_specs=pl.BlockSpec((tm, D_MODEL), lambda i: (i, 0)),
        out_shape=jax.ShapeDtypeStruct((n, D_MODEL), F32),
        scratch_shapes=[pltpu.VMEM((2, TOP_K, tm * SUB, LANES), F32), pltpu.SemaphoreType.DMA((2, TOP_K))],
        compiler_params=pltpu.CompilerParams(dimension_semantics=("arbitrary",)),
        name="combine_ln2",
    )(dest_flat, dest_flat, h1t, route, ybuf, ln_g, ln_b)


def _layer(h, mem, w_in, b_gate, conv_w, w_conv_out, w_pool, pool_scale, w_kv, w_xo, w_o, ln1_g, ln1_b,
           w_rg, b_rg, w_re, b_re, w_up, w_down, ln2_g, ln2_b):
    bn, seq, d = h.shape
    n = bn * seq
    xw = X_HEADS * X_HEAD_DIM

    kt, v = _kv_proj(mem, w_kv[:, :xw].T.astype(BF16), w_kv[:, xw:].astype(BF16))

    pad = LANES - N_EXPERTS - N_GROUPS
    w_r = jnp.concatenate([w_re, w_rg, jnp.zeros((d, pad), F32)], axis=1)
    wr_hi = w_r.astype(BF16)
    wr_lo = (w_r - wr_hi.astype(F32)).astype(BF16)
    b_r = jnp.concatenate([b_re, b_rg, jnp.zeros((pad,), F32)])[None, :]

    h1t, route, route_i, cnt = _mixer(
        h.reshape(n, d), w_in.astype(BF16), b_gate.reshape(3, d), conv_w, w_conv_out.astype(BF16),
        w_pool.astype(BF16), pool_scale[None, :], kt, v, w_xo.astype(BF16), w_o.astype(BF16),
        ln1_g[None, :], ln1_b[None, :], jnp.concatenate([wr_hi, wr_lo], axis=1), b_r, bn, seq)

    counts = cnt[0, :N_EXPERTS]
    padded = (counts + TM_EXP - 1) // TM_EXP * TM_EXP
    pends = jnp.cumsum(padded)
    pstarts = pends - padded
    max_tiles = (n * TOP_K + N_EXPERTS * (TM_EXP - 1)) // TM_EXP
    p_rows = max_tiles * TM_EXP
    n_tiles = (pends[-1] // TM_EXP).astype(jnp.int32)
    tail_ids = n_tiles + jnp.arange(N_EXPERTS, dtype=jnp.int32)
    zero_tiles = jnp.concatenate([
        jnp.where(counts % TM_EXP != 0, pends - TM_EXP, -1),
        jnp.where(tail_ids < max_tiles, tail_ids * TM_EXP, -1)]).astype(jnp.int32)

    dest = _dest(pstarts.astype(jnp.int32), route_i)[:, :TOP_K, :].reshape(-1)
    xbuf = _scatter(zero_tiles, dest, h1t, p_rows)
    ybuf = _experts((pstarts // TM_EXP).astype(jnp.int32), (padded // TM_EXP).astype(jnp.int32), n_tiles.reshape(1),
                    xbuf, w_up, w_down)
    out = _combine(dest, h1t, route, ybuf, ln2_g[None, :], ln2_b[None, :])
    return out.reshape(bn, seq, d)


def kernel(x, mem, w_in, b_gate, conv_w, w_conv_out, w_pool, pool_scale, w_kv, w_xo, w_o, ln1_g, ln1_b,
           w_router_group, b_router_group, w_router_expert, b_router_expert, w_up, w_down, ln2_g, ln2_b):
    h = x
    for l in range(DEPTH):
        h = _layer(h, mem, w_in[l], b_gate[l], conv_w[l], w_conv_out[l], w_pool[l], pool_scale[l], w_kv[l],
                   w_xo[l], w_o[l], ln1_g[l], ln1_b[l], w_router_group[l], b_router_group[l],
                   w_router_expert[l], b_router_expert[l], w_up[l], w_down[l], ln2_g[l], ln2_b[l])
    return h
```

```python
import functools

import jax
import jax.numpy as jnp
from jax import lax
from jax.experimental import pallas as pl
from jax.experimental.pallas import tpu as pltpu

D_MODEL = 1024
CONV_K = 3
POOL_WINDOWS = (2, 4, 8, 16)
POOL_GROUP_DIM = 256
X_HEADS = 4
X_HEAD_DIM = 256
N_GROUPS = 8
EXPERTS_PER_GROUP = 8
N_EXPERTS = 64
TOP_K = 2
D_EXPERT = 512
DEPTH = 1
ALPHA = (2.0 * DEPTH) ** 0.25
LN_EPS = 1e-5

LANES = 128
SUB = 8
assert D_MODEL == SUB * LANES
POOL_HALO = 16
POOL_PAD = 8
assert POOL_WINDOWS[0] == 2 and all(b == 2 * a for a, b in zip(POOL_WINDOWS, POOL_WINDOWS[1:]))
assert POOL_PAD >= POOL_WINDOWS[-2] and POOL_HALO >= POOL_WINDOWS[-1] and len(POOL_WINDOWS) == 4
CONV_HALO = 8
TM_MIX = 512
TM_EXP = 256
TM_ROW = TM_MIX
VMEM_LIMIT = 58 * 1024 * 1024
SCATTER_RING = 3
RING = 4
ROW_DMA_PRIORITY = 1

F32 = jnp.float32
BF16 = jnp.bfloat16


def _dot(a, b):
    return jnp.dot(a, b, preferred_element_type=F32)


def _tok_load(ref, tm):
    return jnp.concatenate([ref[pl.ds(c, tm, stride=SUB), :] for c in range(SUB)], axis=1)


def _tok_store(ref, val, tm):
    for c in range(SUB):
        ref[pl.ds(c, tm, stride=SUB), :] = val[:, c * LANES:(c + 1) * LANES]


def _tile_copy(src, src_tok, dst, dst_tok, sem):
    s = pl.multiple_of(src_tok * SUB, SUB)
    t = pl.multiple_of(dst_tok * SUB, SUB)
    return pltpu.make_async_copy(src.at[pl.ds(s, SUB), :], dst.at[pl.ds(t, SUB), :], sem)


def _layer_norm(h, g, b):
    mu = jnp.mean(h, axis=-1, keepdims=True)
    c = h - mu
    var = jnp.mean(c * c, axis=-1, keepdims=True)
    return c * lax.rsqrt(var + LN_EPS) * g + b


def _kv_kernel(mem_ref, wkt_ref, wv_ref, kt_ref, v_ref):
    mb = mem_ref[0].astype(BF16)
    kt = lax.dot_general(wkt_ref[...], mb, (((1,), (1,)), ((), ())), preferred_element_type=F32)
    kt_ref[0] = kt.astype(BF16)
    v_ref[0] = _dot(mb, wv_ref[...]).astype(BF16)


def _kv_proj(mem, wkt, wv):
    bn, m, d = mem.shape
    return pl.pallas_call(
        _kv_kernel,
        grid=(bn,),
        in_specs=[
            pl.BlockSpec((1, m, d), lambda b: (b, 0, 0)),
            pl.BlockSpec((d, d), lambda b: (0, 0)),
            pl.BlockSpec((d, d), lambda b: (0, 0)),
        ],
        out_specs=[
            pl.BlockSpec((1, d, m), lambda b: (b, 0, 0)),
            pl.BlockSpec((1, m, d), lambda b: (b, 0, 0)),
        ],
        out_shape=[jax.ShapeDtypeStruct((bn, d, m), BF16), jax.ShapeDtypeStruct((bn, m, d), BF16)],
        compiler_params=pltpu.CompilerParams(dimension_semantics=("arbitrary",)),
        name="kv_proj",
    )(mem, wkt, wv)


def _mixer_kernel(x_ref, win_ref, bg_ref, cw_ref, wco_ref, wpool_ref, ps_ref, kt_ref, v_ref, wxo_ref, wo_ref,
                  g1_ref, b1_ref, wr_ref, br_ref,
                  h1_ref, route_ref, ri_ref, cnt_ref,
                  ubuf, pbuf, s1buf, s2buf, s3buf, carry, hpre, *, tiles_per_seq):
    tm = TM_MIX
    d = D_MODEL
    t = pl.program_id(0)
    s = lax.rem(t, tiles_per_seq)

    @pl.when(s == 0)
    def _():
        ubuf[0:CONV_HALO, :] = jnp.zeros((CONV_HALO, d), F32)
        pbuf[0:POOL_PAD + POOL_HALO, :] = jnp.zeros((POOL_PAD + POOL_HALO, d), F32)
        for sbuf in (s1buf, s2buf, s3buf):
            sbuf[0:POOL_PAD, :] = jnp.zeros((POOL_PAD, sbuf.shape[1]), F32)

    @pl.when(t == 0)
    def _():
        carry[...] = jnp.zeros_like(carry)
        hpre[...] = jnp.zeros_like(hpre)

    route_args = (hpre, g1_ref, b1_ref, wr_ref, br_ref, h1_ref, route_ref, ri_ref, cnt_ref, carry)
    last = pl.num_programs(0) - 1

    @pl.when(t < last)
    def _():
        _mix_tile(t, s, x_ref, win_ref, bg_ref, cw_ref, wco_ref, wpool_ref, ps_ref, kt_ref, v_ref, wxo_ref, wo_ref,
                  ubuf, pbuf, s1buf, s2buf, s3buf, route_args)

    @pl.when(t == last)
    def _():
        for _ in _route_stages(True, *route_args):
            pass


def _mix_tile(t, s, x_ref, win_ref, bg_ref, cw_ref, wco_ref, wpool_ref, ps_ref, kt_ref, v_ref, wxo_ref, wo_ref,
              ubuf, pbuf, s1buf, s2buf, s3buf, route_args):
    tm = TM_MIX
    d = D_MODEL
    hpre = route_args[0]
    routing = _route_stages(t >= 1, *route_args)

    x = x_ref[...]
    xb = x.astype(BF16)

    def proj(sec):
        return _dot(xb, win_ref[:, sec * d:(sec + 1) * d])

    def gate(i):
        return jax.nn.sigmoid(proj(5 + i) + bg_ref[i:i + 1, :])

    u = proj(1) * proj(2)
    next(routing)
    ubuf[CONV_HALO:CONV_HALO + tm, :] = u
    v = cw_ref[CONV_K - 1:CONV_K, :] * u
    for k in range(CONV_K - 1):
        off = CONV_HALO - (CONV_K - 1) + k
        v = v + cw_ref[k:k + 1, :] * ubuf[off:off + tm, :]
    ubuf[0:CONV_HALO, :] = ubuf[tm:tm + CONV_HALO, :]
    y_conv = _dot((proj(0) * v).astype(BF16), wco_ref[...])
    acc = gate(0) * y_conv
    next(routing)

    p = proj(3)
    base = POOL_PAD + POOL_HALO
    n = POOL_HALO + tm
    pbuf[base:base + tm, :] = p
    lvl = pbuf[POOL_PAD:POOL_PAD + n, :] + pbuf[POOL_PAD - 1:POOL_PAD - 1 + n, :]
    sums = [lvl[POOL_HALO:, :POOL_GROUP_DIM]]
    for gi, sbuf in enumerate((s1buf, s2buf, s3buf), start=1):
        shift = POOL_WINDOWS[gi - 1]
        rest = lvl[:, POOL_GROUP_DIM:]
        sbuf[POOL_PAD:POOL_PAD + n, :] = rest
        lvl = rest + sbuf[POOL_PAD - shift:POOL_PAD - shift + n, :]
        sums.append(lvl[POOL_HALO:, :POOL_GROUP_DIM])
    pos = (s * tm + 1 + lax.broadcasted_iota(jnp.int32, (tm, 1), 0)).astype(F32)
    ys = []
    for gi, w in enumerate(POOL_WINDOWS):
        c0 = gi * POOL_GROUP_DIM
        inv_cnt = 1.0 / jnp.minimum(pos, float(w))
        dg = (sums[gi] * inv_cnt - p[:, c0:c0 + POOL_GROUP_DIM]).astype(BF16)
        ys.append(_dot(dg, wpool_ref[gi]))
    pbuf[POOL_PAD:base, :] = pbuf[POOL_PAD + tm:base + tm, :]
    y_pool = jnp.concatenate(ys, axis=1) * ps_ref[...]
    acc = acc + gate(1) * y_pool
    next(routing)

    qb = proj(4).astype(BF16)
    scale = X_HEAD_DIM ** -0.5
    os_ = []
    for h in range(X_HEADS):
        c0 = h * X_HEAD_DIM
        c1 = c0 + X_HEAD_DIM
        sc = _dot(qb[:, c0:c1], kt_ref[0, c0:c1, :]) * scale
        e = jnp.exp(sc - jnp.max(sc, axis=-1, keepdims=True))
        a = e * (1.0 / jnp.sum(e, axis=-1, keepdims=True))
        os_.append(_dot(a.astype(BF16), v_ref[0, :, c0:c1]))
    y_mem = _dot(jnp.concatenate(os_, axis=1).astype(BF16), wxo_ref[...])
    acc = acc + gate(2) * y_mem
    next(routing)

    hpre[...] = ALPHA * x + _dot(acc.astype(BF16), wo_ref[...])


def _route_stages(valid, hpre, g1_ref, b1_ref, wr_ref, br_ref, h1_ref, route_ref, ri_ref, cnt_ref, carry):
    tm = TM_MIX
    h1 = _layer_norm(hpre[...], g1_ref[...], b1_ref[...])
    _tok_store(h1_ref, h1, tm)
    yield

    hh = h1.astype(BF16)
    hl = (h1 - hh.astype(F32)).astype(BF16)
    l2 = _dot(hh, wr_ref[...]) + _dot(hl, wr_ref[...])
    logits = l2[:, :LANES] + l2[:, LANES:] + br_ref[...]
    yield
    lane = lax.broadcasted_iota(jnp.int32, (tm, LANES), 1)
    lane_f = lane.astype(F32)
    neg = jnp.float32(-jnp.inf)
    big = jnp.float32(1e9)

    def first_argmax(vals):
        m = jnp.max(vals, axis=-1, keepdims=True)
        idx = jnp.min(jnp.where(vals == m, lane_f, big), axis=-1, keepdims=True)
        return m, idx

    is_g = (lane >= N_EXPERTS) & (lane < N_EXPERTS + N_GROUPS)
    gmax, gidx = first_argmax(jnp.where(is_g, logits, neg))
    g_w = 1.0 / jnp.sum(jnp.where(is_g, jnp.exp(logits - gmax), 0.0), axis=-1, keepdims=True)
    gsel = gidx.astype(jnp.int32) - N_EXPERTS
    in_grp = (lane >> 3) == gsel
    le = jnp.where(in_grp, logits, neg)
    m1, i1 = first_argmax(le)
    m2, i2 = first_argmax(jnp.where(lane_f == i1, neg, le))
    t = jnp.exp(m2 - m1)
    den = 1.0 + t
    w1 = g_w / den
    w2 = g_w * t / den
    yield

    sel1 = lane_f == i1
    sel2 = lane_f == i2
    onehot = (sel1 | sel2).astype(BF16)
    row_i = lax.broadcasted_iota(jnp.int32, (tm, tm), 0)
    col_i = lax.broadcasted_iota(jnp.int32, (tm, tm), 1)
    before = (col_i < row_i).astype(BF16)
    prior = _dot(before, onehot) + carry[0:1, :]
    r1 = jnp.sum(jnp.where(sel1, prior, 0.0), axis=-1, keepdims=True)
    r2 = jnp.sum(jnp.where(sel2, prior, 0.0), axis=-1, keepdims=True)
    carry[...] = carry[...] + jnp.where(valid, jnp.sum(onehot.astype(F32), axis=0, keepdims=True), 0.0)
    cnt_ref[...] = carry[...].astype(jnp.int32)

    route = jnp.where(lane == 0, i1, 0.0)
    for k, col in enumerate((i2, r1, r2, w1, w2), start=1):
        route = jnp.where(lane == k, col, route)
    route_ref[...] = route
    ri_ref[0] = route.T[0:8, :].astype(jnp.int32)
    yield


def _mixer(x2, w_in_b, b_gate3, conv_w, wco_b, wpool_b, pool_scale, kt, v, wxo_b, wo_b, ln_g, ln_b, wr, br,
           bn, seq):
    n, d = x2.shape
    tm = TM_MIX
    spb = seq // tm
    nt = bn * spb
    const2 = lambda t: (0, 0)
    const3 = lambda t: (0, 0, 0)
    one = pl.Buffered(1)
    mixed = lambda t: jnp.minimum(t, nt - 1)
    routed = lambda t: jnp.maximum(t - 1, 0)
    mem_map = lambda t: (mixed(t) // spb, 0, 0)
    row_map = lambda t: (routed(t), 0)
    return pl.pallas_call(
        functools.partial(_mixer_kernel, tiles_per_seq=spb),
        grid=(nt + 1,),
        in_specs=[
            pl.BlockSpec((tm, d), lambda t: (mixed(t), 0)),
            pl.BlockSpec(w_in_b.shape, const2, pipeline_mode=one),
            pl.BlockSpec(b_gate3.shape, const2, pipeline_mode=one),
            pl.BlockSpec(conv_w.shape, const2, pipeline_mode=one),
            pl.BlockSpec(wco_b.shape, const2, pipeline_mode=one),
            pl.BlockSpec(wpool_b.shape, const3, pipeline_mode=one),
            pl.BlockSpec(pool_scale.shape, const2, pipeline_mode=one),
            pl.BlockSpec((1,) + kt.shape[1:], mem_map),
            pl.BlockSpec((1,) + v.shape[1:], mem_map),
            pl.BlockSpec(wxo_b.shape, const2, pipeline_mode=one),
            pl.BlockSpec(wo_b.shape, const2, pipeline_mode=one),
            pl.BlockSpec(ln_g.shape, const2, pipeline_mode=one),
            pl.BlockSpec(ln_b.shape, const2, pipeline_mode=one),
            pl.BlockSpec(wr.shape, const2, pipeline_mode=one),
            pl.BlockSpec(br.shape, const2, pipeline_mode=one),
        ],
        out_specs=[
            pl.BlockSpec((tm * SUB, LANES), row_map),
            pl.BlockSpec((tm, LANES), row_map),
            pl.BlockSpec((1, 8, tm), lambda t: (routed(t), 0, 0)),
            pl.BlockSpec((8, LANES), const2),
        ],
        out_shape=[
            jax.ShapeDtypeStruct((n * SUB, LANES), F32),
            jax.ShapeDtypeStruct((n, LANES), F32),
            jax.ShapeDtypeStruct((n // tm, 8, tm), jnp.int32),
            jax.ShapeDtypeStruct((8, LANES), jnp.int32),
        ],
        scratch_shapes=[
            pltpu.VMEM((tm + CONV_HALO, d), F32),
            pltpu.VMEM((POOL_PAD + POOL_HALO + tm, d), F32),
            pltpu.VMEM((POOL_PAD + POOL_HALO + tm, d - POOL_GROUP_DIM), F32),
            pltpu.VMEM((POOL_PAD + POOL_HALO + tm, d - 2 * POOL_GROUP_DIM), F32),
            pltpu.VMEM((POOL_PAD + POOL_HALO + tm, d - 3 * POOL_GROUP_DIM), F32),
            pltpu.VMEM((8, LANES), F32),
            pltpu.VMEM((tm, d), F32),
        ],
        compiler_params=pltpu.CompilerParams(dimension_semantics=("arbitrary",), vmem_limit_bytes=VMEM_LIMIT),
        name="mixer",
    )(x2, w_in_b, b_gate3, conv_w, wco_b, wpool_b, pool_scale, kt, v, wxo_b, wo_b, ln_g, ln_b, wr, br)


def _dest_kernel(ps_ref, ri_ref, dest_ref):
    dest_ref[...] = jnp.zeros(dest_ref.shape, jnp.int32)
    for t in range(dest_ref.shape[0]):
        ri = ri_ref[t]
        e = ri[0:TOP_K, :]
        start = jnp.zeros_like(e)
        for j in range(N_EXPERTS):
            start = jnp.where(e == j, ps_ref[j], start)
        dest_ref[t, 0:TOP_K, :] = start + ri[TOP_K:2 * TOP_K, :]


def _dest(pstarts, route_i):
    nt, rows, tm = route_i.shape
    blk = 8
    spec = pl.BlockSpec((blk, rows, tm), lambda i, ps: (i, 0, 0))
    return pl.pallas_call(
        _dest_kernel,
        grid_spec=pltpu.PrefetchScalarGridSpec(
            num_scalar_prefetch=1, grid=(nt // blk,), in_specs=[spec], out_specs=spec),
        out_shape=jax.ShapeDtypeStruct(route_i.shape, jnp.int32),
        compiler_params=pltpu.CompilerParams(dimension_semantics=("arbitrary",)),
        name="dest_rows",
    )(pstarts, route_i)


def _scatter_kernel(zt_ref, dest_ref, h_ref, xbuf_ref, zbuf, stage, zsem, fsem, ssem):
    i = pl.program_id(0)
    nsteps = pl.num_programs(0)
    tm = TM_ROW
    rows = tm * SUB
    slot = lax.rem(i, SCATTER_RING)

    def zero_copy(e):
        start = pl.multiple_of(zt_ref[e] * SUB, TM_EXP * SUB)
        return pltpu.make_async_copy(zbuf, xbuf_ref.at[pl.ds(start, TM_EXP * SUB), :], zsem.at[0])

    def fetch(t, s):
        src = h_ref.at[pl.ds(pl.multiple_of(t * rows, rows), rows), :]
        return pltpu.make_async_copy(src, stage.at[s], fsem.at[s])

    def drain(s):
        for k in range(TOP_K):
            pltpu.make_async_copy(stage.at[s], xbuf_ref.at[pl.ds(0, rows), :], ssem.at[s, k]).wait()

    @pl.when(i == 0)
    def _():
        zbuf[...] = jnp.zeros_like(zbuf)

        def start(e, c):
            @pl.when(zt_ref[e] >= 0)
            def _():
                zero_copy(e).start()
            return c

        def wait(e, c):
            @pl.when(zt_ref[e] >= 0)
            def _():
                zero_copy(e).wait()
            return c

        lax.fori_loop(0, 2 * N_EXPERTS, start, 0)
        fetch(0, 0).start()
        lax.fori_loop(0, 2 * N_EXPERTS, wait, 0)

    @pl.when(i >= 2)
    def _():
        drain(lax.rem(i + 1, SCATTER_RING))

    @pl.when(i + 1 < nsteps)
    def _():
        fetch(i + 1, lax.rem(i + 1, SCATTER_RING)).start()

    fetch(i, slot).wait()

    def issue(r, c):
        for k in range(TOP_K):
            _tile_copy(stage.at[slot], r, xbuf_ref, dest_ref[k * tm + r], ssem.at[slot, k]).start(priority=k)
        return c

    lax.fori_loop(0, tm, issue, 0, unroll=8)

    @pl.when(i == nsteps - 1)
    def _():
        @pl.when(i >= 1)
        def _():
            drain(lax.rem(i + 2, SCATTER_RING))

        drain(slot)


def _scatter(zero_tiles, dest_flat, h1t, p_rows):
    tm = TM_ROW
    nt = dest_flat.shape[0] // (TOP_K * tm)
    return pl.pallas_call(
        _scatter_kernel,
        grid_spec=pltpu.PrefetchScalarGridSpec(
            num_scalar_prefetch=1,
            grid=(nt,),
            in_specs=[
                pl.BlockSpec((TOP_K * tm,), lambda i, zt: (i,), memory_space=pltpu.SMEM),
                pl.BlockSpec(memory_space=pl.ANY),
            ],
            out_specs=pl.BlockSpec(memory_space=pl.ANY),
            scratch_shapes=[
                pltpu.VMEM((TM_EXP * SUB, LANES), F32),
                pltpu.VMEM((SCATTER_RING, tm * SUB, LANES), F32),
                pltpu.SemaphoreType.DMA((1,)),
                pltpu.SemaphoreType.DMA((SCATTER_RING,)),
                pltpu.SemaphoreType.DMA((SCATTER_RING, TOP_K)),
            ],
        ),
        out_shape=jax.ShapeDtypeStruct((p_rows * SUB, LANES), F32),
        compiler_params=pltpu.CompilerParams(dimension_semantics=("arbitrary",)),
        name="scatter_rows",
    )(zero_tiles, dest_flat, h1t)


def _expert_kernel(t0_ref, nt_ref, tot_ref, wup_ref, wdn_ref, xbuf_ref, ybuf_ref, wup_b, wdn_b, xs, ys, sem_in, sem_out):
    e = pl.program_id(0)
    t0 = t0_ref[e]
    nt = nt_ref[e]
    total = tot_ref[0]
    rows = TM_EXP * SUB
    max_tiles = xbuf_ref.shape[0] // rows

    def hbm_tile(ref, g):
        return ref.at[pl.ds(pl.multiple_of(g * rows, rows), rows), :]

    def x_copy(g, slot):
        return pltpu.make_async_copy(hbm_tile(xbuf_ref, g), xs.at[slot], sem_in.at[slot])

    def y_copy(g, slot):
        return pltpu.make_async_copy(ys.at[slot], hbm_tile(ybuf_ref, g), sem_out.at[slot])

    ahead = RING - 1

    @pl.when(e == 0)
    def _():
        for g0 in range(ahead):
            @pl.when(g0 < total)
            def _():
                x_copy(g0, g0).start(priority=ROW_DMA_PRIORITY)

    @pl.when(nt > 0)
    def _():
        wup_b[...] = wup_ref[0].astype(BF16)
        wdn_b[...] = wdn_ref[0].astype(BF16)

    def tile(j, c):
        g = t0 + j
        slot = lax.rem(g, RING)

        @pl.when(g + ahead < total)
        def _():
            x_copy(g + ahead, lax.rem(g + ahead, RING)).start(priority=ROW_DMA_PRIORITY)

        x_copy(g, slot).wait()

        @pl.when(g >= RING)
        def _():
            y_copy(g - RING, slot).wait()

        xb = _tok_load(xs.at[slot], TM_EXP).astype(BF16)
        hgv = _dot(xb, wup_b[...])
        hg = hgv[:, :D_EXPERT]
        hv = hgv[:, D_EXPERT:]
        act = (hg * jax.nn.sigmoid(hg)) * hv
        _tok_store(ys.at[slot], _dot(act.astype(BF16), wdn_b[...]), TM_EXP)
        y_copy(g, slot).start(priority=ROW_DMA_PRIORITY)
        return c

    lax.fori_loop(0, nt, tile, 0)

    @pl.when(e == pl.num_programs(0) - 1)
    def _():
        for back in range(RING, 0, -1):
            @pl.when(total >= back)
            def _():
                y_copy(total - back, lax.rem(total - back, RING)).wait()

        ys[0] = jnp.zeros(ys.shape[1:], F32)

        def zero_tail(g, c):
            y_copy(g, 0).start()
            y_copy(g, 0).wait()
            return c

        lax.fori_loop(total, max_tiles, zero_tail, 0)


def _experts(tile_start, tile_count, n_tiles, xbuf, w_up, w_down):
    w_map = lambda e, t0, nt, tot: (e, 0, 0)
    rows = TM_EXP * SUB
    return pl.pallas_call(
        _expert_kernel,
        grid_spec=pltpu.PrefetchScalarGridSpec(
            num_scalar_prefetch=3,
            grid=(N_EXPERTS,),
            in_specs=[
                pl.BlockSpec((1,) + w_up.shape[1:], w_map),
                pl.BlockSpec((1,) + w_down.shape[1:], w_map),
                pl.BlockSpec(memory_space=pl.ANY),
            ],
            out_specs=pl.BlockSpec(memory_space=pl.ANY),
            scratch_shapes=[
                pltpu.VMEM(w_up.shape[1:], BF16), pltpu.VMEM(w_down.shape[1:], BF16),
                pltpu.VMEM((RING, rows, LANES), F32), pltpu.VMEM((RING, rows, LANES), F32),
                pltpu.SemaphoreType.DMA((RING,)), pltpu.SemaphoreType.DMA((RING,)),
            ],
        ),
        out_shape=jax.ShapeDtypeStruct(xbuf.shape, F32),
        compiler_params=pltpu.CompilerParams(dimension_semantics=("arbitrary",), vmem_limit_bytes=VMEM_LIMIT),
        name="experts",
    )(tile_start, tile_count, n_tiles, w_up, w_down, xbuf)


def _combine_kernel(dcur_ref, dnext_ref, h_ref, route_ref, ybuf_ref, g_ref, b_ref, out_ref, ybufs, sem):
    tm = TM_ROW
    i = pl.program_id(0)
    slot = lax.rem(i, 2)

    def gather(d_ref, s):
        def issue(r, c):
            for k in range(TOP_K):
                _tile_copy(ybuf_ref, d_ref[k * tm + r], ybufs.at[s, k], r, sem.at[s, k]).start(priority=k)
            return c

        lax.fori_loop(0, tm, issue, 0, unroll=8)

    @pl.when(i == 0)
    def _():
        gather(dcur_ref, 0)

    @pl.when(i + 1 < pl.num_programs(0))
    def _():
        gather(dnext_ref, 1 - slot)

    for k in range(TOP_K):
        pltpu.make_async_copy(ybuf_ref.at[pl.ds(0, tm * SUB), :], ybufs.at[slot, k], sem.at[slot, k]).wait()
    route = route_ref[...]
    ffn = route[:, 4:5] * _tok_load(ybufs.at[slot, 0], tm) + route[:, 5:6] * _tok_load(ybufs.at[slot, 1], tm)
    out_ref[...] = _layer_norm(ALPHA * _tok_load(h_ref, tm) + ffn, g_ref[...], b_ref[...])


def _combine(dest_flat, h1t, route, ybuf, ln_g, ln_b):
    tm = TM_ROW
    nt = dest_flat.shape[0] // (TOP_K * tm)
    n = nt * tm
    return pl.pallas_call(
        _combine_kernel,
        grid=(nt,),
        in_specs=[
            pl.BlockSpec((TOP_K * tm,), lambda i: (i,), memory_space=pltpu.SMEM),
            pl.BlockSpec((TOP_K * tm,), lambda i: (jnp.minimum(i + 1, nt - 1),), memory_space=pltpu.SMEM),
            pl.BlockSpec((tm * SUB, LANES), lambda i: (i, 0)),
            pl.BlockSpec((tm, LANES), lambda i: (i, 0)),
            pl.BlockSpec(memory_space=pl.ANY),
            pl.BlockSpec(ln_g.shape, lambda i: (0, 0)),
            pl.BlockSpec(ln_b.shape, lambda i: (0, 0)),
        ],
        out_specs=pl.BlockSpec((tm, D_MODEL), lambda i: (i, 0)),
        out_shape=jax.ShapeDtypeStruct((n, D_MODEL), F32),
        scratch_shapes=[pltpu.VMEM((2, TOP_K, tm * SUB, LANES), F32), pltpu.SemaphoreType.DMA((2, TOP_K))],
        compiler_params=pltpu.CompilerParams(dimension_semantics=("arbitrary",)),
        name="combine_ln2",
    )(dest_flat, dest_flat, h1t, route, ybuf, ln_g, ln_b)


def _layer(h, mem, w_in, b_gate, conv_w, w_conv_out, w_pool, pool_scale, w_kv, w_xo, w_o, ln1_g, ln1_b,
           w_rg, b_rg, w_re, b_re, w_up, w_down, ln2_g, ln2_b):
    bn, seq, d = h.shape
    n = bn * seq
    xw = X_HEADS * X_HEAD_DIM

    kt, v = _kv_proj(mem, w_kv[:, :xw].T.astype(BF16), w_kv[:, xw:].astype(BF16))

    pad = LANES - N_EXPERTS - N_GROUPS
    w_r = jnp.concatenate([w_re, w_rg, jnp.zeros((d, pad), F32)], axis=1)
    wr_hi = w_r.astype(BF16)
    wr_lo = (w_r - wr_hi.astype(F32)).astype(BF16)
    b_r = jnp.concatenate([b_re, b_rg, jnp.zeros((pad,), F32)])[None, :]

    h1t, route, route_i, cnt = _mixer(
        h.reshape(n, d), w_in.astype(BF16), b_gate.reshape(3, d), conv_w, w_conv_out.astype(BF16),
        w_pool.astype(BF16), pool_scale[None, :], kt, v, w_xo.astype(BF16), w_o.astype(BF16),
        ln1_g[None, :], ln1_b[None, :], jnp.concatenate([wr_hi, wr_lo], axis=1), b_r, bn, seq)

    counts = cnt[0, :N_EXPERTS]
    padded = (counts + TM_EXP - 1) // TM_EXP * TM_EXP
    pends = jnp.cumsum(padded)
    pstarts = pends - padded
    max_tiles = (n * TOP_K + N_EXPERTS * (TM_EXP - 1)) // TM_EXP
    p_rows = max_tiles * TM_EXP
    n_tiles = (pends[-1] // TM_EXP).astype(jnp.int32)
    tail_ids = n_tiles + jnp.arange(N_EXPERTS, dtype=jnp.int32)
    zero_tiles = jnp.concatenate([
        jnp.where(counts % TM_EXP != 0, pends - TM_EXP, -1),
        jnp.where(tail_ids < max_tiles, tail_ids * TM_EXP, -1)]).astype(jnp.int32)

    dest = _dest(pstarts.astype(jnp.int32), route_i)[:, :TOP_K, :].reshape(-1)
    xbuf = _scatter(zero_tiles, dest, h1t, p_rows)
    ybuf = _experts((pstarts // TM_EXP).astype(jnp.int32), (padded // TM_EXP).astype(jnp.int32), n_tiles.reshape(1),
                    xbuf, w_up, w_down)
    out = _combine(dest, h1t, route, ybuf, ln2_g[None, :], ln2_b[None, :])
    return out.reshape(bn, seq, d)


def kernel(x, mem, w_in, b_gate, conv_w, w_conv_out, w_pool, pool_scale, w_kv, w_xo, w_o, ln1_g, ln1_b,
           w_router_group, b_router_group, w_router_expert, b_router_expert, w_up, w_down, ln2_g, ln2_b):
    h = x
    for l in range(DEPTH):
        h = _layer(h, mem, w_in[l], b_gate[l], conv_w[l], w_conv_out[l], w_pool[l], pool_scale[l], w_kv[l],
                   w_xo[l], w_o[l], ln1_g[l], ln1_b[l], w_router_group[l], b_router_group[l],
                   w_router_expert[l], b_router_expert[l], w_up[l], w_down[l], ln2_g[l], ln2_b[l])
    return h
```

```python
import functools

import jax
import jax.numpy as jnp
from jax import lax
from jax.experimental import pallas as pl
from jax.experimental.pallas import tpu as pltpu

D_MODEL = 1024
CONV_K = 3
POOL_WINDOWS = (2, 4, 8, 16)
POOL_GROUP_DIM = 256
X_HEADS = 4
X_HEAD_DIM = 256
N_GROUPS = 8
EXPERTS_PER_GROUP = 8
N_EXPERTS = 64
TOP_K = 2
D_EXPERT = 512
DEPTH = 1
ALPHA = (2.0 * DEPTH) ** 0.25
LN_EPS = 1e-5

LANES = 128
SUB = 8
assert D_MODEL == SUB * LANES
POOL_HALO = 16
POOL_PAD = 8
assert POOL_WINDOWS[0] == 2 and all(b == 2 * a for a, b in zip(POOL_WINDOWS, POOL_WINDOWS[1:]))
assert POOL_PAD >= POOL_WINDOWS[-2] and POOL_HALO >= POOL_WINDOWS[-1] and len(POOL_WINDOWS) == 4
CONV_HALO = 8
TM_MIX = 512
TM_EXP = 256
TM_ROW = TM_MIX
VMEM_LIMIT = 58 * 1024 * 1024
SCATTER_RING = 3
PAIR = 2
RING = 6
assert PAIR == 2 and RING > PAIR
ROW_DMA_PRIORITY = 1

F32 = jnp.float32
BF16 = jnp.bfloat16


def _dot(a, b):
    return jnp.dot(a, b, preferred_element_type=F32)


def _tok_load(ref, tm):
    return jnp.concatenate([ref[pl.ds(c, tm, stride=SUB), :] for c in range(SUB)], axis=1)


def _tok_store(ref, val, tm):
    for c in range(SUB):
        ref[pl.ds(c, tm, stride=SUB), :] = val[:, c * LANES:(c + 1) * LANES]


def _tile_copy(src, src_tok, dst, dst_tok, sem):
    s = pl.multiple_of(src_tok * SUB, SUB)
    t = pl.multiple_of(dst_tok * SUB, SUB)
    return pltpu.make_async_copy(src.at[pl.ds(s, SUB), :], dst.at[pl.ds(t, SUB), :], sem)


def _layer_norm(h, g, b):
    mu = jnp.mean(h, axis=-1, keepdims=True)
    c = h - mu
    var = jnp.mean(c * c, axis=-1, keepdims=True)
    return c * lax.rsqrt(var + LN_EPS) * g + b


def _kv_kernel(mem_ref, wkt_ref, wv_ref, kt_ref, v_ref):
    mb = mem_ref[0].astype(BF16)
    kt = lax.dot_general(wkt_ref[...], mb, (((1,), (1,)), ((), ())), preferred_element_type=F32)
    kt_ref[0] = kt.astype(BF16)
    v_ref[0] = _dot(mb, wv_ref[...]).astype(BF16)


def _kv_proj(mem, wkt, wv):
    bn, m, d = mem.shape
    return pl.pallas_call(
        _kv_kernel,
        grid=(bn,),
        in_specs=[
            pl.BlockSpec((1, m, d), lambda b: (b, 0, 0)),
            pl.BlockSpec((d, d), lambda b: (0, 0)),
            pl.BlockSpec((d, d), lambda b: (0, 0)),
        ],
        out_specs=[
            pl.BlockSpec((1, d, m), lambda b: (b, 0, 0)),
            pl.BlockSpec((1, m, d), lambda b: (b, 0, 0)),
        ],
        out_shape=[jax.ShapeDtypeStruct((bn, d, m), BF16), jax.ShapeDtypeStruct((bn, m, d), BF16)],
        compiler_params=pltpu.CompilerParams(dimension_semantics=("arbitrary",)),
        name="kv_proj",
    )(mem, wkt, wv)


def _mixer_kernel(x_ref, win_ref, bg_ref, cw_ref, wco_ref, wpool_ref, ps_ref, kt_ref, v_ref, wxo_ref, wo_ref,
                  g1_ref, b1_ref, wr_ref, br_ref,
                  h1_ref, route_ref, ri_ref, cnt_ref,
                  ubuf, pbuf, s1buf, s2buf, s3buf, carry, hpre, *, tiles_per_seq):
    tm = TM_MIX
    d = D_MODEL
    t = pl.program_id(0)
    s = lax.rem(t, tiles_per_seq)

    @pl.when(s == 0)
    def _():
        ubuf[0:CONV_HALO, :] = jnp.zeros((CONV_HALO, d), F32)
        pbuf[0:POOL_PAD + POOL_HALO, :] = jnp.zeros((POOL_PAD + POOL_HALO, d), F32)
        for sbuf in (s1buf, s2buf, s3buf):
            sbuf[0:POOL_PAD, :] = jnp.zeros((POOL_PAD, sbuf.shape[1]), F32)

    @pl.when(t == 0)
    def _():
        carry[...] = jnp.zeros_like(carry)
        hpre[...] = jnp.zeros_like(hpre)

    route_args = (hpre, g1_ref, b1_ref, wr_ref, br_ref, h1_ref, route_ref, ri_ref, cnt_ref, carry)
    last = pl.num_programs(0) - 1

    @pl.when(t < last)
    def _():
        _mix_tile(t, s, x_ref, win_ref, bg_ref, cw_ref, wco_ref, wpool_ref, ps_ref, kt_ref, v_ref, wxo_ref, wo_ref,
                  ubuf, pbuf, s1buf, s2buf, s3buf, route_args)

    @pl.when(t == last)
    def _():
        for _ in _route_stages(True, *route_args):
            pass


def _mix_tile(t, s, x_ref, win_ref, bg_ref, cw_ref, wco_ref, wpool_ref, ps_ref, kt_ref, v_ref, wxo_ref, wo_ref,
              ubuf, pbuf, s1buf, s2buf, s3buf, route_args):
    tm = TM_MIX
    d = D_MODEL
    hpre = route_args[0]
    routing = _route_stages(t >= 1, *route_args)

    x = x_ref[...]
    xb = x.astype(BF16)

    def proj(sec):
        return _dot(xb, win_ref[:, sec * d:(sec + 1) * d])

    def gate(i):
        return jax.nn.sigmoid(proj(5 + i) + bg_ref[i:i + 1, :])

    u = proj(1) * proj(2)
    next(routing)
    ubuf[CONV_HALO:CONV_HALO + tm, :] = u
    v = cw_ref[CONV_K - 1:CONV_K, :] * u
    for k in range(CONV_K - 1):
        off = CONV_HALO - (CONV_K - 1) + k
        v = v + cw_ref[k:k + 1, :] * ubuf[off:off + tm, :]
    ubuf[0:CONV_HALO, :] = ubuf[tm:tm + CONV_HALO, :]
    y_conv = _dot((proj(0) * v).astype(BF16), wco_ref[...])
    acc = gate(0) * y_conv
    next(routing)

    p = proj(3)
    base = POOL_PAD + POOL_HALO
    n = POOL_HALO + tm
    pbuf[base:base + tm, :] = p
    lvl = pbuf[POOL_PAD:POOL_PAD + n, :] + pbuf[POOL_PAD - 1:POOL_PAD - 1 + n, :]
    sums = [lvl[POOL_HALO:, :POOL_GROUP_DIM]]
    for gi, sbuf in enumerate((s1buf, s2buf, s3buf), start=1):
        shift = POOL_WINDOWS[gi - 1]
        rest = lvl[:, POOL_GROUP_DIM:]
        sbuf[POOL_PAD:POOL_PAD + n, :] = rest
        lvl = rest + sbuf[POOL_PAD - shift:POOL_PAD - shift + n, :]
        sums.append(lvl[POOL_HALO:, :POOL_GROUP_DIM])
    pos = (s * tm + 1 + lax.broadcasted_iota(jnp.int32, (tm, 1), 0)).astype(F32)
    ys = []
    for gi, w in enumerate(POOL_WINDOWS):
        c0 = gi * POOL_GROUP_DIM
        inv_cnt = 1.0 / jnp.minimum(pos, float(w))
        dg = (sums[gi] * inv_cnt - p[:, c0:c0 + POOL_GROUP_DIM]).astype(BF16)
        ys.append(_dot(dg, wpool_ref[gi]))
    pbuf[POOL_PAD:base, :] = pbuf[POOL_PAD + tm:base + tm, :]
    y_pool = jnp.concatenate(ys, axis=1) * ps_ref[...]
    acc = acc + gate(1) * y_pool
    next(routing)

    qb = proj(4).astype(BF16)
    scale = X_HEAD_DIM ** -0.5
    os_ = []
    for h in range(X_HEADS):
        c0 = h * X_HEAD_DIM
        c1 = c0 + X_HEAD_DIM
        sc = _dot(qb[:, c0:c1], kt_ref[0, c0:c1, :]) * scale
        e = jnp.exp(sc - jnp.max(sc, axis=-1, keepdims=True))
        a = e * (1.0 / jnp.sum(e, axis=-1, keepdims=True))
        os_.append(_dot(a.astype(BF16), v_ref[0, :, c0:c1]))
    y_mem = _dot(jnp.concatenate(os_, axis=1).astype(BF16), wxo_ref[...])
    acc = acc + gate(2) * y_mem
    next(routing)

    hpre[...] = ALPHA * x + _dot(acc.astype(BF16), wo_ref[...])


def _route_stages(valid, hpre, g1_ref, b1_ref, wr_ref, br_ref, h1_ref, route_ref, ri_ref, cnt_ref, carry):
    tm = TM_MIX
    h1 = _layer_norm(hpre[...], g1_ref[...], b1_ref[...])
    _tok_store(h1_ref, h1, tm)
    yield

    hh = h1.astype(BF16)
    hl = (h1 - hh.astype(F32)).astype(BF16)
    l2 = _dot(hh, wr_ref[...]) + _dot(hl, wr_ref[...])
    logits = l2[:, :LANES] + l2[:, LANES:] + br_ref[...]
    yield
    lane = lax.broadcasted_iota(jnp.int32, (tm, LANES), 1)
    lane_f = lane.astype(F32)
    neg = jnp.float32(-jnp.inf)
    big = jnp.float32(1e9)

    def first_argmax(vals):
        m = jnp.max(vals, axis=-1, keepdims=True)
        idx = jnp.min(jnp.where(vals == m, lane_f, big), axis=-1, keepdims=True)
        return m, idx

    is_g = (lane >= N_EXPERTS) & (lane < N_EXPERTS + N_GROUPS)
    gmax, gidx = first_argmax(jnp.where(is_g, logits, neg))
    g_w = 1.0 / jnp.sum(jnp.where(is_g, jnp.exp(logits - gmax), 0.0), axis=-1, keepdims=True)
    gsel = gidx.astype(jnp.int32) - N_EXPERTS
    in_grp = (lane >> 3) == gsel
    le = jnp.where(in_grp, logits, neg)
    m1, i1 = first_argmax(le)
    m2, i2 = first_argmax(jnp.where(lane_f == i1, neg, le))
    t = jnp.exp(m2 - m1)
    den = 1.0 + t
    w1 = g_w / den
    w2 = g_w * t / den
    yield

    sel1 = lane_f == i1
    sel2 = lane_f == i2
    onehot = (sel1 | sel2).astype(BF16)
    row_i = lax.broadcasted_iota(jnp.int32, (tm, tm), 0)
    col_i = lax.broadcasted_iota(jnp.int32, (tm, tm), 1)
    before = (col_i < row_i).astype(BF16)
    prior = _dot(before, onehot) + carry[0:1, :]
    r1 = jnp.sum(jnp.where(sel1, prior, 0.0), axis=-1, keepdims=True)
    r2 = jnp.sum(jnp.where(sel2, prior, 0.0), axis=-1, keepdims=True)
    carry[...] = carry[...] + jnp.where(valid, jnp.sum(onehot.astype(F32), axis=0, keepdims=True), 0.0)
    cnt_ref[...] = carry[...].astype(jnp.int32)

    route = jnp.where(lane == 0, i1, 0.0)
    for k, col in enumerate((i2, r1, r2, w1, w2), start=1):
        route = jnp.where(lane == k, col, route)
    route_ref[...] = route
    ri_ref[0] = route.T[0:8, :].astype(jnp.int32)
    yield


def _mixer(x2, w_in_b, b_gate3, conv_w, wco_b, wpool_b, pool_scale, kt, v, wxo_b, wo_b, ln_g, ln_b, wr, br,
           bn, seq):
    n, d = x2.shape
    tm = TM_MIX
    spb = seq // tm
    nt = bn * spb
    const2 = lambda t: (0, 0)
    const3 = lambda t: (0, 0, 0)
    one = pl.Buffered(1)
    mixed = lambda t: jnp.minimum(t, nt - 1)
    routed = lambda t: jnp.maximum(t - 1, 0)
    mem_map = lambda t: (mixed(t) // spb, 0, 0)
    row_map = lambda t: (routed(t), 0)
    return pl.pallas_call(
        functools.partial(_mixer_kernel, tiles_per_seq=spb),
        grid=(nt + 1,),
        in_specs=[
            pl.BlockSpec((tm, d), lambda t: (mixed(t), 0)),
            pl.BlockSpec(w_in_b.shape, const2, pipeline_mode=one),
            pl.BlockSpec(b_gate3.shape, const2, pipeline_mode=one),
            pl.BlockSpec(conv_w.shape, const2, pipeline_mode=one),
            pl.BlockSpec(wco_b.shape, const2, pipeline_mode=one),
            pl.BlockSpec(wpool_b.shape, const3, pipeline_mode=one),
            pl.BlockSpec(pool_scale.shape, const2, pipeline_mode=one),
            pl.BlockSpec((1,) + kt.shape[1:], mem_map),
            pl.BlockSpec((1,) + v.shape[1:], mem_map),
            pl.BlockSpec(wxo_b.shape, const2, pipeline_mode=one),
            pl.BlockSpec(wo_b.shape, const2, pipeline_mode=one),
            pl.BlockSpec(ln_g.shape, const2, pipeline_mode=one),
            pl.BlockSpec(ln_b.shape, const2, pipeline_mode=one),
            pl.BlockSpec(wr.shape, const2, pipeline_mode=one),
            pl.BlockSpec(br.shape, const2, pipeline_mode=one),
        ],
        out_specs=[
            pl.BlockSpec((tm * SUB, LANES), row_map),
            pl.BlockSpec((tm, LANES), row_map),
            pl.BlockSpec((1, 8, tm), lambda t: (routed(t), 0, 0)),
            pl.BlockSpec((8, LANES), const2),
        ],
        out_shape=[
            jax.ShapeDtypeStruct((n * SUB, LANES), F32),
            jax.ShapeDtypeStruct((n, LANES), F32),
            jax.ShapeDtypeStruct((n // tm, 8, tm), jnp.int32),
            jax.ShapeDtypeStruct((8, LANES), jnp.int32),
        ],
        scratch_shapes=[
            pltpu.VMEM((tm + CONV_HALO, d), F32),
            pltpu.VMEM((POOL_PAD + POOL_HALO + tm, d), F32),
            pltpu.VMEM((POOL_PAD + POOL_HALO + tm, d - POOL_GROUP_DIM), F32),
            pltpu.VMEM((POOL_PAD + POOL_HALO + tm, d - 2 * POOL_GROUP_DIM), F32),
            pltpu.VMEM((POOL_PAD + POOL_HALO + tm, d - 3 * POOL_GROUP_DIM), F32),
            pltpu.VMEM((8, LANES), F32),
            pltpu.VMEM((tm, d), F32),
        ],
        compiler_params=pltpu.CompilerParams(dimension_semantics=("arbitrary",), vmem_limit_bytes=VMEM_LIMIT),
        name="mixer",
    )(x2, w_in_b, b_gate3, conv_w, wco_b, wpool_b, pool_scale, kt, v, wxo_b, wo_b, ln_g, ln_b, wr, br)


def _dest_kernel(ps_ref, ri_ref, dest_ref):
    dest_ref[...] = jnp.zeros(dest_ref.shape, jnp.int32)
    for t in range(dest_ref.shape[0]):
        ri = ri_ref[t]
        e = ri[0:TOP_K, :]
        start = jnp.zeros_like(e)
        for j in range(N_EXPERTS):
            start = jnp.where(e == j, ps_ref[j], start)
        dest_ref[t, 0:TOP_K, :] = start + ri[TOP_K:2 * TOP_K, :]


def _dest(pstarts, route_i):
    nt, rows, tm = route_i.shape
    blk = 8
    spec = pl.BlockSpec((blk, rows, tm), lambda i, ps: (i, 0, 0))
    return pl.pallas_call(
        _dest_kernel,
        grid_spec=pltpu.PrefetchScalarGridSpec(
            num_scalar_prefetch=1, grid=(nt // blk,), in_specs=[spec], out_specs=spec),
        out_shape=jax.ShapeDtypeStruct(route_i.shape, jnp.int32),
        compiler_params=pltpu.CompilerParams(dimension_semantics=("arbitrary",)),
        name="dest_rows",
    )(pstarts, route_i)


def _scatter_kernel(zt_ref, dest_ref, h_ref, xbuf_ref, zbuf, stage, zsem, fsem, ssem):
    i = pl.program_id(0)
    nsteps = pl.num_programs(0)
    tm = TM_ROW
    rows = tm * SUB
    slot = lax.rem(i, SCATTER_RING)

    def zero_copy(e):
        start = pl.multiple_of(zt_ref[e] * SUB, TM_EXP * SUB)
        return pltpu.make_async_copy(zbuf, xbuf_ref.at[pl.ds(start, TM_EXP * SUB), :], zsem.at[0])

    def fetch(t, s):
        src = h_ref.at[pl.ds(pl.multiple_of(t * rows, rows), rows), :]
        return pltpu.make_async_copy(src, stage.at[s], fsem.at[s])

    def drain(s):
        for k in range(TOP_K):
            pltpu.make_async_copy(stage.at[s], xbuf_ref.at[pl.ds(0, rows), :], ssem.at[s, k]).wait()

    @pl.when(i == 0)
    def _():
        zbuf[...] = jnp.zeros_like(zbuf)

        def start(e, c):
            @pl.when(zt_ref[e] >= 0)
            def _():
                zero_copy(e).start()
            return c

        def wait(e, c):
            @pl.when(zt_ref[e] >= 0)
            def _():
                zero_copy(e).wait()
            return c

        lax.fori_loop(0, 2 * N_EXPERTS, start, 0)
        fetch(0, 0).start()
        lax.fori_loop(0, 2 * N_EXPERTS, wait, 0)

    @pl.when(i >= 2)
    def _():
        drain(lax.rem(i + 1, SCATTER_RING))

    @pl.when(i + 1 < nsteps)
    def _():
        fetch(i + 1, lax.rem(i + 1, SCATTER_RING)).start()

    fetch(i, slot).wait()

    def issue(r, c):
        for k in range(TOP_K):
            _tile_copy(stage.at[slot], r, xbuf_ref, dest_ref[k * tm + r], ssem.at[slot, k]).start(priority=k)
        return c

    lax.fori_loop(0, tm, issue, 0, unroll=8)

    @pl.when(i == nsteps - 1)
    def _():
        @pl.when(i >= 1)
        def _():
            drain(lax.rem(i + 2, SCATTER_RING))

        drain(slot)


def _scatter(zero_tiles, dest_flat, h1t, p_rows):
    tm = TM_ROW
    nt = dest_flat.shape[0] // (TOP_K * tm)
    return pl.pallas_call(
        _scatter_kernel,
        grid_spec=pltpu.PrefetchScalarGridSpec(
            num_scalar_prefetch=1,
            grid=(nt,),
            in_specs=[
                pl.BlockSpec((TOP_K * tm,), lambda i, zt: (i,), memory_space=pltpu.SMEM),
                pl.BlockSpec(memory_space=pl.ANY),
            ],
            out_specs=pl.BlockSpec(memory_space=pl.ANY),
            scratch_shapes=[
                pltpu.VMEM((TM_EXP * SUB, LANES), F32),
                pltpu.VMEM((SCATTER_RING, tm * SUB, LANES), F32),
                pltpu.SemaphoreType.DMA((1,)),
                pltpu.SemaphoreType.DMA((SCATTER_RING,)),
                pltpu.SemaphoreType.DMA((SCATTER_RING, TOP_K)),
            ],
        ),
        out_shape=jax.ShapeDtypeStruct((p_rows * SUB, LANES), F32),
        compiler_params=pltpu.CompilerParams(dimension_semantics=("arbitrary",)),
        name="scatter_rows",
    )(zero_tiles, dest_flat, h1t)


def _expert_kernel(t0_ref, nt_ref, tot_ref, wup_ref, wdn_ref, xbuf_ref, ybuf_ref, wup_b, wdn_b, xs, ys, sem_in, sem_out):
    e = pl.program_id(0)
    t0 = t0_ref[e]
    nt = nt_ref[e]
    total = tot_ref[0]
    rows = TM_EXP * SUB
    max_tiles = xbuf_ref.shape[0] // rows

    def hbm_tile(ref, g):
        return ref.at[pl.ds(pl.multiple_of(g * rows, rows), rows), :]

    def x_copy(g, slot):
        return pltpu.make_async_copy(hbm_tile(xbuf_ref, g), xs.at[slot], sem_in.at[slot])

    def y_copy(g, slot):
        return pltpu.make_async_copy(ys.at[slot], hbm_tile(ybuf_ref, g), sem_out.at[slot])

    ahead = RING - PAIR

    @pl.when(e == 0)
    def _():
        for g0 in range(ahead):
            @pl.when(g0 < total)
            def _():
                x_copy(g0, g0).start(priority=ROW_DMA_PRIORITY)

    @pl.when(nt > 0)
    def _():
        wup_b[...] = wup_ref[0].astype(BF16)
        wdn_b[...] = wdn_ref[0].astype(BF16)

    def process(g, width):
        slots = [lax.rem(g + w, RING) for w in range(width)]
        for w in range(width):
            @pl.when(g + w + ahead < total)
            def _(w=w):
                x_copy(g + w + ahead, lax.rem(g + w + ahead, RING)).start(priority=ROW_DMA_PRIORITY)

        for w in range(width):
            x_copy(g + w, slots[w]).wait()

            @pl.when(g + w >= RING)
            def _(w=w):
                y_copy(g + w - RING, slots[w]).wait()

        xb = jnp.concatenate([_tok_load(xs.at[s], TM_EXP) for s in slots], axis=0).astype(BF16)
        hgv = _dot(xb, wup_b[...])
        hg = hgv[:, :D_EXPERT]
        hv = hgv[:, D_EXPERT:]
        act = (hg * jax.nn.sigmoid(hg)) * hv
        y = _dot(act.astype(BF16), wdn_b[...])
        for w in range(width):
            _tok_store(ys.at[slots[w]], y[w * TM_EXP:(w + 1) * TM_EXP], TM_EXP)
            y_copy(g + w, slots[w]).start(priority=ROW_DMA_PRIORITY)

    def pair(jp, c):
        process(t0 + PAIR * jp, PAIR)
        return c

    lax.fori_loop(0, nt // PAIR, pair, 0)

    @pl.when(lax.rem(nt, PAIR) == 1)
    def _():
        process(t0 + nt - 1, 1)

    @pl.when(e == pl.num_programs(0) - 1)
    def _():
        for back in range(RING, 0, -1):
            @pl.when(total >= back)
            def _():
                y_copy(total - back, lax.rem(total - back, RING)).wait()

        ys[0] = jnp.zeros(ys.shape[1:], F32)

        def zero_tail(g, c):
            y_copy(g, 0).start()
            y_copy(g, 0).wait()
            return c

        lax.fori_loop(total, max_tiles, zero_tail, 0)


def _experts(tile_start, tile_count, n_tiles, xbuf, w_up, w_down):
    w_map = lambda e, t0, nt, tot: (e, 0, 0)
    rows = TM_EXP * SUB
    return pl.pallas_call(
        _expert_kernel,
        grid_spec=pltpu.PrefetchScalarGridSpec(
            num_scalar_prefetch=3,
            grid=(N_EXPERTS,),
            in_specs=[
                pl.BlockSpec((1,) + w_up.shape[1:], w_map),
                pl.BlockSpec((1,) + w_down.shape[1:], w_map),
                pl.BlockSpec(memory_space=pl.ANY),
            ],
            out_specs=pl.BlockSpec(memory_space=pl.ANY),
            scratch_shapes=[
                pltpu.VMEM(w_up.shape[1:], BF16), pltpu.VMEM(w_down.shape[1:], BF16),
                pltpu.VMEM((RING, rows, LANES), F32), pltpu.VMEM((RING, rows, LANES), F32),
                pltpu.SemaphoreType.DMA((RING,)), pltpu.SemaphoreType.DMA((RING,)),
            ],
        ),
        out_shape=jax.ShapeDtypeStruct(xbuf.shape, F32),
        compiler_params=pltpu.CompilerParams(dimension_semantics=("arbitrary",), vmem_limit_bytes=VMEM_LIMIT),
        name="experts",
    )(tile_start, tile_count, n_tiles, w_up, w_down, xbuf)


def _combine_kernel(dcur_ref, dnext_ref, h_ref, route_ref, ybuf_ref, g_ref, b_ref, out_ref, ybufs, sem):
    tm = TM_ROW
    i = pl.program_id(0)
    slot = lax.rem(i, 2)

    def gather(d_ref, s):
        def issue(r, c):
            for k in range(TOP_K):
                _tile_copy(ybuf_ref, d_ref[k * tm + r], ybufs.at[s, k], r, sem.at[s, k]).start(priority=k)
            return c

        lax.fori_loop(0, tm, issue, 0, unroll=8)

    @pl.when(i == 0)
    def _():
        gather(dcur_ref, 0)

    @pl.when(i + 1 < pl.num_programs(0))
    def _():
        gather(dnext_ref, 1 - slot)

    for k in range(TOP_K):
        pltpu.make_async_copy(ybuf_ref.at[pl.ds(0, tm * SUB), :], ybufs.at[slot, k], sem.at[slot, k]).wait()
    route = route_ref[...]
    ffn = route[:, 4:5] * _tok_load(ybufs.at[slot, 0], tm) + route[:, 5:6] * _tok_load(ybufs.at[slot, 1], tm)
    out_ref[...] = _layer_norm(ALPHA * _tok_load(h_ref, tm) + ffn, g_ref[...], b_ref[...])


def _combine(dest_flat, h1t, route, ybuf, ln_g, ln_b):
    tm = TM_ROW
    nt = dest_flat.shape[0] // (TOP_K * tm)
    n = nt * tm
    return pl.pallas_call(
        _combine_kernel,
        grid=(nt,),
        in_specs=[
            pl.BlockSpec((TOP_K * tm,), lambda i: (i,), memory_space=pltpu.SMEM),
            pl.BlockSpec((TOP_K * tm,), lambda i: (jnp.minimum(i + 1, nt - 1),), memory_space=pltpu.SMEM),
            pl.BlockSpec((tm * SUB, LANES), lambda i: (i, 0)),
            pl.BlockSpec((tm, LANES), lambda i: (i, 0)),
            pl.BlockSpec(memory_space=pl.ANY),
            pl.BlockSpec(ln_g.shape, lambda i: (0, 0)),
            pl.BlockSpec(ln_b.shape, lambda i: (0, 0)),
        ],
        out_specs=pl.BlockSpec((tm, D_MODEL), lambda i: (i, 0)),
        out_shape=jax.ShapeDtypeStruct((n, D_MODEL), F32),
        scratch_shapes=[pltpu.VMEM((2, TOP_K, tm * SUB, LANES), F32), pltpu.SemaphoreType.DMA((2, TOP_K))],
        compiler_params=pltpu.CompilerParams(dimension_semantics=("arbitrary",)),
        name="combine_ln2",
    )(dest_flat, dest_flat, h1t, route, ybuf, ln_g, ln_b)


def _layer(h, mem, w_in, b_gate, conv_w, w_conv_out, w_pool, pool_scale, w_kv, w_xo, w_o, ln1_g, ln1_b,
           w_rg, b_rg, w_re, b_re, w_up, w_down, ln2_g, ln2_b):
    bn, seq, d = h.shape
    n = bn * seq
    xw = X_HEADS * X_HEAD_DIM

    kt, v = _kv_proj(mem, w_kv[:, :xw].T.astype(BF16), w_kv[:, xw:].astype(BF16))

    pad = LANES - N_EXPERTS - N_GROUPS
    w_r = jnp.concatenate([w_re, w_rg, jnp.zeros((d, pad), F32)], axis=1)
    wr_hi = w_r.astype(BF16)
    wr_lo = (w_r - wr_hi.astype(F32)).astype(BF16)
    b_r = jnp.concatenate([b_re, b_rg, jnp.zeros((pad,), F32)])[None, :]

    h1t, route, route_i, cnt = _mixer(
        h.reshape(n, d), w_in.astype(BF16), b_gate.reshape(3, d), conv_w, w_conv_out.astype(BF16),
        w_pool.astype(BF16), pool_scale[None, :], kt, v, w_xo.astype(BF16), w_o.astype(BF16),
        ln1_g[None, :], ln1_b[None, :], jnp.concatenate([wr_hi, wr_lo], axis=1), b_r, bn, seq)

    counts = cnt[0, :N_EXPERTS]
    padded = (counts + TM_EXP - 1) // TM_EXP * TM_EXP
    pends = jnp.cumsum(padded)
    pstarts = pends - padded
    max_tiles = (n * TOP_K + N_EXPERTS * (TM_EXP - 1)) // TM_EXP
    p_rows = max_tiles * TM_EXP
    n_tiles = (pends[-1] // TM_EXP).astype(jnp.int32)
    tail_ids = n_tiles + jnp.arange(N_EXPERTS, dtype=jnp.int32)
    zero_tiles = jnp.concatenate([
        jnp.where(counts % TM_EXP != 0, pends - TM_EXP, -1),
        jnp.where(tail_ids < max_tiles, tail_ids * TM_EXP, -1)]).astype(jnp.int32)

    dest = _dest(pstarts.astype(jnp.int32), route_i)[:, :TOP_K, :].reshape(-1)
    xbuf = _scatter(zero_tiles, dest, h1t, p_rows)
    ybuf = _experts((pstarts // TM_EXP).astype(jnp.int32), (padded // TM_EXP).astype(jnp.int32), n_tiles.reshape(1),
                    xbuf, w_up, w_down)
    out = _combine(dest, h1t, route, ybuf, ln2_g[None, :], ln2_b[None, :])
    return out.reshape(bn, seq, d)


def kernel(x, mem, w_in, b_gate, conv_w, w_conv_out, w_pool, pool_scale, w_kv, w_xo, w_o, ln1_g, ln1_b,
           w_router_group, b_router_group, w_router_expert, b_router_expert, w_up, w_down, ln2_g, ln2_b):
    h = x
    for l in range(DEPTH):
        h = _layer(h, mem, w_in[l], b_gate[l], conv_w[l], w_conv_out[l], w_pool[l], pool_scale[l], w_kv[l],
                   w_xo[l], w_o[l], ln1_g[l], ln1_b[l], w_router_group[l], b_router_group[l],
                   w_router_expert[l], b_router_expert[l], w_up[l], w_down[l], ln2_g[l], ln2_b[l])
    return h
```

```python
import functools

import jax
import jax.numpy as jnp
from jax import lax
from jax.experimental import pallas as pl
from jax.experimental.pallas import tpu as pltpu

D_MODEL = 1024
CONV_K = 3
POOL_WINDOWS = (2, 4, 8, 16)
POOL_GROUP_DIM = 256
X_HEADS = 4
X_HEAD_DIM = 256
N_GROUPS = 8
EXPERTS_PER_GROUP = 8
N_EXPERTS = 64
TOP_K = 2
D_EXPERT = 512
DEPTH = 1
ALPHA = (2.0 * DEPTH) ** 0.25
LN_EPS = 1e-5

LANES = 128
SUB = 8
assert D_MODEL == SUB * LANES
POOL_HALO = 16
POOL_PAD = 8
assert POOL_WINDOWS[0] == 2 and all(b == 2 * a for a, b in zip(POOL_WINDOWS, POOL_WINDOWS[1:]))
assert POOL_PAD >= POOL_WINDOWS[-2] and POOL_HALO >= POOL_WINDOWS[-1] and len(POOL_WINDOWS) == 4
CONV_HALO = 8
TM_MIX = 512
TM_EXP = 256
TM_ROW = TM_MIX
VMEM_LIMIT = 58 * 1024 * 1024
SCATTER_RING = 3
PAIR = 4
RING = 8
assert RING > PAIR
ROW_DMA_PRIORITY = 1

F32 = jnp.float32
BF16 = jnp.bfloat16


def _dot(a, b):
    return jnp.dot(a, b, preferred_element_type=F32)


def _tok_load(ref, tm):
    return jnp.concatenate([ref[pl.ds(c, tm, stride=SUB), :] for c in range(SUB)], axis=1)


def _tok_store(ref, val, tm):
    for c in range(SUB):
        ref[pl.ds(c, tm, stride=SUB), :] = val[:, c * LANES:(c + 1) * LANES]


def _tile_copy(src, src_tok, dst, dst_tok, sem):
    s = pl.multiple_of(src_tok * SUB, SUB)
    t = pl.multiple_of(dst_tok * SUB, SUB)
    return pltpu.make_async_copy(src.at[pl.ds(s, SUB), :], dst.at[pl.ds(t, SUB), :], sem)


def _layer_norm(h, g, b):
    mu = jnp.mean(h, axis=-1, keepdims=True)
    c = h - mu
    var = jnp.mean(c * c, axis=-1, keepdims=True)
    return c * lax.rsqrt(var + LN_EPS) * g + b


def _kv_kernel(mem_ref, wkt_ref, wv_ref, kt_ref, v_ref):
    mb = mem_ref[0].astype(BF16)
    kt = lax.dot_general(wkt_ref[...], mb, (((1,), (1,)), ((), ())), preferred_element_type=F32)
    kt_ref[0] = kt.astype(BF16)
    v_ref[0] = _dot(mb, wv_ref[...]).astype(BF16)


def _kv_proj(mem, wkt, wv):
    bn, m, d = mem.shape
    return pl.pallas_call(
        _kv_kernel,
        grid=(bn,),
        in_specs=[
            pl.BlockSpec((1, m, d), lambda b: (b, 0, 0)),
            pl.BlockSpec((d, d), lambda b: (0, 0)),
            pl.BlockSpec((d, d), lambda b: (0, 0)),
        ],
        out_specs=[
            pl.BlockSpec((1, d, m), lambda b: (b, 0, 0)),
            pl.BlockSpec((1, m, d), lambda b: (b, 0, 0)),
        ],
        out_shape=[jax.ShapeDtypeStruct((bn, d, m), BF16), jax.ShapeDtypeStruct((bn, m, d), BF16)],
        compiler_params=pltpu.CompilerParams(dimension_semantics=("arbitrary",)),
        name="kv_proj",
    )(mem, wkt, wv)


def _mixer_kernel(x_ref, win_ref, bg_ref, cw_ref, wco_ref, wpool_ref, ps_ref, kt_ref, v_ref, wxo_ref, wo_ref,
                  g1_ref, b1_ref, wr_ref, br_ref,
                  h1_ref, route_ref, ri_ref, cnt_ref,
                  ubuf, pbuf, s1buf, s2buf, s3buf, carry, hpre, *, tiles_per_seq):
    tm = TM_MIX
    d = D_MODEL
    t = pl.program_id(0)
    s = lax.rem(t, tiles_per_seq)

    @pl.when(s == 0)
    def _():
        ubuf[0:CONV_HALO, :] = jnp.zeros((CONV_HALO, d), F32)
        pbuf[0:POOL_PAD + POOL_HALO, :] = jnp.zeros((POOL_PAD + POOL_HALO, d), F32)
        for sbuf in (s1buf, s2buf, s3buf):
            sbuf[0:POOL_PAD, :] = jnp.zeros((POOL_PAD, sbuf.shape[1]), F32)

    @pl.when(t == 0)
    def _():
        carry[...] = jnp.zeros_like(carry)
        hpre[...] = jnp.zeros_like(hpre)

    route_args = (hpre, g1_ref, b1_ref, wr_ref, br_ref, h1_ref, route_ref, ri_ref, cnt_ref, carry)
    last = pl.num_programs(0) - 1

    @pl.when(t < last)
    def _():
        _mix_tile(t, s, x_ref, win_ref, bg_ref, cw_ref, wco_ref, wpool_ref, ps_ref, kt_ref, v_ref, wxo_ref, wo_ref,
                  ubuf, pbuf, s1buf, s2buf, s3buf, route_args)

    @pl.when(t == last)
    def _():
        for _ in _route_stages(True, *route_args):
            pass


def _mix_tile(t, s, x_ref, win_ref, bg_ref, cw_ref, wco_ref, wpool_ref, ps_ref, kt_ref, v_ref, wxo_ref, wo_ref,
              ubuf, pbuf, s1buf, s2buf, s3buf, route_args):
    tm = TM_MIX
    d = D_MODEL
    hpre = route_args[0]
    routing = _route_stages(t >= 1, *route_args)

    x = x_ref[...]
    xb = x.astype(BF16)

    def proj(sec):
        return _dot(xb, win_ref[:, sec * d:(sec + 1) * d])

    def gate(i):
        return jax.nn.sigmoid(proj(5 + i) + bg_ref[i:i + 1, :])

    u = proj(1) * proj(2)
    next(routing)
    ubuf[CONV_HALO:CONV_HALO + tm, :] = u
    v = cw_ref[CONV_K - 1:CONV_K, :] * u
    for k in range(CONV_K - 1):
        off = CONV_HALO - (CONV_K - 1) + k
        v = v + cw_ref[k:k + 1, :] * ubuf[off:off + tm, :]
    ubuf[0:CONV_HALO, :] = ubuf[tm:tm + CONV_HALO, :]
    y_conv = _dot((proj(0) * v).astype(BF16), wco_ref[...])
    acc = gate(0) * y_conv
    next(routing)

    p = proj(3)
    base = POOL_PAD + POOL_HALO
    n = POOL_HALO + tm
    pbuf[base:base + tm, :] = p
    lvl = pbuf[POOL_PAD:POOL_PAD + n, :] + pbuf[POOL_PAD - 1:POOL_PAD - 1 + n, :]
    sums = [lvl[POOL_HALO:, :POOL_GROUP_DIM]]
    for gi, sbuf in enumerate((s1buf, s2buf, s3buf), start=1):
        shift = POOL_WINDOWS[gi - 1]
        rest = lvl[:, POOL_GROUP_DIM:]
        sbuf[POOL_PAD:POOL_PAD + n, :] = rest
        lvl = rest + sbuf[POOL_PAD - shift:POOL_PAD - shift + n, :]
        sums.append(lvl[POOL_HALO:, :POOL_GROUP_DIM])
    pos = (s * tm + 1 + lax.broadcasted_iota(jnp.int32, (tm, 1), 0)).astype(F32)
    ys = []
    for gi, w in enumerate(POOL_WINDOWS):
        c0 = gi * POOL_GROUP_DIM
        inv_cnt = 1.0 / jnp.minimum(pos, float(w))
        dg = (sums[gi] * inv_cnt - p[:, c0:c0 + POOL_GROUP_DIM]).astype(BF16)
        ys.append(_dot(dg, wpool_ref[gi]))
    pbuf[POOL_PAD:base, :] = pbuf[POOL_PAD + tm:base + tm, :]
    y_pool = jnp.concatenate(ys, axis=1) * ps_ref[...]
    acc = acc + gate(1) * y_pool
    next(routing)

    qb = proj(4).astype(BF16)
    scale = X_HEAD_DIM ** -0.5
    os_ = []
    for h in range(X_HEADS):
        c0 = h * X_HEAD_DIM
        c1 = c0 + X_HEAD_DIM
        sc = _dot(qb[:, c0:c1], kt_ref[0, c0:c1, :]) * scale
        e = jnp.exp(sc - jnp.max(sc, axis=-1, keepdims=True))
        a = e * (1.0 / jnp.sum(e, axis=-1, keepdims=True))
        os_.append(_dot(a.astype(BF16), v_ref[0, :, c0:c1]))
    y_mem = _dot(jnp.concatenate(os_, axis=1).astype(BF16), wxo_ref[...])
    acc = acc + gate(2) * y_mem
    next(routing)

    hpre[...] = ALPHA * x + _dot(acc.astype(BF16), wo_ref[...])


def _route_stages(valid, hpre, g1_ref, b1_ref, wr_ref, br_ref, h1_ref, route_ref, ri_ref, cnt_ref, carry):
    tm = TM_MIX
    h1 = _layer_norm(hpre[...], g1_ref[...], b1_ref[...])
    _tok_store(h1_ref, h1, tm)
    yield

    hh = h1.astype(BF16)
    hl = (h1 - hh.astype(F32)).astype(BF16)
    l2 = _dot(hh, wr_ref[...]) + _dot(hl, wr_ref[...])
    logits = l2[:, :LANES] + l2[:, LANES:] + br_ref[...]
    yield
    lane = lax.broadcasted_iota(jnp.int32, (tm, LANES), 1)
    lane_f = lane.astype(F32)
    neg = jnp.float32(-jnp.inf)
    big = jnp.float32(1e9)

    def first_argmax(vals):
        m = jnp.max(vals, axis=-1, keepdims=True)
        idx = jnp.min(jnp.where(vals == m, lane_f, big), axis=-1, keepdims=True)
        return m, idx

    is_g = (lane >= N_EXPERTS) & (lane < N_EXPERTS + N_GROUPS)
    gmax, gidx = first_argmax(jnp.where(is_g, logits, neg))
    g_w = 1.0 / jnp.sum(jnp.where(is_g, jnp.exp(logits - gmax), 0.0), axis=-1, keepdims=True)
    gsel = gidx.astype(jnp.int32) - N_EXPERTS
    in_grp = (lane >> 3) == gsel
    le = jnp.where(in_grp, logits, neg)
    m1, i1 = first_argmax(le)
    m2, i2 = first_argmax(jnp.where(lane_f == i1, neg, le))
    t = jnp.exp(m2 - m1)
    den = 1.0 + t
    w1 = g_w / den
    w2 = g_w * t / den
    yield

    sel1 = lane_f == i1
    sel2 = lane_f == i2
    onehot = (sel1 | sel2).astype(BF16)
    row_i = lax.broadcasted_iota(jnp.int32, (tm, tm), 0)
    col_i = lax.broadcasted_iota(jnp.int32, (tm, tm), 1)
    before = (col_i < row_i).astype(BF16)
    prior = _dot(before, onehot) + carry[0:1, :]
    r1 = jnp.sum(jnp.where(sel1, prior, 0.0), axis=-1, keepdims=True)
    r2 = jnp.sum(jnp.where(sel2, prior, 0.0), axis=-1, keepdims=True)
    carry[...] = carry[...] + jnp.where(valid, jnp.sum(onehot.astype(F32), axis=0, keepdims=True), 0.0)
    cnt_ref[...] = carry[...].astype(jnp.int32)

    route = jnp.where(lane == 0, i1, 0.0)
    for k, col in enumerate((i2, r1, r2, w1, w2), start=1):
        route = jnp.where(lane == k, col, route)
    route_ref[...] = route
    ri_ref[0] = route.T[0:8, :].astype(jnp.int32)
    yield


def _mixer(x2, w_in_b, b_gate3, conv_w, wco_b, wpool_b, pool_scale, kt, v, wxo_b, wo_b, ln_g, ln_b, wr, br,
           bn, seq):
    n, d = x2.shape
    tm = TM_MIX
    spb = seq // tm
    nt = bn * spb
    const2 = lambda t: (0, 0)
    const3 = lambda t: (0, 0, 0)
    one = pl.Buffered(1)
    mixed = lambda t: jnp.minimum(t, nt - 1)
    routed = lambda t: jnp.maximum(t - 1, 0)
    mem_map = lambda t: (mixed(t) // spb, 0, 0)
    row_map = lambda t: (routed(t), 0)
    return pl.pallas_call(
        functools.partial(_mixer_kernel, tiles_per_seq=spb),
        grid=(nt + 1,),
        in_specs=[
            pl.BlockSpec((tm, d), lambda t: (mixed(t), 0)),
            pl.BlockSpec(w_in_b.shape, const2, pipeline_mode=one),
            pl.BlockSpec(b_gate3.shape, const2, pipeline_mode=one),
            pl.BlockSpec(conv_w.shape, const2, pipeline_mode=one),
            pl.BlockSpec(wco_b.shape, const2, pipeline_mode=one),
            pl.BlockSpec(wpool_b.shape, const3, pipeline_mode=one),
            pl.BlockSpec(pool_scale.shape, const2, pipeline_mode=one),
            pl.BlockSpec((1,) + kt.shape[1:], mem_map),
            pl.BlockSpec((1,) + v.shape[1:], mem_map),
            pl.BlockSpec(wxo_b.shape, const2, pipeline_mode=one),
            pl.BlockSpec(wo_b.shape, const2, pipeline_mode=one),
            pl.BlockSpec(ln_g.shape, const2, pipeline_mode=one),
            pl.BlockSpec(ln_b.shape, const2, pipeline_mode=one),
            pl.BlockSpec(wr.shape, const2, pipeline_mode=one),
            pl.BlockSpec(br.shape, const2, pipeline_mode=one),
        ],
        out_specs=[
            pl.BlockSpec((tm * SUB, LANES), row_map),
            pl.BlockSpec((tm, LANES), row_map),
            pl.BlockSpec((1, 8, tm), lambda t: (routed(t), 0, 0)),
            pl.BlockSpec((8, LANES), const2),
        ],
        out_shape=[
            jax.ShapeDtypeStruct((n * SUB, LANES), F32),
            jax.ShapeDtypeStruct((n, LANES), F32),
            jax.ShapeDtypeStruct((n // tm, 8, tm), jnp.int32),
            jax.ShapeDtypeStruct((8, LANES), jnp.int32),
        ],
        scratch_shapes=[
            pltpu.VMEM((tm + CONV_HALO, d), F32),
            pltpu.VMEM((POOL_PAD + POOL_HALO + tm, d), F32),
            pltpu.VMEM((POOL_PAD + POOL_HALO + tm, d - POOL_GROUP_DIM), F32),
            pltpu.VMEM((POOL_PAD + POOL_HALO + tm, d - 2 * POOL_GROUP_DIM), F32),
            pltpu.VMEM((POOL_PAD + POOL_HALO + tm, d - 3 * POOL_GROUP_DIM), F32),
            pltpu.VMEM((8, LANES), F32),
            pltpu.VMEM((tm, d), F32),
        ],
        compiler_params=pltpu.CompilerParams(dimension_semantics=("arbitrary",), vmem_limit_bytes=VMEM_LIMIT),
        name="mixer",
    )(x2, w_in_b, b_gate3, conv_w, wco_b, wpool_b, pool_scale, kt, v, wxo_b, wo_b, ln_g, ln_b, wr, br)


def _dest_kernel(ps_ref, ri_ref, dest_ref):
    dest_ref[...] = jnp.zeros(dest_ref.shape, jnp.int32)
    for t in range(dest_ref.shape[0]):
        ri = ri_ref[t]
        e = ri[0:TOP_K, :]
        start = jnp.zeros_like(e)
        for j in range(N_EXPERTS):
            start = jnp.where(e == j, ps_ref[j], start)
        dest_ref[t, 0:TOP_K, :] = start + ri[TOP_K:2 * TOP_K, :]


def _dest(pstarts, route_i):
    nt, rows, tm = route_i.shape
    blk = 8
    spec = pl.BlockSpec((blk, rows, tm), lambda i, ps: (i, 0, 0))
    return pl.pallas_call(
        _dest_kernel,
        grid_spec=pltpu.PrefetchScalarGridSpec(
            num_scalar_prefetch=1, grid=(nt // blk,), in_specs=[spec], out_specs=spec),
        out_shape=jax.ShapeDtypeStruct(route_i.shape, jnp.int32),
        compiler_params=pltpu.CompilerParams(dimension_semantics=("arbitrary",)),
        name="dest_rows",
    )(pstarts, route_i)


def _scatter_kernel(zt_ref, dest_ref, h_ref, xbuf_ref, zbuf, stage, zsem, fsem, ssem):
    i = pl.program_id(0)
    nsteps = pl.num_programs(0)
    tm = TM_ROW
    rows = tm * SUB
    slot = lax.rem(i, SCATTER_RING)

    def zero_copy(e):
        start = pl.multiple_of(zt_ref[e] * SUB, TM_EXP * SUB)
        return pltpu.make_async_copy(zbuf, xbuf_ref.at[pl.ds(start, TM_EXP * SUB), :], zsem.at[0])

    def fetch(t, s):
        src = h_ref.at[pl.ds(pl.multiple_of(t * rows, rows), rows), :]
        return pltpu.make_async_copy(src, stage.at[s], fsem.at[s])

    def drain(s):
        for k in range(TOP_K):
            pltpu.make_async_copy(stage.at[s], xbuf_ref.at[pl.ds(0, rows), :], ssem.at[s, k]).wait()

    @pl.when(i == 0)
    def _():
        zbuf[...] = jnp.zeros_like(zbuf)

        def start(e, c):
            @pl.when(zt_ref[e] >= 0)
            def _():
                zero_copy(e).start()
            return c

        def wait(e, c):
            @pl.when(zt_ref[e] >= 0)
            def _():
                zero_copy(e).wait()
            return c

        lax.fori_loop(0, 2 * N_EXPERTS, start, 0)
        fetch(0, 0).start()
        lax.fori_loop(0, 2 * N_EXPERTS, wait, 0)

    @pl.when(i >= 2)
    def _():
        drain(lax.rem(i + 1, SCATTER_RING))

    @pl.when(i + 1 < nsteps)
    def _():
        fetch(i + 1, lax.rem(i + 1, SCATTER_RING)).start()

    fetch(i, slot).wait()

    def issue(r, c):
        for k in range(TOP_K):
            _tile_copy(stage.at[slot], r, xbuf_ref, dest_ref[k * tm + r], ssem.at[slot, k]).start(priority=k)
        return c

    lax.fori_loop(0, tm, issue, 0, unroll=8)

    @pl.when(i == nsteps - 1)
    def _():
        @pl.when(i >= 1)
        def _():
            drain(lax.rem(i + 2, SCATTER_RING))

        drain(slot)


def _scatter(zero_tiles, dest_flat, h1t, p_rows):
    tm = TM_ROW
    nt = dest_flat.shape[0] // (TOP_K * tm)
    return pl.pallas_call(
        _scatter_kernel,
        grid_spec=pltpu.PrefetchScalarGridSpec(
            num_scalar_prefetch=1,
            grid=(nt,),
            in_specs=[
                pl.BlockSpec((TOP_K * tm,), lambda i, zt: (i,), memory_space=pltpu.SMEM),
                pl.BlockSpec(memory_space=pl.ANY),
            ],
            out_specs=pl.BlockSpec(memory_space=pl.ANY),
            scratch_shapes=[
                pltpu.VMEM((TM_EXP * SUB, LANES), F32),
                pltpu.VMEM((SCATTER_RING, tm * SUB, LANES), F32),
                pltpu.SemaphoreType.DMA((1,)),
                pltpu.SemaphoreType.DMA((SCATTER_RING,)),
                pltpu.SemaphoreType.DMA((SCATTER_RING, TOP_K)),
            ],
        ),
        out_shape=jax.ShapeDtypeStruct((p_rows * SUB, LANES), F32),
        compiler_params=pltpu.CompilerParams(dimension_semantics=("arbitrary",)),
        name="scatter_rows",
    )(zero_tiles, dest_flat, h1t)


def _expert_kernel(t0_ref, nt_ref, tot_ref, wup_ref, wdn_ref, xbuf_ref, ybuf_ref, wup_b, wdn_b, xs, ys, sem_in, sem_out):
    e = pl.program_id(0)
    t0 = t0_ref[e]
    nt = nt_ref[e]
    total = tot_ref[0]
    rows = TM_EXP * SUB
    max_tiles = xbuf_ref.shape[0] // rows

    def hbm_tile(ref, g):
        return ref.at[pl.ds(pl.multiple_of(g * rows, rows), rows), :]

    def x_copy(g, slot):
        return pltpu.make_async_copy(hbm_tile(xbuf_ref, g), xs.at[slot], sem_in.at[slot])

    def y_copy(g, slot):
        return pltpu.make_async_copy(ys.at[slot], hbm_tile(ybuf_ref, g), sem_out.at[slot])

    ahead = RING - PAIR

    @pl.when(e == 0)
    def _():
        for g0 in range(ahead):
            @pl.when(g0 < total)
            def _():
                x_copy(g0, g0).start(priority=ROW_DMA_PRIORITY)

    @pl.when(nt > 0)
    def _():
        wup_b[...] = wup_ref[0].astype(BF16)
        wdn_b[...] = wdn_ref[0].astype(BF16)

    def process(g, width):
        slots = [lax.rem(g + w, RING) for w in range(width)]
        for w in range(width):
            @pl.when(g + w + ahead < total)
            def _(w=w):
                x_copy(g + w + ahead, lax.rem(g + w + ahead, RING)).start(priority=ROW_DMA_PRIORITY)

        for w in range(width):
            x_copy(g + w, slots[w]).wait()

            @pl.when(g + w >= RING)
            def _(w=w):
                y_copy(g + w - RING, slots[w]).wait()

        xb = jnp.concatenate([_tok_load(xs.at[s], TM_EXP) for s in slots], axis=0).astype(BF16)
        hgv = _dot(xb, wup_b[...])
        hg = hgv[:, :D_EXPERT]
        hv = hgv[:, D_EXPERT:]
        act = (hg * jax.nn.sigmoid(hg)) * hv
        y = _dot(act.astype(BF16), wdn_b[...])
        for w in range(width):
            _tok_store(ys.at[slots[w]], y[w * TM_EXP:(w + 1) * TM_EXP], TM_EXP)
            y_copy(g + w, slots[w]).start(priority=ROW_DMA_PRIORITY)

    def chain(jp, c):
        process(t0 + PAIR * jp, PAIR)
        return c

    def single(j, c):
        process(t0 + j, 1)
        return c

    full = nt // PAIR
    lax.fori_loop(0, full, chain, 0)
    lax.fori_loop(full * PAIR, nt, single, 0)

    @pl.when(e == pl.num_programs(0) - 1)
    def _():
        for back in range(RING, 0, -1):
            @pl.when(total >= back)
            def _():
                y_copy(total - back, lax.rem(total - back, RING)).wait()

        ys[0] = jnp.zeros(ys.shape[1:], F32)

        def zero_tail(g, c):
            y_copy(g, 0).start()
            y_copy(g, 0).wait()
            return c

        lax.fori_loop(total, max_tiles, zero_tail, 0)


def _experts(tile_start, tile_count, n_tiles, xbuf, w_up, w_down):
    w_map = lambda e, t0, nt, tot: (e, 0, 0)
    rows = TM_EXP * SUB
    return pl.pallas_call(
        _expert_kernel,
        grid_spec=pltpu.PrefetchScalarGridSpec(
            num_scalar_prefetch=3,
            grid=(N_EXPERTS,),
            in_specs=[
                pl.BlockSpec((1,) + w_up.shape[1:], w_map),
                pl.BlockSpec((1,) + w_down.shape[1:], w_map),
                pl.BlockSpec(memory_space=pl.ANY),
            ],
            out_specs=pl.BlockSpec(memory_space=pl.ANY),
            scratch_shapes=[
                pltpu.VMEM(w_up.shape[1:], BF16), pltpu.VMEM(w_down.shape[1:], BF16),
                pltpu.VMEM((RING, rows, LANES), F32), pltpu.VMEM((RING, rows, LANES), F32),
                pltpu.SemaphoreType.DMA((RING,)), pltpu.SemaphoreType.DMA((RING,)),
            ],
        ),
        out_shape=jax.ShapeDtypeStruct(xbuf.shape, F32),
        compiler_params=pltpu.CompilerParams(dimension_semantics=("arbitrary",), vmem_limit_bytes=VMEM_LIMIT),
        name="experts",
    )(tile_start, tile_count, n_tiles, w_up, w_down, xbuf)


def _combine_kernel(dcur_ref, dnext_ref, h_ref, route_ref, ybuf_ref, g_ref, b_ref, out_ref, ybufs, sem):
    tm = TM_ROW
    i = pl.program_id(0)
    slot = lax.rem(i, 2)

    def gather(d_ref, s):
        def issue(r, c):
            for k in range(TOP_K):
                _tile_copy(ybuf_ref, d_ref[k * tm + r], ybufs.at[s, k], r, sem.at[s, k]).start(priority=k)
            return c

        lax.fori_loop(0, tm, issue, 0, unroll=8)

    @pl.when(i == 0)
    def _():
        gather(dcur_ref, 0)

    @pl.when(i + 1 < pl.num_programs(0))
    def _():
        gather(dnext_ref, 1 - slot)

    for k in range(TOP_K):
        pltpu.make_async_copy(ybuf_ref.at[pl.ds(0, tm * SUB), :], ybufs.at[slot, k], sem.at[slot, k]).wait()
    route = route_ref[...]
    ffn = route[:, 4:5] * _tok_load(ybufs.at[slot, 0], tm) + route[:, 5:6] * _tok_load(ybufs.at[slot, 1], tm)
    out_ref[...] = _layer_norm(ALPHA * _tok_load(h_ref, tm) + ffn, g_ref[...], b_ref[...])


def _combine(dest_flat, h1t, route, ybuf, ln_g, ln_b):
    tm = TM_ROW
    nt = dest_flat.shape[0] // (TOP_K * tm)
    n = nt * tm
    return pl.pallas_call(
        _combine_kernel,
        grid=(nt,),
        in_specs=[
            pl.BlockSpec((TOP_K * tm,), lambda i: (i,), memory_space=pltpu.SMEM),
            pl.BlockSpec((TOP_K * tm,), lambda i: (jnp.minimum(i + 1, nt - 1),), memory_space=pltpu.SMEM),
            pl.BlockSpec((tm * SUB, LANES), lambda i: (i, 0)),
            pl.BlockSpec((tm, LANES), lambda i: (i, 0)),
            pl.BlockSpec(memory_space=pl.ANY),
            pl.BlockSpec(ln_g.shape, lambda i: (0, 0)),
            pl.BlockSpec(ln_b.shape, lambda i: (0, 0)),
        ],
        out_specs=pl.BlockSpec((tm, D_MODEL), lambda i: (i, 0)),
        out_shape=jax.ShapeDtypeStruct((n, D_MODEL), F32),
        scratch_shapes=[pltpu.VMEM((2, TOP_K, tm * SUB, LANES), F32), pltpu.SemaphoreType.DMA((2, TOP_K))],
        compiler_params=pltpu.CompilerParams(dimension_semantics=("arbitrary",)),
        name="combine_ln2",
    )(dest_flat, dest_flat, h1t, route, ybuf, ln_g, ln_b)


def _layer(h, mem, w_in, b_gate, conv_w, w_conv_out, w_pool, pool_scale, w_kv, w_xo, w_o, ln1_g, ln1_b,
           w_rg, b_rg, w_re, b_re, w_up, w_down, ln2_g, ln2_b):
    bn, seq, d = h.shape
    n = bn * seq
    xw = X_HEADS * X_HEAD_DIM

    kt, v = _kv_proj(mem, w_kv[:, :xw].T.astype(BF16), w_kv[:, xw:].astype(BF16))

    pad = LANES - N_EXPERTS - N_GROUPS
    w_r = jnp.concatenate([w_re, w_rg, jnp.zeros((d, pad), F32)], axis=1)
    wr_hi = w_r.astype(BF16)
    wr_lo = (w_r - wr_hi.astype(F32)).astype(BF16)
    b_r = jnp.concatenate([b_re, b_rg, jnp.zeros((pad,), F32)])[None, :]

    h1t, route, route_i, cnt = _mixer(
        h.reshape(n, d), w_in.astype(BF16), b_gate.reshape(3, d), conv_w, w_conv_out.astype(BF16),
        w_pool.astype(BF16), pool_scale[None, :], kt, v, w_xo.astype(BF16), w_o.astype(BF16),
        ln1_g[None, :], ln1_b[None, :], jnp.concatenate([wr_hi, wr_lo], axis=1), b_r, bn, seq)

    counts = cnt[0, :N_EXPERTS]
    padded = (counts + TM_EXP - 1) // TM_EXP * TM_EXP
    pends = jnp.cumsum(padded)
    pstarts = pends - padded
    max_tiles = (n * TOP_K + N_EXPERTS * (TM_EXP - 1)) // TM_EXP
    p_rows = max_tiles * TM_EXP
    n_tiles = (pends[-1] // TM_EXP).astype(jnp.int32)
    tail_ids = n_tiles + jnp.arange(N_EXPERTS, dtype=jnp.int32)
    zero_tiles = jnp.concatenate([
        jnp.where(counts % TM_EXP != 0, pends - TM_EXP, -1),
        jnp.where(tail_ids < max_tiles, tail_ids * TM_EXP, -1)]).astype(jnp.int32)

    dest = _dest(pstarts.astype(jnp.int32), route_i)[:, :TOP_K, :].reshape(-1)
    xbuf = _scatter(zero_tiles, dest, h1t, p_rows)
    ybuf = _experts((pstarts // TM_EXP).astype(jnp.int32), (padded // TM_EXP).astype(jnp.int32), n_tiles.reshape(1),
                    xbuf, w_up, w_down)
    out = _combine(dest, h1t, route, ybuf, ln2_g[None, :], ln2_b[None, :])
    return out.reshape(bn, seq, d)


def kernel(x, mem, w_in, b_gate, conv_w, w_conv_out, w_pool, pool_scale, w_kv, w_xo, w_o, ln1_g, ln1_b,
           w_router_group, b_router_group, w_router_expert, b_router_expert, w_up, w_down, ln2_g, ln2_b):
    h = x
    for l in range(DEPTH):
        h = _layer(h, mem, w_in[l], b_gate[l], conv_w[l], w_conv_out[l], w_pool[l], pool_scale[l], w_kv[l],
                   w_xo[l], w_o[l], ln1_g[l], ln1_b[l], w_router_group[l], b_router_group[l],
                   w_router_expert[l], b_router_expert[l], w_up[l], w_down[l], ln2_g[l], ln2_b[l])
    return h
```

```python
import functools

import jax
import jax.numpy as jnp
from jax import lax
from jax.experimental import pallas as pl
from jax.experimental.pallas import tpu as pltpu

D_MODEL = 1024
CONV_K = 3
POOL_WINDOWS = (2, 4, 8, 16)
POOL_GROUP_DIM = 256
X_HEADS = 4
X_HEAD_DIM = 256
N_GROUPS = 8
EXPERTS_PER_GROUP = 8
N_EXPERTS = 64
TOP_K = 2
D_EXPERT = 512
DEPTH = 1
ALPHA = (2.0 * DEPTH) ** 0.25
LN_EPS = 1e-5

LANES = 128
SUB = 8
assert D_MODEL == SUB * LANES
POOL_HALO = 16
POOL_PAD = 8
assert POOL_WINDOWS[0] == 2 and all(b == 2 * a for a, b in zip(POOL_WINDOWS, POOL_WINDOWS[1:]))
assert POOL_PAD >= POOL_WINDOWS[-2] and POOL_HALO >= POOL_WINDOWS[-1] and len(POOL_WINDOWS) == 4
CONV_HALO = 8
TM_MIX = 512
TM_EXP = 256
TM_ROW = TM_MIX
VMEM_LIMIT = 58 * 1024 * 1024
SCATTER_RING = 3
PAIR = 2
RING = 6
assert PAIR == 2 and RING > PAIR
ROW_DMA_PRIORITY = 1

F32 = jnp.float32
BF16 = jnp.bfloat16


def _dot(a, b):
    return jnp.dot(a, b, preferred_element_type=F32)


def _tok_load(ref, tm):
    return jnp.concatenate([ref[pl.ds(c, tm, stride=SUB), :] for c in range(SUB)], axis=1)


def _tok_store(ref, val, tm):
    for c in range(SUB):
        ref[pl.ds(c, tm, stride=SUB), :] = val[:, c * LANES:(c + 1) * LANES]


def _tile_copy(src, src_tok, dst, dst_tok, sem):
    s = pl.multiple_of(src_tok * SUB, SUB)
    t = pl.multiple_of(dst_tok * SUB, SUB)
    return pltpu.make_async_copy(src.at[pl.ds(s, SUB), :], dst.at[pl.ds(t, SUB), :], sem)


def _layer_norm(h, g, b):
    mu = jnp.mean(h, axis=-1, keepdims=True)
    c = h - mu
    var = jnp.mean(c * c, axis=-1, keepdims=True)
    return c * lax.rsqrt(var + LN_EPS) * g + b


def _kv_kernel(mem_ref, wkt_ref, wv_ref, kt_ref, v_ref):
    mb = mem_ref[0].astype(BF16)
    kt = lax.dot_general(wkt_ref[...], mb, (((1,), (1,)), ((), ())), preferred_element_type=F32)
    kt_ref[0] = kt.astype(BF16)
    v_ref[0] = _dot(mb, wv_ref[...]).astype(BF16)


def _kv_proj(mem, wkt, wv):
    bn, m, d = mem.shape
    return pl.pallas_call(
        _kv_kernel,
        grid=(bn,),
        in_specs=[
            pl.BlockSpec((1, m, d), lambda b: (b, 0, 0)),
            pl.BlockSpec((d, d), lambda b: (0, 0)),
            pl.BlockSpec((d, d), lambda b: (0, 0)),
        ],
        out_specs=[
            pl.BlockSpec((1, d, m), lambda b: (b, 0, 0)),
            pl.BlockSpec((1, m, d), lambda b: (b, 0, 0)),
        ],
        out_shape=[jax.ShapeDtypeStruct((bn, d, m), BF16), jax.ShapeDtypeStruct((bn, m, d), BF16)],
        compiler_params=pltpu.CompilerParams(dimension_semantics=("arbitrary",)),
        name="kv_proj",
    )(mem, wkt, wv)


def _mixer_kernel(x_ref, win_ref, bg_ref, cw_ref, wco_ref, wpool_ref, ps_ref, kt_ref, v_ref, wxo_ref, wo_ref,
                  g1_ref, b1_ref, wr_ref, br_ref,
                  h1_ref, route_ref, ri_ref, cnt_ref,
                  ubuf, pbuf, s1buf, s2buf, s3buf, carry, hpre, *, tiles_per_seq):
    tm = TM_MIX
    d = D_MODEL
    t = pl.program_id(0)
    s = lax.rem(t, tiles_per_seq)

    @pl.when(s == 0)
    def _():
        ubuf[0:CONV_HALO, :] = jnp.zeros((CONV_HALO, d), F32)
        pbuf[0:POOL_PAD + POOL_HALO, :] = jnp.zeros((POOL_PAD + POOL_HALO, d), F32)
        for sbuf in (s1buf, s2buf, s3buf):
            sbuf[0:POOL_PAD, :] = jnp.zeros((POOL_PAD, sbuf.shape[1]), F32)

    @pl.when(t == 0)
    def _():
        carry[...] = jnp.zeros_like(carry)
        hpre[...] = jnp.zeros_like(hpre)

    route_args = (hpre, g1_ref, b1_ref, wr_ref, br_ref, h1_ref, route_ref, ri_ref, cnt_ref, carry)
    last = pl.num_programs(0) - 1

    @pl.when(t < last)
    def _():
        _mix_tile(t, s, x_ref, win_ref, bg_ref, cw_ref, wco_ref, wpool_ref, ps_ref, kt_ref, v_ref, wxo_ref, wo_ref,
                  ubuf, pbuf, s1buf, s2buf, s3buf, route_args)

    @pl.when(t == last)
    def _():
        for _ in _route_stages(True, *route_args):
            pass


def _mix_tile(t, s, x_ref, win_ref, bg_ref, cw_ref, wco_ref, wpool_ref, ps_ref, kt_ref, v_ref, wxo_ref, wo_ref,
              ubuf, pbuf, s1buf, s2buf, s3buf, route_args):
    tm = TM_MIX
    d = D_MODEL
    hpre = route_args[0]
    routing = _route_stages(t >= 1, *route_args)

    x = x_ref[...]
    xb = x.astype(BF16)

    def proj(sec):
        return _dot(xb, win_ref[:, sec * d:(sec + 1) * d])

    def gate(i):
        return jax.nn.sigmoid(proj(5 + i) + bg_ref[i:i + 1, :])

    u = proj(1) * proj(2)
    next(routing)
    ubuf[CONV_HALO:CONV_HALO + tm, :] = u
    v = cw_ref[CONV_K - 1:CONV_K, :] * u
    for k in range(CONV_K - 1):
        off = CONV_HALO - (CONV_K - 1) + k
        v = v + cw_ref[k:k + 1, :] * ubuf[off:off + tm, :]
    ubuf[0:CONV_HALO, :] = ubuf[tm:tm + CONV_HALO, :]
    y_conv = _dot((proj(0) * v).astype(BF16), wco_ref[...])
    acc = gate(0) * y_conv
    next(routing)

    p = proj(3)
    base = POOL_PAD + POOL_HALO
    n = POOL_HALO + tm
    pbuf[base:base + tm, :] = p
    lvl = pbuf[POOL_PAD:POOL_PAD + n, :] + pbuf[POOL_PAD - 1:POOL_PAD - 1 + n, :]
    sums = [lvl[POOL_HALO:, :POOL_GROUP_DIM]]
    for gi, sbuf in enumerate((s1buf, s2buf, s3buf), start=1):
        shift = POOL_WINDOWS[gi - 1]
        rest = lvl[:, POOL_GROUP_DIM:]
        sbuf[POOL_PAD:POOL_PAD + n, :] = rest
        lvl = rest + sbuf[POOL_PAD - shift:POOL_PAD - shift + n, :]
        sums.append(lvl[POOL_HALO:, :POOL_GROUP_DIM])
    pos = (s * tm + 1 + lax.broadcasted_iota(jnp.int32, (tm, 1), 0)).astype(F32)
    ys = []
    for gi, w in enumerate(POOL_WINDOWS):
        c0 = gi * POOL_GROUP_DIM
        inv_cnt = 1.0 / jnp.minimum(pos, float(w))
        dg = (sums[gi] * inv_cnt - p[:, c0:c0 + POOL_GROUP_DIM]).astype(BF16)
        ys.append(_dot(dg, wpool_ref[gi]))
    pbuf[POOL_PAD:base, :] = pbuf[POOL_PAD + tm:base + tm, :]
    y_pool = jnp.concatenate(ys, axis=1) * ps_ref[...]
    acc = acc + gate(1) * y_pool
    next(routing)

    qb = proj(4).astype(BF16)
    scale = X_HEAD_DIM ** -0.5
    os_ = []
    for h in range(X_HEADS):
        c0 = h * X_HEAD_DIM
        c1 = c0 + X_HEAD_DIM
        sc = _dot(qb[:, c0:c1], kt_ref[0, c0:c1, :]) * scale
        e = jnp.exp(sc - jnp.max(sc, axis=-1, keepdims=True))
        a = e * (1.0 / jnp.sum(e, axis=-1, keepdims=True))
        os_.append(_dot(a.astype(BF16), v_ref[0, :, c0:c1]))
    y_mem = _dot(jnp.concatenate(os_, axis=1).astype(BF16), wxo_ref[...])
    acc = acc + gate(2) * y_mem
    next(routing)

    hpre[...] = ALPHA * x + _dot(acc.astype(BF16), wo_ref[...])


def _route_stages(valid, hpre, g1_ref, b1_ref, wr_ref, br_ref, h1_ref, route_ref, ri_ref, cnt_ref, carry):
    tm = TM_MIX
    h1 = _layer_norm(hpre[...], g1_ref[...], b1_ref[...])
    _tok_store(h1_ref, h1, tm)
    yield

    hh = h1.astype(BF16)
    hl = (h1 - hh.astype(F32)).astype(BF16)
    l2 = _dot(hh, wr_ref[...]) + _dot(hl, wr_ref[...])
    logits = l2[:, :LANES] + l2[:, LANES:] + br_ref[...]
    yield
    lane = lax.broadcasted_iota(jnp.int32, (tm, LANES), 1)
    lane_f = lane.astype(F32)
    neg = jnp.float32(-jnp.inf)
    big = jnp.float32(1e9)

    def first_argmax(vals):
        m = jnp.max(vals, axis=-1, keepdims=True)
        idx = jnp.min(jnp.where(vals == m, lane_f, big), axis=-1, keepdims=True)
        return m, idx

    is_g = (lane >= N_EXPERTS) & (lane < N_EXPERTS + N_GROUPS)
    gmax, gidx = first_argmax(jnp.where(is_g, logits, neg))
    g_w = 1.0 / jnp.sum(jnp.where(is_g, jnp.exp(logits - gmax), 0.0), axis=-1, keepdims=True)
    gsel = gidx.astype(jnp.int32) - N_EXPERTS
    in_grp = (lane >> 3) == gsel
    le = jnp.where(in_grp, logits, neg)
    m1, i1 = first_argmax(le)
    m2, i2 = first_argmax(jnp.where(lane_f == i1, neg, le))
    t = jnp.exp(m2 - m1)
    den = 1.0 + t
    w1 = g_w / den
    w2 = g_w * t / den
    yield

    sel1 = lane_f == i1
    sel2 = lane_f == i2
    onehot = (sel1 | sel2).astype(BF16)
    row_i = lax.broadcasted_iota(jnp.int32, (tm, tm), 0)
    col_i = lax.broadcasted_iota(jnp.int32, (tm, tm), 1)
    before = (col_i < row_i).astype(BF16)
    prior = _dot(before, onehot) + carry[0:1, :]
    r1 = jnp.sum(jnp.where(sel1, prior, 0.0), axis=-1, keepdims=True)
    r2 = jnp.sum(jnp.where(sel2, prior, 0.0), axis=-1, keepdims=True)
    carry[...] = carry[...] + jnp.where(valid, jnp.sum(onehot.astype(F32), axis=0, keepdims=True), 0.0)
    cnt_ref[...] = carry[...].astype(jnp.int32)

    route = jnp.where(lane == 0, i1, 0.0)
    for k, col in enumerate((i2, r1, r2, w1, w2), start=1):
        route = jnp.where(lane == k, col, route)
    route_ref[...] = route
    ri_ref[0] = route.T[0:8, :].astype(jnp.int32)
    yield


def _mixer(x2, w_in_b, b_gate3, conv_w, wco_b, wpool_b, pool_scale, kt, v, wxo_b, wo_b, ln_g, ln_b, wr, br,
           bn, seq):
    n, d = x2.shape
    tm = TM_MIX
    spb = seq // tm
    nt = bn * spb
    const2 = lambda t: (0, 0)
    const3 = lambda t: (0, 0, 0)
    one = pl.Buffered(1)
    mixed = lambda t: jnp.minimum(t, nt - 1)
    routed = lambda t: jnp.maximum(t - 1, 0)
    mem_map = lambda t: (mixed(t) // spb, 0, 0)
    row_map = lambda t: (routed(t), 0)
    return pl.pallas_call(
        functools.partial(_mixer_kernel, tiles_per_seq=spb),
        grid=(nt + 1,),
        in_specs=[
            pl.BlockSpec((tm, d), lambda t: (mixed(t), 0)),
            pl.BlockSpec(w_in_b.shape, const2, pipeline_mode=one),
            pl.BlockSpec(b_gate3.shape, const2, pipeline_mode=one),
            pl.BlockSpec(conv_w.shape, const2, pipeline_mode=one),
            pl.BlockSpec(wco_b.shape, const2, pipeline_mode=one),
            pl.BlockSpec(wpool_b.shape, const3, pipeline_mode=one),
            pl.BlockSpec(pool_scale.shape, const2, pipeline_mode=one),
            pl.BlockSpec((1,) + kt.shape[1:], mem_map),
            pl.BlockSpec((1,) + v.shape[1:], mem_map),
            pl.BlockSpec(wxo_b.shape, const2, pipeline_mode=one),
            pl.BlockSpec(wo_b.shape, const2, pipeline_mode=one),
            pl.BlockSpec(ln_g.shape, const2, pipeline_mode=one),
            pl.BlockSpec(ln_b.shape, const2, pipeline_mode=one),
            pl.BlockSpec(wr.shape, const2, pipeline_mode=one),
            pl.BlockSpec(br.shape, const2, pipeline_mode=one),
        ],
        out_specs=[
            pl.BlockSpec((tm * SUB, LANES), row_map),
            pl.BlockSpec((tm, LANES), row_map),
            pl.BlockSpec((1, 8, tm), lambda t: (routed(t), 0, 0)),
            pl.BlockSpec((8, LANES), const2),
        ],
        out_shape=[
            jax.ShapeDtypeStruct((n * SUB, LANES), F32),
            jax.ShapeDtypeStruct((n, LANES), F32),
            jax.ShapeDtypeStruct((n // tm, 8, tm), jnp.int32),
            jax.ShapeDtypeStruct((8, LANES), jnp.int32),
        ],
        scratch_shapes=[
            pltpu.VMEM((tm + CONV_HALO, d), F32),
            pltpu.VMEM((POOL_PAD + POOL_HALO + tm, d), F32),
            pltpu.VMEM((POOL_PAD + POOL_HALO + tm, d - POOL_GROUP_DIM), F32),
            pltpu.VMEM((POOL_PAD + POOL_HALO + tm, d - 2 * POOL_GROUP_DIM), F32),
            pltpu.VMEM((POOL_PAD + POOL_HALO + tm, d - 3 * POOL_GROUP_DIM), F32),
            pltpu.VMEM((8, LANES), F32),
            pltpu.VMEM((tm, d), F32),
        ],
        compiler_params=pltpu.CompilerParams(dimension_semantics=("arbitrary",), vmem_limit_bytes=VMEM_LIMIT),
        name="mixer",
    )(x2, w_in_b, b_gate3, conv_w, wco_b, wpool_b, pool_scale, kt, v, wxo_b, wo_b, ln_g, ln_b, wr, br)


def _dest_kernel(ps_ref, ri_ref, dest_ref):
    dest_ref[...] = jnp.zeros(dest_ref.shape, jnp.int32)
    for t in range(dest_ref.shape[0]):
        ri = ri_ref[t]
        e = ri[0:TOP_K, :]
        start = jnp.zeros_like(e)
        for j in range(N_EXPERTS):
            start = jnp.where(e == j, ps_ref[j], start)
        dest_ref[t, 0:TOP_K, :] = start + ri[TOP_K:2 * TOP_K, :]


def _dest(pstarts, route_i):
    nt, rows, tm = route_i.shape
    blk = 8
    spec = pl.BlockSpec((blk, rows, tm), lambda i, ps: (i, 0, 0))
    return pl.pallas_call(
        _dest_kernel,
        grid_spec=pltpu.PrefetchScalarGridSpec(
            num_scalar_prefetch=1, grid=(nt // blk,), in_specs=[spec], out_specs=spec),
        out_shape=jax.ShapeDtypeStruct(route_i.shape, jnp.int32),
        compiler_params=pltpu.CompilerParams(dimension_semantics=("arbitrary",)),
        name="dest_rows",
    )(pstarts, route_i)


def _scatter_kernel(zt_ref, dest_ref, h_ref, xbuf_ref, zbuf, stage, zsem, fsem, ssem):
    i = pl.program_id(0)
    nsteps = pl.num_programs(0)
    tm = TM_ROW
    rows = tm * SUB
    slot = lax.rem(i, SCATTER_RING)

    def zero_copy(e):
        start = pl.multiple_of(zt_ref[e] * SUB, TM_EXP * SUB)
        return pltpu.make_async_copy(zbuf, xbuf_ref.at[pl.ds(start, TM_EXP * SUB), :], zsem.at[0])

    def fetch(t, s):
        src = h_ref.at[pl.ds(pl.multiple_of(t * rows, rows), rows), :]
        return pltpu.make_async_copy(src, stage.at[s], fsem.at[s])

    def drain(s):
        for k in range(TOP_K):
            pltpu.make_async_copy(stage.at[s], xbuf_ref.at[pl.ds(0, rows), :], ssem.at[s, k]).wait()

    @pl.when(i == 0)
    def _():
        zbuf[...] = jnp.zeros_like(zbuf)

        def start(e, c):
            @pl.when(zt_ref[e] >= 0)
            def _():
                zero_copy(e).start()
            return c

        def wait(e, c):
            @pl.when(zt_ref[e] >= 0)
            def _():
                zero_copy(e).wait()
            return c

        lax.fori_loop(0, 2 * N_EXPERTS, start, 0)
        fetch(0, 0).start()
        lax.fori_loop(0, 2 * N_EXPERTS, wait, 0)

    @pl.when(i >= 2)
    def _():
        drain(lax.rem(i + 1, SCATTER_RING))

    @pl.when(i + 1 < nsteps)
    def _():
        fetch(i + 1, lax.rem(i + 1, SCATTER_RING)).start()

    fetch(i, slot).wait()

    def issue(r, c):
        for k in range(TOP_K):
            _tile_copy(stage.at[slot], r, xbuf_ref, dest_ref[k * tm + r], ssem.at[slot, k]).start()
        return c

    lax.fori_loop(0, tm, issue, 0, unroll=8)

    @pl.when(i == nsteps - 1)
    def _():
        @pl.when(i >= 1)
        def _():
            drain(lax.rem(i + 2, SCATTER_RING))

        drain(slot)


def _scatter(zero_tiles, dest_flat, h1t, p_rows):
    tm = TM_ROW
    nt = dest_flat.shape[0] // (TOP_K * tm)
    return pl.pallas_call(
        _scatter_kernel,
        grid_spec=pltpu.PrefetchScalarGridSpec(
            num_scalar_prefetch=1,
            grid=(nt,),
            in_specs=[
                pl.BlockSpec((TOP_K * tm,), lambda i, zt: (i,), memory_space=pltpu.SMEM),
                pl.BlockSpec(memory_space=pl.ANY),
            ],
            out_specs=pl.BlockSpec(memory_space=pl.ANY),
            scratch_shapes=[
                pltpu.VMEM((TM_EXP * SUB, LANES), F32),
                pltpu.VMEM((SCATTER_RING, tm * SUB, LANES), F32),
                pltpu.SemaphoreType.DMA((1,)),
                pltpu.SemaphoreType.DMA((SCATTER_RING,)),
                pltpu.SemaphoreType.DMA((SCATTER_RING, TOP_K)),
            ],
        ),
        out_shape=jax.ShapeDtypeStruct((p_rows * SUB, LANES), F32),
        compiler_params=pltpu.CompilerParams(dimension_semantics=("arbitrary",)),
        name="scatter_rows",
    )(zero_tiles, dest_flat, h1t)


def _expert_kernel(t0_ref, nt_ref, tot_ref, wup_ref, wdn_ref, xbuf_ref, ybuf_ref, wup_b, wdn_b, xs, ys, sem_in, sem_out):
    e = pl.program_id(0)
    t0 = t0_ref[e]
    nt = nt_ref[e]
    total = tot_ref[0]
    rows = TM_EXP * SUB
    max_tiles = xbuf_ref.shape[0] // rows

    def hbm_tile(ref, g):
        return ref.at[pl.ds(pl.multiple_of(g * rows, rows), rows), :]

    def x_copy(g, slot):
        return pltpu.make_async_copy(hbm_tile(xbuf_ref, g), xs.at[slot], sem_in.at[slot])

    def y_copy(g, slot):
        return pltpu.make_async_copy(ys.at[slot], hbm_tile(ybuf_ref, g), sem_out.at[slot])

    ahead = RING - PAIR

    @pl.when(e == 0)
    def _():
        for g0 in range(ahead):
            @pl.when(g0 < total)
            def _():
                x_copy(g0, g0).start(priority=ROW_DMA_PRIORITY)

    @pl.when(nt > 0)
    def _():
        wup_b[...] = wup_ref[0].astype(BF16)
        wdn_b[...] = wdn_ref[0].astype(BF16)

    def process(g, width):
        slots = [lax.rem(g + w, RING) for w in range(width)]
        for w in range(width):
            @pl.when(g + w + ahead < total)
            def _(w=w):
                x_copy(g + w + ahead, lax.rem(g + w + ahead, RING)).start(priority=ROW_DMA_PRIORITY)

        for w in range(width):
            x_copy(g + w, slots[w]).wait()

            @pl.when(g + w >= RING)
            def _(w=w):
                y_copy(g + w - RING, slots[w]).wait()

        xb = jnp.concatenate([_tok_load(xs.at[s], TM_EXP) for s in slots], axis=0).astype(BF16)
        hgv = _dot(xb, wup_b[...])
        hg = hgv[:, :D_EXPERT]
        hv = hgv[:, D_EXPERT:]
        act = (hg * jax.nn.sigmoid(hg)) * hv
        y = _dot(act.astype(BF16), wdn_b[...])
        for w in range(width):
            _tok_store(ys.at[slots[w]], y[w * TM_EXP:(w + 1) * TM_EXP], TM_EXP)
            y_copy(g + w, slots[w]).start(priority=ROW_DMA_PRIORITY)

    def pair(jp, c):
        process(t0 + PAIR * jp, PAIR)
        return c

    lax.fori_loop(0, nt // PAIR, pair, 0)

    @pl.when(lax.rem(nt, PAIR) == 1)
    def _():
        process(t0 + nt - 1, 1)

    @pl.when(e == pl.num_programs(0) - 1)
    def _():
        for back in range(RING, 0, -1):
            @pl.when(total >= back)
            def _():
                y_copy(total - back, lax.rem(total - back, RING)).wait()

        ys[0] = jnp.zeros(ys.shape[1:], F32)

        def zero_tail(g, c):
            y_copy(g, 0).start()
            y_copy(g, 0).wait()
            return c

        lax.fori_loop(total, max_tiles, zero_tail, 0)


def _experts(tile_start, tile_count, n_tiles, xbuf, w_up, w_down):
    w_map = lambda e, t0, nt, tot: (e, 0, 0)
    rows = TM_EXP * SUB
    return pl.pallas_call(
        _expert_kernel,
        grid_spec=pltpu.PrefetchScalarGridSpec(
            num_scalar_prefetch=3,
            grid=(N_EXPERTS,),
            in_specs=[
                pl.BlockSpec((1,) + w_up.shape[1:], w_map),
                pl.BlockSpec((1,) + w_down.shape[1:], w_map),
                pl.BlockSpec(memory_space=pl.ANY),
            ],
            out_specs=pl.BlockSpec(memory_space=pl.ANY),
            scratch_shapes=[
                pltpu.VMEM(w_up.shape[1:], BF16), pltpu.VMEM(w_down.shape[1:], BF16),
                pltpu.VMEM((RING, rows, LANES), F32), pltpu.VMEM((RING, rows, LANES), F32),
                pltpu.SemaphoreType.DMA((RING,)), pltpu.SemaphoreType.DMA((RING,)),
            ],
        ),
        out_shape=jax.ShapeDtypeStruct(xbuf.shape, F32),
        compiler_params=pltpu.CompilerParams(dimension_semantics=("arbitrary",), vmem_limit_bytes=VMEM_LIMIT),
        name="experts",
    )(tile_start, tile_count, n_tiles, w_up, w_down, xbuf)


def _combine_kernel(dcur_ref, dnext_ref, h_ref, route_ref, ybuf_ref, g_ref, b_ref, out_ref, ybufs, sem):
    tm = TM_ROW
    i = pl.program_id(0)
    slot = lax.rem(i, 2)

    def gather(d_ref, s):
        def issue(r, c):
            for k in range(TOP_K):
                _tile_copy(ybuf_ref, d_ref[k * tm + r], ybufs.at[s, k], r, sem.at[s, k]).start()
            return c

        lax.fori_loop(0, tm, issue, 0, unroll=8)

    @pl.when(i == 0)
    def _():
        gather(dcur_ref, 0)

    @pl.when(i + 1 < pl.num_programs(0))
    def _():
        gather(dnext_ref, 1 - slot)

    for k in range(TOP_K):
        pltpu.make_async_copy(ybuf_ref.at[pl.ds(0, tm * SUB), :], ybufs.at[slot, k], sem.at[slot, k]).wait()
    route = route_ref[...]
    ffn = route[:, 4:5] * _tok_load(ybufs.at[slot, 0], tm) + route[:, 5:6] * _tok_load(ybufs.at[slot, 1], tm)
    out_ref[...] = _layer_norm(ALPHA * _tok_load(h_ref, tm) + ffn, g_ref[...], b_ref[...])


def _combine(dest_flat, h1t, route, ybuf, ln_g, ln_b):
    tm = TM_ROW
    nt = dest_flat.shape[0] // (TOP_K * tm)
    n = nt * tm
    return pl.pallas_call(
        _combine_kernel,
        grid=(nt,),
        in_specs=[
            pl.BlockSpec((TOP_K * tm,), lambda i: (i,), memory_space=pltpu.SMEM),
            pl.BlockSpec((TOP_K * tm,), lambda i: (jnp.minimum(i + 1, nt - 1),), memory_space=pltpu.SMEM),
            pl.BlockSpec((tm * SUB, LANES), lambda i: (i, 0)),
            pl.BlockSpec((tm, LANES), lambda i: (i, 0)),
            pl.BlockSpec(memory_space=pl.ANY),
            pl.BlockSpec(ln_g.shape, lambda i: (0, 0)),
            pl.BlockSpec(ln_b.shape, lambda i: (0, 0)),
        ],
        out_specs=pl.BlockSpec((tm, D_MODEL), lambda i: (i, 0)),
        out_shape=jax.ShapeDtypeStruct((n, D_MODEL), F32),
        scratch_shapes=[pltpu.VMEM((2, TOP_K, tm * SUB, LANES), F32), pltpu.SemaphoreType.DMA((2, TOP_K))],
        compiler_params=pltpu.CompilerParams(dimension_semantics=("arbitrary",)),
        name="combine_ln2",
    )(dest_flat, dest_flat, h1t, route, ybuf, ln_g, ln_b)


def _layer(h, mem, w_in, b_gate, conv_w, w_conv_out, w_pool, pool_scale, w_kv, w_xo, w_o, ln1_g, ln1_b,
           w_rg, b_rg, w_re, b_re, w_up, w_down, ln2_g, ln2_b):
    bn, seq, d = h.shape
    n = bn * seq
    xw = X_HEADS * X_HEAD_DIM

    kt, v = _kv_proj(mem, w_kv[:, :xw].T.astype(BF16), w_kv[:, xw:].astype(BF16))

    pad = LANES - N_EXPERTS - N_GROUPS
    w_r = jnp.concatenate([w_re, w_rg, jnp.zeros((d, pad), F32)], axis=1)
    wr_hi = w_r.astype(BF16)
    wr_lo = (w_r - wr_hi.astype(F32)).astype(BF16)
    b_r = jnp.concatenate([b_re, b_rg, jnp.zeros((pad,), F32)])[None, :]

    h1t, route, route_i, cnt = _mixer(
        h.reshape(n, d), w_in.astype(BF16), b_gate.reshape(3, d), conv_w, w_conv_out.astype(BF16),
        w_pool.astype(BF16), pool_scale[None, :], kt, v, w_xo.astype(BF16), w_o.astype(BF16),
        ln1_g[None, :], ln1_b[None, :], jnp.concatenate([wr_hi, wr_lo], axis=1), b_r, bn, seq)

    counts = cnt[0, :N_EXPERTS]
    padded = (counts + TM_EXP - 1) // TM_EXP * TM_EXP
    pends = jnp.cumsum(padded)
    pstarts = pends - padded
    max_tiles = (n * TOP_K + N_EXPERTS * (TM_EXP - 1)) // TM_EXP
    p_rows = max_tiles * TM_EXP
    n_tiles = (pends[-1] // TM_EXP).astype(jnp.int32)
    tail_ids = n_tiles + jnp.arange(N_EXPERTS, dtype=jnp.int32)
    zero_tiles = jnp.concatenate([
        jnp.where(counts % TM_EXP != 0, pends - TM_EXP, -1),
        jnp.where(tail_ids < max_tiles, tail_ids * TM_EXP, -1)]).astype(jnp.int32)

    dest = _dest(pstarts.astype(jnp.int32), route_i)[:, :TOP_K, :].reshape(-1)
    xbuf = _scatter(zero_tiles, dest, h1t, p_rows)
    ybuf = _experts((pstarts // TM_EXP).astype(jnp.int32), (padded // TM_EXP).astype(jnp.int32), n_tiles.reshape(1),
                    xbuf, w_up, w_down)
    out = _combine(dest, h1t, route, ybuf, ln2_g[None, :], ln2_b[None, :])
    return out.reshape(bn, seq, d)


def kernel(x, mem, w_in, b_gate, conv_w, w_conv_out, w_pool, pool_scale, w_kv, w_xo, w_o, ln1_g, ln1_b,
           w_router_group, b_router_group, w_router_expert, b_router_expert, w_up, w_down, ln2_g, ln2_b):
    h = x
    for l in range(DEPTH):
        h = _layer(h, mem, w_in[l], b_gate[l], conv_w[l], w_conv_out[l], w_pool[l], pool_scale[l], w_kv[l],
                   w_xo[l], w_o[l], ln1_g[l], ln1_b[l], w_router_group[l], b_router_group[l],
                   w_router_expert[l], b_router_expert[l], w_up[l], w_down[l], ln2_g[l], ln2_b[l])
    return h
```

```python
import functools

import jax
import jax.numpy as jnp
from jax import lax
from jax.experimental import pallas as pl
from jax.experimental.pallas import tpu as pltpu

D_MODEL = 1024
CONV_K = 3
POOL_WINDOWS = (2, 4, 8, 16)
POOL_GROUP_DIM = 256
X_HEADS = 4
X_HEAD_DIM = 256
N_GROUPS = 8
EXPERTS_PER_GROUP = 8
N_EXPERTS = 64
TOP_K = 2
D_EXPERT = 512
DEPTH = 1
ALPHA = (2.0 * DEPTH) ** 0.25
LN_EPS = 1e-5

LANES = 128
SUB = 8
assert D_MODEL == SUB * LANES
POOL_HALO = 16
POOL_PAD = 8
assert POOL_WINDOWS[0] == 2 and all(b == 2 * a for a, b in zip(POOL_WINDOWS, POOL_WINDOWS[1:]))
assert POOL_PAD >= POOL_WINDOWS[-2] and POOL_HALO >= POOL_WINDOWS[-1] and len(POOL_WINDOWS) == 4
CONV_HALO = 8
TM_MIX = 512
TM_EXP = 256
TM_ROW = TM_MIX
VMEM_LIMIT = 58 * 1024 * 1024
SCATTER_RING = 3
PAIR = 2
RING = 6
assert PAIR == 2 and RING > PAIR
ROW_DMA_PRIORITY = 1

F32 = jnp.float32
BF16 = jnp.bfloat16


def _dot(a, b):
    return jnp.dot(a, b, preferred_element_type=F32)


def _tok_load(ref, tm):
    return jnp.concatenate([ref[pl.ds(c, tm, stride=SUB), :] for c in range(SUB)], axis=1)


def _tok_store(ref, val, tm):
    for c in range(SUB):
        ref[pl.ds(c, tm, stride=SUB), :] = val[:, c * LANES:(c + 1) * LANES]


def _tile_copy(src, src_tok, dst, dst_tok, sem):
    s = pl.multiple_of(src_tok * SUB, SUB)
    t = pl.multiple_of(dst_tok * SUB, SUB)
    return pltpu.make_async_copy(src.at[pl.ds(s, SUB), :], dst.at[pl.ds(t, SUB), :], sem)


def _layer_norm(h, g, b):
    mu = jnp.mean(h, axis=-1, keepdims=True)
    c = h - mu
    var = jnp.mean(c * c, axis=-1, keepdims=True)
    return c * lax.rsqrt(var + LN_EPS) * g + b


def _kv_kernel(mem_ref, wkt_ref, wv_ref, kt_ref, v_ref):
    mb = mem_ref[0].astype(BF16)
    kt = lax.dot_general(wkt_ref[...], mb, (((1,), (1,)), ((), ())), preferred_element_type=F32)
    kt_ref[0] = kt.astype(BF16)
    v_ref[0] = _dot(mb, wv_ref[...]).astype(BF16)


def _kv_proj(mem, wkt, wv):
    bn, m, d = mem.shape
    return pl.pallas_call(
        _kv_kernel,
        grid=(bn,),
        in_specs=[
            pl.BlockSpec((1, m, d), lambda b: (b, 0, 0)),
            pl.BlockSpec((d, d), lambda b: (0, 0)),
            pl.BlockSpec((d, d), lambda b: (0, 0)),
        ],
        out_specs=[
            pl.BlockSpec((1, d, m), lambda b: (b, 0, 0)),
            pl.BlockSpec((1, m, d), lambda b: (b, 0, 0)),
        ],
        out_shape=[jax.ShapeDtypeStruct((bn, d, m), BF16), jax.ShapeDtypeStruct((bn, m, d), BF16)],
        compiler_params=pltpu.CompilerParams(dimension_semantics=("arbitrary",)),
        name="kv_proj",
    )(mem, wkt, wv)


def _mixer_kernel(x_ref, win_ref, bg_ref, cw_ref, wco_ref, wpool_ref, ps_ref, kt_ref, v_ref, wxo_ref, wo_ref,
                  g1_ref, b1_ref, wr_ref, br_ref,
                  h1_ref, route_ref, ri_ref, cnt_ref,
                  ubuf, pbuf, s1buf, s2buf, s3buf, carry, hpre, *, tiles_per_seq):
    tm = TM_MIX
    d = D_MODEL
    t = pl.program_id(0)
    s = lax.rem(t, tiles_per_seq)

    @pl.when(s == 0)
    def _():
        ubuf[0:CONV_HALO, :] = jnp.zeros((CONV_HALO, d), F32)
        pbuf[0:POOL_PAD + POOL_HALO, :] = jnp.zeros((POOL_PAD + POOL_HALO, d), F32)
        for sbuf in (s1buf, s2buf, s3buf):
            sbuf[0:POOL_PAD, :] = jnp.zeros((POOL_PAD, sbuf.shape[1]), F32)

    @pl.when(t == 0)
    def _():
        carry[...] = jnp.zeros_like(carry)
        hpre[...] = jnp.zeros_like(hpre)

    route_args = (hpre, g1_ref, b1_ref, wr_ref, br_ref, h1_ref, route_ref, ri_ref, cnt_ref, carry)
    last = pl.num_programs(0) - 1

    @pl.when(t < last)
    def _():
        _mix_tile(t, s, x_ref, win_ref, bg_ref, cw_ref, wco_ref, wpool_ref, ps_ref, kt_ref, v_ref, wxo_ref, wo_ref,
                  ubuf, pbuf, s1buf, s2buf, s3buf, route_args)

    @pl.when(t == last)
    def _():
        for _ in _route_stages(True, *route_args):
            pass


def _mix_tile(t, s, x_ref, win_ref, bg_ref, cw_ref, wco_ref, wpool_ref, ps_ref, kt_ref, v_ref, wxo_ref, wo_ref,
              ubuf, pbuf, s1buf, s2buf, s3buf, route_args):
    tm = TM_MIX
    d = D_MODEL
    hpre = route_args[0]
    routing = _route_stages(t >= 1, *route_args)

    x = x_ref[...]
    xb = x.astype(BF16)

    def proj(sec):
        return _dot(xb, win_ref[:, sec * d:(sec + 1) * d])

    def gate(i):
        return jax.nn.sigmoid(proj(5 + i) + bg_ref[i:i + 1, :])

    u = proj(1) * proj(2)
    next(routing)
    ubuf[CONV_HALO:CONV_HALO + tm, :] = u
    v = cw_ref[CONV_K - 1:CONV_K, :] * u
    for k in range(CONV_K - 1):
        off = CONV_HALO - (CONV_K - 1) + k
        v = v + cw_ref[k:k + 1, :] * ubuf[off:off + tm, :]
    ubuf[0:CONV_HALO, :] = ubuf[tm:tm + CONV_HALO, :]
    y_conv = _dot((proj(0) * v).astype(BF16), wco_ref[...])
    acc = gate(0) * y_conv
    next(routing)

    p = proj(3)
    base = POOL_PAD + POOL_HALO
    n = POOL_HALO + tm
    pbuf[base:base + tm, :] = p
    lvl = pbuf[POOL_PAD:POOL_PAD + n, :] + pbuf[POOL_PAD - 1:POOL_PAD - 1 + n, :]
    sums = [lvl[POOL_HALO:, :POOL_GROUP_DIM]]
    for gi, sbuf in enumerate((s1buf, s2buf, s3buf), start=1):
        shift = POOL_WINDOWS[gi - 1]
        rest = lvl[:, POOL_GROUP_DIM:]
        sbuf[POOL_PAD:POOL_PAD + n, :] = rest
        lvl = rest + sbuf[POOL_PAD - shift:POOL_PAD - shift + n, :]
        sums.append(lvl[POOL_HALO:, :POOL_GROUP_DIM])
    pos = (s * tm + 1 + lax.broadcasted_iota(jnp.int32, (tm, 1), 0)).astype(F32)
    ys = []
    for gi, w in enumerate(POOL_WINDOWS):
        c0 = gi * POOL_GROUP_DIM
        inv_cnt = 1.0 / jnp.minimum(pos, float(w))
        dg = (sums[gi] * inv_cnt - p[:, c0:c0 + POOL_GROUP_DIM]).astype(BF16)
        ys.append(_dot(dg, wpool_ref[gi]))
    pbuf[POOL_PAD:base, :] = pbuf[POOL_PAD + tm:base + tm, :]
    y_pool = jnp.concatenate(ys, axis=1) * ps_ref[...]
    acc = acc + gate(1) * y_pool
    next(routing)

    qb = proj(4).astype(BF16)
    scale = X_HEAD_DIM ** -0.5
    os_ = []
    for h in range(X_HEADS):
        c0 = h * X_HEAD_DIM
        c1 = c0 + X_HEAD_DIM
        sc = _dot(qb[:, c0:c1], kt_ref[0, c0:c1, :]) * scale
        e = jnp.exp(sc - jnp.max(sc, axis=-1, keepdims=True))
        a = e * (1.0 / jnp.sum(e, axis=-1, keepdims=True))
        os_.append(_dot(a.astype(BF16), v_ref[0, :, c0:c1]))
    y_mem = _dot(jnp.concatenate(os_, axis=1).astype(BF16), wxo_ref[...])
    acc = acc + gate(2) * y_mem
    next(routing)

    hpre[...] = ALPHA * x + _dot(acc.astype(BF16), wo_ref[...])


def _route_stages(valid, hpre, g1_ref, b1_ref, wr_ref, br_ref, h1_ref, route_ref, ri_ref, cnt_ref, carry):
    tm = TM_MIX
    h1 = _layer_norm(hpre[...], g1_ref[...], b1_ref[...])
    _tok_store(h1_ref, h1, tm)
    yield

    hh = h1.astype(BF16)
    hl = (h1 - hh.astype(F32)).astype(BF16)
    l2 = _dot(hh, wr_ref[...]) + _dot(hl, wr_ref[...])
    logits = l2[:, :LANES] + l2[:, LANES:] + br_ref[...]
    yield
    lane = lax.broadcasted_iota(jnp.int32, (tm, LANES), 1)
    lane_f = lane.astype(F32)
    neg = jnp.float32(-jnp.inf)
    big = jnp.float32(1e9)

    def first_argmax(vals):
        m = jnp.max(vals, axis=-1, keepdims=True)
        idx = jnp.min(jnp.where(vals == m, lane_f, big), axis=-1, keepdims=True)
        return m, idx

    is_g = (lane >= N_EXPERTS) & (lane < N_EXPERTS + N_GROUPS)
    gmax, gidx = first_argmax(jnp.where(is_g, logits, neg))
    g_w = 1.0 / jnp.sum(jnp.where(is_g, jnp.exp(logits - gmax), 0.0), axis=-1, keepdims=True)
    gsel = gidx.astype(jnp.int32) - N_EXPERTS
    in_grp = (lane >> 3) == gsel
    le = jnp.where(in_grp, logits, neg)
    m1, i1 = first_argmax(le)
    m2, i2 = first_argmax(jnp.where(lane_f == i1, neg, le))
    t = jnp.exp(m2 - m1)
    den = 1.0 + t
    w1 = g_w / den
    w2 = g_w * t / den
    yield

    sel1 = lane_f == i1
    sel2 = lane_f == i2
    onehot = (sel1 | sel2).astype(BF16)
    row_i = lax.broadcasted_iota(jnp.int32, (tm, tm), 0)
    col_i = lax.broadcasted_iota(jnp.int32, (tm, tm), 1)
    before = (col_i < row_i).astype(BF16)
    prior = _dot(before, onehot) + carry[0:1, :]
    r1 = jnp.sum(jnp.where(sel1, prior, 0.0), axis=-1, keepdims=True)
    r2 = jnp.sum(jnp.where(sel2, prior, 0.0), axis=-1, keepdims=True)
    carry[...] = carry[...] + jnp.where(valid, jnp.sum(onehot.astype(F32), axis=0, keepdims=True), 0.0)
    cnt_ref[...] = carry[...].astype(jnp.int32)

    route = jnp.where(lane == 0, i1, 0.0)
    for k, col in enumerate((i2, r1, r2, w1, w2), start=1):
        route = jnp.where(lane == k, col, route)
    route_ref[...] = route
    ri_ref[0] = route.T[0:8, :].astype(jnp.int32)
    yield


def _mixer(x2, w_in_b, b_gate3, conv_w, wco_b, wpool_b, pool_scale, kt, v, wxo_b, wo_b, ln_g, ln_b, wr, br,
           bn, seq):
    n, d = x2.shape
    tm = TM_MIX
    spb = seq // tm
    nt = bn * spb
    const2 = lambda t: (0, 0)
    const3 = lambda t: (0, 0, 0)
    one = pl.Buffered(1)
    mixed = lambda t: jnp.minimum(t, nt - 1)
    routed = lambda t: jnp.maximum(t - 1, 0)
    mem_map = lambda t: (mixed(t) // spb, 0, 0)
    row_map = lambda t: (routed(t), 0)
    return pl.pallas_call(
        functools.partial(_mixer_kernel, tiles_per_seq=spb),
        grid=(nt + 1,),
        in_specs=[
            pl.BlockSpec((tm, d), lambda t: (mixed(t), 0)),
            pl.BlockSpec(w_in_b.shape, const2, pipeline_mode=one),
            pl.BlockSpec(b_gate3.shape, const2, pipeline_mode=one),
            pl.BlockSpec(conv_w.shape, const2, pipeline_mode=one),
            pl.BlockSpec(wco_b.shape, const2, pipeline_mode=one),
            pl.BlockSpec(wpool_b.shape, const3, pipeline_mode=one),
            pl.BlockSpec(pool_scale.shape, const2, pipeline_mode=one),
            pl.BlockSpec((1,) + kt.shape[1:], mem_map),
            pl.BlockSpec((1,) + v.shape[1:], mem_map),
            pl.BlockSpec(wxo_b.shape, const2, pipeline_mode=one),
            pl.BlockSpec(wo_b.shape, const2, pipeline_mode=one),
            pl.BlockSpec(ln_g.shape, const2, pipeline_mode=one),
            pl.BlockSpec(ln_b.shape, const2, pipeline_mode=one),
            pl.BlockSpec(wr.shape, const2, pipeline_mode=one),
            pl.BlockSpec(br.shape, const2, pipeline_mode=one),
        ],
        out_specs=[
            pl.BlockSpec((tm * SUB, LANES), row_map),
            pl.BlockSpec((tm, LANES), row_map),
            pl.BlockSpec((1, 8, tm), lambda t: (routed(t), 0, 0)),
            pl.BlockSpec((8, LANES), const2),
        ],
        out_shape=[
            jax.ShapeDtypeStruct((n * SUB, LANES), F32),
            jax.ShapeDtypeStruct((n, LANES), F32),
            jax.ShapeDtypeStruct((n // tm, 8, tm), jnp.int32),
            jax.ShapeDtypeStruct((8, LANES), jnp.int32),
        ],
        scratch_shapes=[
            pltpu.VMEM((tm + CONV_HALO, d), F32),
            pltpu.VMEM((POOL_PAD + POOL_HALO + tm, d), F32),
            pltpu.VMEM((POOL_PAD + POOL_HALO + tm, d - POOL_GROUP_DIM), F32),
            pltpu.VMEM((POOL_PAD + POOL_HALO + tm, d - 2 * POOL_GROUP_DIM), F32),
            pltpu.VMEM((POOL_PAD + POOL_HALO + tm, d - 3 * POOL_GROUP_DIM), F32),
            pltpu.VMEM((8, LANES), F32),
            pltpu.VMEM((tm, d), F32),
        ],
        compiler_params=pltpu.CompilerParams(dimension_semantics=("arbitrary",), vmem_limit_bytes=VMEM_LIMIT),
        name="mixer",
    )(x2, w_in_b, b_gate3, conv_w, wco_b, wpool_b, pool_scale, kt, v, wxo_b, wo_b, ln_g, ln_b, wr, br)


def _dest_kernel(ps_ref, ri_ref, dest_ref):
    dest_ref[...] = jnp.zeros(dest_ref.shape, jnp.int32)
    for t in range(dest_ref.shape[0]):
        ri = ri_ref[t]
        e = ri[0:TOP_K, :]
        start = jnp.zeros_like(e)
        for j in range(N_EXPERTS):
            start = jnp.where(e == j, ps_ref[j], start)
        dest_ref[t, 0:TOP_K, :] = start + ri[TOP_K:2 * TOP_K, :]


def _dest(pstarts, route_i):
    nt, rows, tm = route_i.shape
    blk = 8
    spec = pl.BlockSpec((blk, rows, tm), lambda i, ps: (i, 0, 0))
    return pl.pallas_call(
        _dest_kernel,
        grid_spec=pltpu.PrefetchScalarGridSpec(
            num_scalar_prefetch=1, grid=(nt // blk,), in_specs=[spec], out_specs=spec),
        out_shape=jax.ShapeDtypeStruct(route_i.shape, jnp.int32),
        compiler_params=pltpu.CompilerParams(dimension_semantics=("arbitrary",)),
        name="dest_rows",
    )(pstarts, route_i)


def _scatter_kernel(zt_ref, dest_ref, h_ref, xbuf_ref, zbuf, stage, zsem, fsem, ssem):
    i = pl.program_id(0)
    nsteps = pl.num_programs(0)
    tm = TM_ROW
    rows = tm * SUB
    slot = lax.rem(i, SCATTER_RING)

    def zero_copy(e):
        start = pl.multiple_of(zt_ref[e] * SUB, TM_EXP * SUB)
        return pltpu.make_async_copy(zbuf, xbuf_ref.at[pl.ds(start, TM_EXP * SUB), :], zsem.at[0])

    def fetch(t, s):
        src = h_ref.at[pl.ds(pl.multiple_of(t * rows, rows), rows), :]
        return pltpu.make_async_copy(src, stage.at[s], fsem.at[s])

    def drain(s):
        for k in range(TOP_K):
            pltpu.make_async_copy(stage.at[s], xbuf_ref.at[pl.ds(0, rows), :], ssem.at[s, k]).wait()

    @pl.when(i == 0)
    def _():
        zbuf[...] = jnp.zeros_like(zbuf)

        def start(e, c):
            @pl.when(zt_ref[e] >= 0)
            def _():
                zero_copy(e).start()
            return c

        def wait(e, c):
            @pl.when(zt_ref[e] >= 0)
            def _():
                zero_copy(e).wait()
            return c

        lax.fori_loop(0, 2 * N_EXPERTS, start, 0)
        fetch(0, 0).start()
        lax.fori_loop(0, 2 * N_EXPERTS, wait, 0)

    @pl.when(i >= 2)
    def _():
        drain(lax.rem(i + 1, SCATTER_RING))

    @pl.when(i + 1 < nsteps)
    def _():
        fetch(i + 1, lax.rem(i + 1, SCATTER_RING)).start()

    fetch(i, slot).wait()

    def issue(r, c):
        for k in range(TOP_K):
            _tile_copy(stage.at[slot], r, xbuf_ref, dest_ref[k * tm + r], ssem.at[slot, k]).start(priority=k)
        return c

    lax.fori_loop(0, tm, issue, 0, unroll=8)

    @pl.when(i == nsteps - 1)
    def _():
        @pl.when(i >= 1)
        def _():
            drain(lax.rem(i + 2, SCATTER_RING))

        drain(slot)


def _scatter(zero_tiles, dest_flat, h1t, p_rows):
    tm = TM_ROW
    nt = dest_flat.shape[0] // (TOP_K * tm)
    return pl.pallas_call(
        _scatter_kernel,
        grid_spec=pltpu.PrefetchScalarGridSpec(
            num_scalar_prefetch=1,
            grid=(nt,),
            in_specs=[
                pl.BlockSpec((TOP_K * tm,), lambda i, zt: (i,), memory_space=pltpu.SMEM),
                pl.BlockSpec(memory_space=pl.ANY),
            ],
            out_specs=pl.BlockSpec(memory_space=pl.ANY),
            scratch_shapes=[
                pltpu.VMEM((TM_EXP * SUB, LANES), F32),
                pltpu.VMEM((SCATTER_RING, tm * SUB, LANES), F32),
                pltpu.SemaphoreType.DMA((1,)),
                pltpu.SemaphoreType.DMA((SCATTER_RING,)),
                pltpu.SemaphoreType.DMA((SCATTER_RING, TOP_K)),
            ],
        ),
        out_shape=jax.ShapeDtypeStruct((p_rows * SUB, LANES), F32),
        compiler_params=pltpu.CompilerParams(dimension_semantics=("arbitrary",)),
        name="scatter_rows",
    )(zero_tiles, dest_flat, h1t)


def _expert_kernel(t0_ref, nt_ref, tot_ref, wup_ref, wdn_ref, xbuf_ref, ybuf_ref, wup_b, wdn_b, xs, ys, sem_in, sem_out):
    e = pl.program_id(0)
    t0 = t0_ref[e]
    nt = nt_ref[e]
    total = tot_ref[0]
    rows = TM_EXP * SUB
    max_tiles = xbuf_ref.shape[0] // rows

    def hbm_tile(ref, g):
        return ref.at[pl.ds(pl.multiple_of(g * rows, rows), rows), :]

    def x_copy(g, slot):
        return pltpu.make_async_copy(hbm_tile(xbuf_ref, g), xs.at[slot], sem_in.at[slot])

    def y_copy(g, slot):
        return pltpu.make_async_copy(ys.at[slot], hbm_tile(ybuf_ref, g), sem_out.at[slot])

    ahead = RING - PAIR

    @pl.when(e == 0)
    def _():
        for g0 in range(ahead):
            @pl.when(g0 < total)
            def _():
                x_copy(g0, g0).start(priority=ROW_DMA_PRIORITY)

    @pl.when(nt > 0)
    def _():
        wup_b[...] = wup_ref[0].astype(BF16)
        wdn_b[...] = wdn_ref[0].astype(BF16)

    def process(g, width):
        slots = [lax.rem(g + w, RING) for w in range(width)]
        for w in range(width):
            @pl.when(g + w + ahead < total)
            def _(w=w):
                x_copy(g + w + ahead, lax.rem(g + w + ahead, RING)).start(priority=ROW_DMA_PRIORITY)

        for w in range(width):
            x_copy(g + w, slots[w]).wait()

            @pl.when(g + w >= RING)
            def _(w=w):
                y_copy(g + w - RING, slots[w]).wait()

        xb = jnp.concatenate([_tok_load(xs.at[s], TM_EXP) for s in slots], axis=0).astype(BF16)
        hgv = _dot(xb, wup_b[...])
        hg = hgv[:, :D_EXPERT]
        hv = hgv[:, D_EXPERT:]
        act = (hg * jax.nn.sigmoid(hg)) * hv
        y = _dot(act.astype(BF16), wdn_b[...])
        for w in range(width):
            _tok_store(ys.at[slots[w]], y[w * TM_EXP:(w + 1) * TM_EXP], TM_EXP)
            y_copy(g + w, slots[w]).start(priority=1 - ROW_DMA_PRIORITY)

    def pair(jp, c):
        process(t0 + PAIR * jp, PAIR)
        return c

    lax.fori_loop(0, nt // PAIR, pair, 0)

    @pl.when(lax.rem(nt, PAIR) == 1)
    def _():
        process(t0 + nt - 1, 1)

    @pl.when(e == pl.num_programs(0) - 1)
    def _():
        for back in range(RING, 0, -1):
            @pl.when(total >= back)
            def _():
                y_copy(total - back, lax.rem(total - back, RING)).wait()

        ys[0] = jnp.zeros(ys.shape[1:], F32)

        def zero_tail(g, c):
            y_copy(g, 0).start()
            y_copy(g, 0).wait()
            return c

        lax.fori_loop(total, max_tiles, zero_tail, 0)


def _experts(tile_start, tile_count, n_tiles, xbuf, w_up, w_down):
    w_map = lambda e, t0, nt, tot: (e, 0, 0)
    rows = TM_EXP * SUB
    return pl.pallas_call(
        _expert_kernel,
        grid_spec=pltpu.PrefetchScalarGridSpec(
            num_scalar_prefetch=3,
            grid=(N_EXPERTS,),
            in_specs=[
                pl.BlockSpec((1,) + w_up.shape[1:], w_map),
                pl.BlockSpec((1,) + w_down.shape[1:], w_map),
                pl.BlockSpec(memory_space=pl.ANY),
            ],
            out_specs=pl.BlockSpec(memory_space=pl.ANY),
            scratch_shapes=[
                pltpu.VMEM(w_up.shape[1:], BF16), pltpu.VMEM(w_down.shape[1:], BF16),
                pltpu.VMEM((RING, rows, LANES), F32), pltpu.VMEM((RING, rows, LANES), F32),
                pltpu.SemaphoreType.DMA((RING,)), pltpu.SemaphoreType.DMA((RING,)),
            ],
        ),
        out_shape=jax.ShapeDtypeStruct(xbuf.shape, F32),
        compiler_params=pltpu.CompilerParams(dimension_semantics=("arbitrary",), vmem_limit_bytes=VMEM_LIMIT),
        name="experts",
    )(tile_start, tile_count, n_tiles, w_up, w_down, xbuf)


def _combine_kernel(dcur_ref, dnext_ref, h_ref, route_ref, ybuf_ref, g_ref, b_ref, out_ref, ybufs, sem):
    tm = TM_ROW
    i = pl.program_id(0)
    slot = lax.rem(i, 2)

    def gather(d_ref, s):
        def issue(r, c):
            for k in range(TOP_K):
                _tile_copy(ybuf_ref, d_ref[k * tm + r], ybufs.at[s, k], r, sem.at[s, k]).start(priority=k)
            return c

        lax.fori_loop(0, tm, issue, 0, unroll=8)

    @pl.when(i == 0)
    def _():
        gather(dcur_ref, 0)

    @pl.when(i + 1 < pl.num_programs(0))
    def _():
        gather(dnext_ref, 1 - slot)

    for k in range(TOP_K):
        pltpu.make_async_copy(ybuf_ref.at[pl.ds(0, tm * SUB), :], ybufs.at[slot, k], sem.at[slot, k]).wait()
    route = route_ref[...]
    ffn = route[:, 4:5] * _tok_load(ybufs.at[slot, 0], tm) + route[:, 5:6] * _tok_load(ybufs.at[slot, 1], tm)
    out_ref[...] = _layer_norm(ALPHA * _tok_load(h_ref, tm) + ffn, g_ref[...], b_ref[...])


def _combine(dest_flat, h1t, route, ybuf, ln_g, ln_b):
    tm = TM_ROW
    nt = dest_flat.shape[0] // (TOP_K * tm)
    n = nt * tm
    return pl.pallas_call(
        _combine_kernel,
        grid=(nt,),
        in_specs=[
            pl.BlockSpec((TOP_K * tm,), lambda i: (i,), memory_space=pltpu.SMEM),
            pl.BlockSpec((TOP_K * tm,), lambda i: (jnp.minimum(i + 1, nt - 1),), memory_space=pltpu.SMEM),
            pl.BlockSpec((tm * SUB, LANES), lambda i: (i, 0)),
            pl.BlockSpec((tm, LANES), lambda i: (i, 0)),
            pl.BlockSpec(memory_space=pl.ANY),
            pl.BlockSpec(ln_g.shape, lambda i: (0, 0)),
            pl.BlockSpec(ln_b.shape, lambda i: (0, 0)),
        ],
        out_specs=pl.BlockSpec((tm, D_MODEL), lambda i: (i, 0)),
        out_shape=jax.ShapeDtypeStruct((n, D_MODEL), F32),
        scratch_shapes=[pltpu.VMEM((2, TOP_K, tm * SUB, LANES), F32), pltpu.SemaphoreType.DMA((2, TOP_K))],
        compiler_params=pltpu.CompilerParams(dimension_semantics=("arbitrary",)),
        name="combine_ln2",
    )(dest_flat, dest_flat, h1t, route, ybuf, ln_g, ln_b)


def _layer(h, mem, w_in, b_gate, conv_w, w_conv_out, w_pool, pool_scale, w_kv, w_xo, w_o, ln1_g, ln1_b,
           w_rg, b_rg, w_re, b_re, w_up, w_down, ln2_g, ln2_b):
    bn, seq, d = h.shape
    n = bn * seq
    xw = X_HEADS * X_HEAD_DIM

    kt, v = _kv_proj(mem, w_kv[:, :xw].T.astype(BF16), w_kv[:, xw:].astype(BF16))

    pad = LANES - N_EXPERTS - N_GROUPS
    w_r = jnp.concatenate([w_re, w_rg, jnp.zeros((d, pad), F32)], axis=1)
    wr_hi = w_r.astype(BF16)
    wr_lo = (w_r - wr_hi.astype(F32)).astype(BF16)
    b_r = jnp.concatenate([b_re, b_rg, jnp.zeros((pad,), F32)])[None, :]

    h1t, route, route_i, cnt = _mixer(
        h.reshape(n, d), w_in.astype(BF16), b_gate.reshape(3, d), conv_w, w_conv_out.astype(BF16),
        w_pool.astype(BF16), pool_scale[None, :], kt, v, w_xo.astype(BF16), w_o.astype(BF16),
        ln1_g[None, :], ln1_b[None, :], jnp.concatenate([wr_hi, wr_lo], axis=1), b_r, bn, seq)

    counts = cnt[0, :N_EXPERTS]
    padded = (counts + TM_EXP - 1) // TM_EXP * TM_EXP
    pends = jnp.cumsum(padded)
    pstarts = pends - padded
    max_tiles = (n * TOP_K + N_EXPERTS * (TM_EXP - 1)) // TM_EXP
    p_rows = max_tiles * TM_EXP
    n_tiles = (pends[-1] // TM_EXP).astype(jnp.int32)
    tail_ids = n_tiles + jnp.arange(N_EXPERTS, dtype=jnp.int32)
    zero_tiles = jnp.concatenate([
        jnp.where(counts % TM_EXP != 0, pends - TM_EXP, -1),
        jnp.where(tail_ids < max_tiles, tail_ids * TM_EXP, -1)]).astype(jnp.int32)

    dest = _dest(pstarts.astype(jnp.int32), route_i)[:, :TOP_K, :].reshape(-1)
    xbuf = _scatter(zero_tiles, dest, h1t, p_rows)
    ybuf = _experts((pstarts // TM_EXP).astype(jnp.int32), (padded // TM_EXP).astype(jnp.int32), n_tiles.reshape(1),
                    xbuf, w_up, w_down)
    out = _combine(dest, h1t, route, ybuf, ln2_g[None, :], ln2_b[None, :])
    return out.reshape(bn, seq, d)


def kernel(x, mem, w_in, b_gate, conv_w, w_conv_out, w_pool, pool_scale, w_kv, w_xo, w_o, ln1_g, ln1_b,
           w_router_group, b_router_group, w_router_expert, b_router_expert, w_up, w_down, ln2_g, ln2_b):
    h = x
    for l in range(DEPTH):
        h = _layer(h, mem, w_in[l], b_gate[l], conv_w[l], w_conv_out[l], w_pool[l], pool_scale[l], w_kv[l],
                   w_xo[l], w_o[l], ln1_g[l], ln1_b[l], w_router_group[l], b_router_group[l],
                   w_router_expert[l], b_router_expert[l], w_up[l], w_down[l], ln2_g[l], ln2_b[l])
    return h
```
